```python
import math
import jax, jax.numpy as jnp
from jax import lax
import numpy as np

D_MODEL = 2048
BATCH = 32
SEQ = 256
DEPTH = 4
DEC_BATCH = 8
DEC_SEQ = 2048
PAST_LEN = 512

GRID_W = 64
CHUNK_A = 128
D_A = 1024
N_A_GROUPS = 8
A_GROUP = D_A // N_A_GROUPS
N_GLA_HEADS = 4
GLA_DK = D_MODEL // 4
GLA_DV = D_MODEL // 2
GLA_DK_HEAD = GLA_DK // N_GLA_HEADS
GLA_DV_HEAD = GLA_DV // N_GLA_HEADS
GLA_RANK = 16
GLA_TAU = 16.0
GLA_CHUNK = 64
N_BRANCH = 2
D_FF = 5632
N_EXPERTS = 8
TOP_K = 2
N_EVEN = (DEPTH + 1) // 2
N_ODD = DEPTH // 2
EPS = 1e-6
IN_SIZES = (D_A, D_A, GLA_DK, GLA_DK, GLA_DV, GLA_DV, GLA_RANK, GLA_RANK, D_MODEL, D_MODEL)
N_IN = 2 * D_A + 2 * GLA_DK + 2 * GLA_DV + 2 * GLA_RANK + N_BRANCH * D_MODEL

kernel_name = 'hybrid_gmlp_gla_moe_diffusion_step'


def rmsnorm(x, g):
    x32 = x.astype(jnp.float32)
    y = x32 * lax.rsqrt(jnp.mean(x32 * x32, axis=-1, keepdims=True) + EPS)
    return (y * g).astype(x.dtype)


def grid_pos_embed(L, dtype):
    rows = L // GRID_W
    r, col = jnp.meshgrid(jnp.arange(rows, dtype=jnp.float32), jnp.arange(GRID_W, dtype=jnp.float32), indexing='ij')
    r = r.reshape(-1)
    col = col.reshape(-1)
    quarter = D_MODEL // 4
    freq = jnp.exp(-math.log(10000.0) * jnp.arange(quarter, dtype=jnp.float32) / quarter)
    ar = r[:, None] * freq
    ac = col[:, None] * freq
    emb = jnp.concatenate([jnp.sin(ar), jnp.cos(ar), jnp.sin(ac), jnp.cos(ac)], axis=-1)
    return emb.astype(dtype)


def to_heads(t, n):
    B, L, Dt = t.shape
    return t.reshape(B, L, n, Dt // n).transpose(0, 2, 1, 3)


def chunk_sgu(u, v, ln_g, ln_b, w_s, b_s):
    B, L, _ = u.shape
    n = L // CHUNK_A
    v32 = v.astype(jnp.float32)
    mu = jnp.mean(v32, axis=-1, keepdims=True)
    var = jnp.mean(jnp.square(v32 - mu), axis=-1, keepdims=True)
    vn = ((v32 - mu) * lax.rsqrt(var + EPS) * ln_g + ln_b).astype(v.dtype)
    vn = vn.reshape(B, n, CHUNK_A, N_A_GROUPS, A_GROUP)
    f = jnp.einsum('gpq,bnqgc->bnpgc', w_s, vn) + b_s.T[None, None, :, :, None]
    return u * f.reshape(B, L, D_A)


def gla_chunked(q, k, v, log_a, s0):
    B, H, L, DK = q.shape
    DV = v.shape[-1]
    C = GLA_CHUNK
    N = L // C
    q = q.reshape(B, H, N, C, DK)
    k = k.reshape(B, H, N, C, DK)
    v = v.reshape(B, H, N, C, DV)
    b = jnp.cumsum(log_a.reshape(B, H, N, C, DK), axis=3)
    b_last = b[:, :, :, -1:, :]
    qd = q * jnp.exp(b)
    kd = k * jnp.exp(-b)
    att = jnp.einsum('bhncd,bhnsd->bhncs', qd, kd)
    mask = jnp.tril(jnp.ones((C, C), dtype=bool))
    att = jnp.where(mask, att, 0.0)
    o_intra = jnp.einsum('bhncs,bhnse->bhnce', att, v)
    k_end = k * jnp.exp(b_last - b)
    dS = jnp.einsum('bhncd,bhnce->bhnde', k_end, v)
    g = jnp.exp(b_last[:, :, :, 0, :])

    def step(S, inp):
        g_n, dS_n = inp
        return g_n[..., None] * S + dS_n, S

    s_fin, s_prev = lax.scan(step, s0, (jnp.moveaxis(g, 2, 0), jnp.moveaxis(dS, 2, 0)))
    s_prev = jnp.moveaxis(s_prev, 0, 2)
    o_inter = jnp.einsum('bhncd,bhnde->bhnce', qd, s_prev)
    return (o_intra + o_inter).reshape(B, H, L, DV), s_fin


def gla_bidir(q, k, v, la_f, la_b, s0_f, s0_b):
    o_f, s_f = gla_chunked(q, k, v, la_f, s0_f)
    fl = lambda t: jnp.flip(t, axis=2)
    o_b, s_b = gla_chunked(fl(q), fl(k), fl(v), fl(la_b), s0_b)
    return o_f + fl(o_b), s_f, s_b


def mixer(h, w_in, sgu_ln_g, sgu_ln_b, w_spatial, b_spatial, gla_a2, gla_ab, gla_norm_g,
          w_branch_a, w_branch_b, w_out, s0_f, s0_b):
    dt = h.dtype
    proj = h @ w_in
    idx = np.cumsum(IN_SIZES)[:-1].tolist()
    pu, pv, pq, pk, pvg, pr, plf, plb, pga, pgb = jnp.split(proj, idx, axis=-1)
    a = chunk_sgu(jax.nn.gelu(pu), jax.nn.gelu(pv), sgu_ln_g, sgu_ln_b, w_spatial, b_spatial)
    q = to_heads(pq.astype(jnp.float32), N_GLA_HEADS) * (GLA_DK_HEAD ** -0.5)
    k = to_heads(pk.astype(jnp.float32), N_GLA_HEADS)
    v = to_heads(pvg.astype(jnp.float32), N_GLA_HEADS)
    la_f = to_heads(jax.nn.log_sigmoid((plf @ gla_a2[0] + gla_ab[0]).astype(jnp.float32)) / GLA_TAU, N_GLA_HEADS)
    la_b = to_heads(jax.nn.log_sigmoid((plb @ gla_a2[1] + gla_ab[1]).astype(jnp.float32)) / GLA_TAU, N_GLA_HEADS)
    o, s_f, s_b = gla_bidir(q, k, v, la_f, la_b, s0_f, s0_b)
    o = o * lax.rsqrt(jnp.mean(o * o, axis=-1, keepdims=True) + EPS)
    B, H, L, E = o.shape
    o = o.transpose(0, 2, 1, 3).reshape(B, L, H * E) * gla_norm_g
    o = o.astype(dt) * jax.nn.silu(pr)
    merged = jax.nn.sigmoid(pga) * (a @ w_branch_a) + jax.nn.sigmoid(pgb) * (o @ w_branch_b)
    return merged @ w_out, s_f, s_b


def swiglu(h, w1, w3, w2):
    return (jax.nn.silu(h @ w1) * (h @ w3)) @ w2


def moe(h, router_w, router_b, w1, w3, w2):
    B, L, D = h.shape
    t = h.reshape(B * L, D)
    logits = (t @ router_w).astype(jnp.float32) + router_b
    top_v, top_i = lax.top_k(logits, TOP_K)
    probs = jax.nn.softmax(top_v, axis=-1)
    combine = jnp.sum(jax.nn.one_hot(top_i, N_EXPERTS, dtype=jnp.float32) * probs[..., None], axis=1)
    combine = combine.astype(h.dtype)
    out = jnp.zeros_like(t)
    for e in range(N_EXPERTS):
        out = out + combine[:, e:e + 1] * swiglu(t, w1[e], w3[e], w2[e])
    return out.reshape(B, L, D)


def trunk(x, cond, s_init, P):
    B = x.shape[0]
    finals = []
    for l in range(DEPTH):
        mod = (jax.nn.silu(cond) @ P['w_mod'][l] + P['b_mod'][l])[:, None, :]
        sh1, sc1, g1, sh2, sc2, g2 = jnp.split(mod, 6, axis=-1)
        h = rmsnorm(x, P['norm1_g'][l]) * (1 + sc1) + sh1
        if s_init is None:
            s0_f = jnp.zeros((B, N_GLA_HEADS, GLA_DK_HEAD, GLA_DV_HEAD), jnp.float32)
            s0_b = s0_f
        else:
            s0_f = s_init[:, l, 0].astype(jnp.float32)
            s0_b = s_init[:, l, 1].astype(jnp.float32)
        y, s_f, s_b = mixer(h, P['w_in'][l], P['sgu_ln_g'][l], P['sgu_ln_b'][l], P['w_spatial'][l],
                            P['b_spatial'][l], P['gla_a2'][l], P['gla_ab'][l], P['gla_norm_g'][l],
                            P['w_branch_a'][l], P['w_branch_b'][l], P['w_out'][l], s0_f, s0_b)
        x = x + g1 * y
        h = rmsnorm(x, P['norm2_g'][l]) * (1 + sc2) + sh2
        if l % 2 == 0:
            j = l // 2
            f = swiglu(h, P['ffn_w1'][j], P['ffn_w3'][j], P['ffn_w2'][j])
        else:
            j = l // 2
            f = moe(h, P['moe_router'][j], P['moe_router_b'][j], P['moe_w1'][j], P['moe_w3'][j], P['moe_w2'][j])
        x = x + g2 * f
        if s_init is None:
            finals.append(jnp.stack([s_f, s_b], axis=1))
    states = jnp.stack(finals, axis=1) if s_init is None else None
    return rmsnorm(x, P['final_g']), states


def setup_inputs(seed: int = 0) -> dict:
    key = jax.random.key(seed)
    ks = jax.random.split(key, 32)
    f32 = jnp.float32
    nrm = lambda k, shape, s: jax.random.normal(k, shape, f32) * s
    D = D_MODEL
    return {
        'x_prompt': nrm(ks[0], (BATCH, SEQ, D), 1.0),
        'x_sample': nrm(ks[1], (DEC_BATCH, DEC_SEQ, D), 1.0),
        'state_gla': nrm(ks[2], (DEC_BATCH, DEPTH, 2, N_GLA_HEADS, GLA_DK_HEAD, GLA_DV_HEAD), 1.0),
        'c': nrm(ks[3], (DEC_BATCH, D), 1.0),
        'c_ctx': nrm(ks[4], (D,), 1.0),
        'norm1_g': 1.0 + nrm(ks[5], (DEPTH, D), 0.01),
        'norm2_g': 1.0 + nrm(ks[6], (DEPTH, D), 0.01),
        'w_mod': nrm(ks[7], (DEPTH, D, 6 * D), 0.5 * D ** -0.5),
        'b_mod': nrm(ks[8], (DEPTH, 6 * D), 0.01),
        'w_in': nrm(ks[9], (DEPTH, D, N_IN), D ** -0.5),
        'sgu_ln_g': 1.0 + nrm(ks[10], (DEPTH, D_A), 0.01),
        'sgu_ln_b': nrm(ks[11], (DEPTH, D_A), 0.01),
        'w_spatial': nrm(ks[12], (DEPTH, N_A_GROUPS, CHUNK_A, CHUNK_A), CHUNK_A ** -0.5),
        'b_spatial': 1.0 + nrm(ks[13], (DEPTH, N_A_GROUPS, CHUNK_A), 0.01),
        'gla_a2': nrm(ks[14], (DEPTH, 2, GLA_RANK, GLA_DK), GLA_RANK ** -0.5),
        'gla_ab': nrm(ks[15], (DEPTH, 2, GLA_DK), 0.01),
        'gla_norm_g': 1.0 + nrm(ks[16], (DEPTH, GLA_DV), 0.01),
        'w_branch_a': nrm(ks[17], (DEPTH, D_A, D), D_A ** -0.5),
        'w_branch_b': nrm(ks[18], (DEPTH, GLA_DV, D), GLA_DV ** -0.5),
        'w_out': nrm(ks[19], (DEPTH, D, D), D ** -0.5),
        'ffn_w1': nrm(ks[20], (N_EVEN, D, D_FF), D ** -0.5),
        'ffn_w3': nrm(ks[21], (N_EVEN, D, D_FF), D ** -0.5),
        'ffn_w2': nrm(ks[22], (N_EVEN, D_FF, D), D_FF ** -0.5),
        'moe_router': nrm(ks[23], (N_ODD, D, N_EXPERTS), D ** -0.5),
        'moe_router_b': nrm(ks[24], (N_ODD, N_EXPERTS), 0.01),
        'moe_w1': nrm(ks[25], (N_ODD, N_EXPERTS, D, D_FF), D ** -0.5),
        'moe_w3': nrm(ks[26], (N_ODD, N_EXPERTS, D, D_FF), D ** -0.5),
        'moe_w2': nrm(ks[27], (N_ODD, N_EXPERTS, D_FF, D), D_FF ** -0.5),
        'final_g': 1.0 + nrm(ks[28], (D,), 0.01),
    }


def reference(x_prompt, x_sample, state_gla, c, c_ctx, norm1_g, norm2_g, w_mod, b_mod, w_in,
              sgu_ln_g, sgu_ln_b, w_spatial, b_spatial, gla_a2, gla_ab, gla_norm_g,
              w_branch_a, w_branch_b, w_out, ffn_w1, ffn_w3, ffn_w2,
              moe_router, moe_router_b, moe_w1, moe_w3, moe_w2, final_g):
    P = dict(norm1_g=norm1_g, norm2_g=norm2_g, w_mod=w_mod, b_mod=b_mod, w_in=w_in,
             sgu_ln_g=sgu_ln_g, sgu_ln_b=sgu_ln_b, w_spatial=w_spatial, b_spatial=b_spatial,
             gla_a2=gla_a2, gla_ab=gla_ab, gla_norm_g=gla_norm_g, w_branch_a=w_branch_a,
             w_branch_b=w_branch_b, w_out=w_out, ffn_w1=ffn_w1, ffn_w3=ffn_w3, ffn_w2=ffn_w2,
             moe_router=moe_router, moe_router_b=moe_router_b, moe_w1=moe_w1, moe_w3=moe_w3,
             moe_w2=moe_w2, final_g=final_g)
    y_prompt, new_state_gla = trunk(x_prompt, c_ctx[None, :], None, P)
    xs = x_sample + grid_pos_embed(x_sample.shape[1], x_sample.dtype)[None]
    y_sample, _ = trunk(xs, c, state_gla, P)
    return (y_prompt, y_sample, new_state_gla)
```

```python
import functools
import math

import numpy as np
import jax
import jax.numpy as jnp
from jax import lax
from jax.experimental import pallas as pl
from jax.experimental.pallas import tpu as pltpu

F32 = jnp.float32
BF16 = jnp.bfloat16
I32 = jnp.int32

D_MODEL = 2048
BATCH = 32
SEQ = 256
DEPTH = 4
DEC_BATCH = 8
DEC_SEQ = 2048
GRID_W = 64
CHUNK_A = 128
D_A = 1024
N_A_GROUPS = 8
A_GROUP = D_A // N_A_GROUPS
N_GLA_HEADS = 4
GLA_DK = D_MODEL // 4
GLA_DV = D_MODEL // 2
GLA_DK_HEAD = GLA_DK // N_GLA_HEADS
GLA_DV_HEAD = GLA_DV // N_GLA_HEADS
GLA_RANK = 16
GLA_TAU = 16.0
GLA_CHUNK = 64
D_FF = 5632
N_EXPERTS = 8
TOP_K = 2
EPS = 1e-6

TP = BATCH * SEQ
TS = DEC_BATCH * DEC_SEQ
T = TP + TS
N_COND = 1 + DEC_BATCH
COND_PAD = 16

C_U, C_V, C_Q, C_K, C_VG, C_R, C_GA, C_GB = 0, 1024, 2048, 2560, 3072, 4096, 5120, 7168
N_MAIN = 9216
LR_PAD = 128

VMEM_LIMIT = 56 * 1024 * 1024

TM_PROJ, TN_PROJ = 1024, 1024
TM_SGU = 512
TB_GLA = 256
TM_POST = 512
TM_MERGE, TN_MERGE = 1024, 1024
TM_OUT, TN_OUT = 1024, 1024
TM_FFN, TF_FFN = 512, 512
TM_ROUTE = 512
TD_DISP = 256
TR_EXP, TF_EXP = 512, 512
TC_COMB = 256
TM_NORM = 512
SLAB = 256

NT_EXP = (TOP_K * T) // TR_EXP + N_EXPERTS
ROWS_CAP = NT_EXP * TR_EXP


def _cparams(sem):
    return pltpu.CompilerParams(dimension_semantics=sem, vmem_limit_bytes=VMEM_LIMIT)


def _mod_row(i, tm):
    return jnp.maximum((i * tm) // DEC_SEQ - (TP // DEC_SEQ - 1), 0)


def _mod_spec(layer, j, tm, width, col_of=None):
    if col_of is None:
        return pl.BlockSpec((None, None, None, 1, width),
                            lambda m, n: (layer, _mod_row(m, tm), j, 0, 0))
    return pl.BlockSpec((None, None, None, 1, width),
                        lambda m, n: (layer, _mod_row(m, tm), j, 0, col_of(n)))


def _modnorm(x, g, sc, sh):
    y = x * lax.rsqrt(jnp.mean(x * x, axis=-1, keepdims=True) + EPS)
    return (y * g) * (1.0 + sc) + sh


def _fill_modnorm(x_ref, g, sc, sh, h_ref, rows):
    def body(i, carry):
        r = pl.multiple_of(i * SLAB, SLAB)
        h_ref[pl.ds(r, SLAB), :] = _modnorm(x_ref[pl.ds(r, SLAB), :], g, sc, sh).astype(h_ref.dtype)
        return carry
    lax.fori_loop(0, rows // SLAB, body, 0)


def _silu(x):
    return x * jax.nn.sigmoid(x)


def _log_sigmoid(x):
    return jnp.minimum(x, 0.0) - jnp.log1p(jnp.exp(-jnp.abs(x)))


def _embed_kernel(xp_ref, xs_ref, emb_ref, o_ref, *, n_ctx_tiles):
    i = pl.program_id(0)

    @pl.when(i < n_ctx_tiles)
    def _():
        o_ref[...] = xp_ref[...]

    @pl.when(i >= n_ctx_tiles)
    def _():
        o_ref[...] = xs_ref[...] + emb_ref[...]


def _embed(xp, xs, emb):
    tm = 512
    nct = TP // tm
    per_seq = DEC_SEQ // tm
    return pl.pallas_call(
        functools.partial(_embed_kernel, n_ctx_tiles=nct),
        out_shape=jax.ShapeDtypeStruct((T, D_MODEL), F32),
        grid=(T // tm,),
        in_specs=[
            pl.BlockSpec((tm, D_MODEL), lambda i: (jnp.minimum(i, nct - 1), 0)),
            pl.BlockSpec((tm, D_MODEL), lambda i: (jnp.maximum(i - nct, 0), 0)),
            pl.BlockSpec((tm, D_MODEL), lambda i: (jnp.maximum(i - nct, 0) % per_seq, 0)),
        ],
        out_specs=pl.BlockSpec((tm, D_MODEL), lambda i: (i, 0)),
        compiler_params=_cparams(("parallel",)),
        name="embed",
    )(xp, xs, emb)


def _mod_kernel(c_ref, w_ref, b_ref, o_ref):
    c = c_ref[...]
    s = _silu(c).astype(BF16)
    o_ref[...] = jnp.dot(s, w_ref[...].astype(BF16), preferred_element_type=F32) + b_ref[...]


def _mod_table(cond, w_mod, b_mod):
    tn = 1024
    n6 = 6 * D_MODEL
    return pl.pallas_call(
        _mod_kernel,
        out_shape=jax.ShapeDtypeStruct((DEPTH, COND_PAD, n6), F32),
        grid=(DEPTH, n6 // tn),
        in_specs=[
            pl.BlockSpec((COND_PAD, D_MODEL), lambda l, n: (0, 0)),
            pl.BlockSpec((None, D_MODEL, tn), lambda l, n: (l, 0, n)),
            pl.BlockSpec((None, 1, tn), lambda l, n: (l, 0, n)),
        ],
        out_specs=pl.BlockSpec((None, COND_PAD, tn), lambda l, n: (l, 0, n)),
        compiler_params=_cparams(("parallel", "parallel")),
        name="mod_table",
    )(cond, w_mod, b_mod.reshape(DEPTH, 1, n6))


def _proj_kernel(x_ref, g_ref, sh_ref, sc_ref, w_ref, wlr_ref, o_ref, lr_ref, h_ref):
    @pl.when(pl.program_id(1) == 0)
    def _():
        _fill_modnorm(x_ref, g_ref[...], sc_ref[...], sh_ref[...], h_ref, TM_PROJ)
        lr_ref[...] = jnp.dot(h_ref[...], wlr_ref[...], preferred_element_type=F32)

    o_ref[...] = jnp.dot(h_ref[...], w_ref[...], preferred_element_type=F32)


def _proj(x, norm_g, mod, layer, w_main, w_lr):
    tm, tn = TM_PROJ, TN_PROJ
    return pl.pallas_call(
        _proj_kernel,
        out_shape=(jax.ShapeDtypeStruct((T, N_MAIN), F32),
                   jax.ShapeDtypeStruct((T, LR_PAD), F32)),
        grid=(T // tm, N_MAIN // tn),
        in_specs=[
            pl.BlockSpec((tm, D_MODEL), lambda m, n: (m, 0)),
            pl.BlockSpec((None, 1, D_MODEL), lambda m, n: (layer, 0, 0)),
            _mod_spec(layer, 0, tm, D_MODEL),
            _mod_spec(layer, 1, tm, D_MODEL),
            pl.BlockSpec((D_MODEL, tn), lambda m, n: (0, n)),
            pl.BlockSpec((D_MODEL, LR_PAD), lambda m, n: (0, 0)),
        ],
        out_specs=(pl.BlockSpec((tm, tn), lambda m, n: (m, n)),
                   pl.BlockSpec((tm, LR_PAD), lambda m, n: (m, 0))),
        scratch_shapes=[pltpu.VMEM((tm, D_MODEL), BF16)],
        compiler_params=_cparams(("parallel", "arbitrary")),
        name="proj",
    )(x, norm_g, mod, mod, w_main, w_lr)


def _sgu_kernel(pu_ref, pv_ref, lng_ref, lnb_ref, ws_ref, bs_ref, a_ref, vn_ref):
    gv = jax.nn.gelu(pv_ref[...], approximate=True)
    mu = jnp.mean(gv, axis=-1, keepdims=True)
    d = gv - mu
    var = jnp.mean(d * d, axis=-1, keepdims=True)
    vn_ref[...] = (d * lax.rsqrt(var + EPS) * lng_ref[...] + lnb_ref[...]).astype(BF16)
    for c in range(TM_SGU // CHUNK_A):
        rows = slice(c * CHUNK_A, (c + 1) * CHUNK_A)
        for g in range(N_A_GROUPS):
            cols = slice(g * A_GROUP, (g + 1) * A_GROUP)
            f = jnp.dot(ws_ref[g], vn_ref[rows, cols], preferred_element_type=F32) + bs_ref[:, cols]
            gu = jax.nn.gelu(pu_ref[rows, cols], approximate=True)
            a_ref[rows, cols] = (gu * f).astype(BF16)


def _sgu(proj, ln_g, ln_b, w_s, bs_full):
    tm = TM_SGU
    return pl.pallas_call(
        _sgu_kernel,
        out_shape=jax.ShapeDtypeStruct((T, D_A), BF16),
        grid=(T // tm,),
        in_specs=[
            pl.BlockSpec((tm, D_A), lambda i: (i, C_U // D_A)),
            pl.BlockSpec((tm, D_A), lambda i: (i, C_V // D_A)),
            pl.BlockSpec((1, D_A), lambda i: (0, 0)),
            pl.BlockSpec((1, D_A), lambda i: (0, 0)),
            pl.BlockSpec((N_A_GROUPS, CHUNK_A, CHUNK_A), lambda i: (0, 0, 0)),
            pl.BlockSpec((CHUNK_A, D_A), lambda i: (0, 0)),
        ],
        out_specs=pl.BlockSpec((tm, D_A), lambda i: (i, 0)),
        scratch_shapes=[pltpu.VMEM((tm, D_A), BF16)],
        compiler_params=_cparams(("parallel",)),
        name="sgu",
    )(proj, proj, ln_g, ln_b, w_s, bs_full)


def _split3(x):
    hi = x.astype(BF16)
    r = x - hi.astype(F32)
    mid = r.astype(BF16)
    lo = (r - mid.astype(F32)).astype(BF16)
    return hi, mid, lo


def _tri_cumsum(tri, x):
    hi, mid, lo = _split3(x)
    return (jnp.dot(tri, hi, preferred_element_type=F32)
            + jnp.dot(tri, mid, preferred_element_type=F32)
            + jnp.dot(tri, lo, preferred_element_type=F32))


_NT = (((1,), (1,)), ((), ()))
_TN = (((0,), (0,)), ((), ()))


def _gla_direction(lr_ref, a2_ref, ab_ref, q_ref, k_ref, v_ref, o_ref, st_ref, *, tb, backward):
    nchunk = tb // GLA_CHUNK
    z = jnp.dot(lr_ref[...].astype(BF16), a2_ref[...], preferred_element_type=F32) + ab_ref[...]
    la = _log_sigmoid(z) / GLA_TAU
    r = lax.broadcasted_iota(I32, (tb, tb), 0)
    c = lax.broadcasted_iota(I32, (tb, tb), 1)
    same = (r // GLA_CHUNK) == (c // GLA_CHUNK)
    tri = (same & ((c >= r) if backward else (c <= r))).astype(BF16)
    b = _tri_cumsum(tri, la)
    eb = jnp.exp(b)
    enb = jnp.exp(-b)
    rr = lax.broadcasted_iota(I32, (GLA_CHUNK, GLA_CHUNK), 0)
    cc = lax.broadcasted_iota(I32, (GLA_CHUNK, GLA_CHUNK), 1)
    mask = (cc >= rr) if backward else (cc <= rr)
    scale = GLA_DK_HEAD ** -0.5
    order = range(nchunk - 1, -1, -1) if backward else range(nchunk)
    for ci in order:
        rows = slice(ci * GLA_CHUNK, (ci + 1) * GLA_CHUNK)
        last = ci * GLA_CHUNK if backward else (ci + 1) * GLA_CHUNK - 1
        btot = b[last:last + 1, :]
        eke = jnp.exp(btot - b[rows, :])
        gdec = jnp.exp(btot)
        for h in range(N_GLA_HEADS):
            kc = slice(h * GLA_DK_HEAD, (h + 1) * GLA_DK_HEAD)
            vc = slice(h * GLA_DV_HEAD, (h + 1) * GLA_DV_HEAD)
            q = q_ref[rows, kc] * scale
            k = k_ref[rows, kc]
            qd = (q * eb[rows, kc]).astype(BF16)
            kd = (k * enb[rows, kc]).astype(BF16)
            ke = (k * eke[:, kc]).astype(BF16)
            vv = v_ref[rows, vc].astype(BF16)
            att = lax.dot_general(qd, kd, _NT, preferred_element_type=F32)
            att = jnp.where(mask, att, 0.0).astype(BF16)
            st = st_ref[h]
            o = (jnp.dot(att, vv, preferred_element_type=F32)
                 + lax.dot_general(qd, st.astype(BF16), _NT, preferred_element_type=F32))
            o_ref[rows, vc] = o
            dst = lax.dot_general(vv, ke, _TN, preferred_element_type=F32)
            st_ref[h] = st * gdec[:, kc] + dst


def _gla_kernel(fwd_ref, bwd_ref, flag_ref, seq_ref,
                lrf_ref, lrb_ref, a2f_ref, a2b_ref, abf_ref, abb_ref,
                qf_ref, kf_ref, vf_ref, qb_ref, kb_ref, vb_ref, s0_ref,
                of_ref, ob_ref, sfin_ref, stf_ref, stb_ref):
    i = pl.program_id(0)
    first = (flag_ref[i] & 1) == 1
    last = (flag_ref[i] & 2) == 2
    is_ctx = seq_ref[i] < BATCH

    @pl.when(first & is_ctx)
    def _():
        stf_ref[...] = jnp.zeros_like(stf_ref)
        stb_ref[...] = jnp.zeros_like(stb_ref)

    @pl.when(first & jnp.logical_not(is_ctx))
    def _():
        for h in range(N_GLA_HEADS):
            stf_ref[h] = s0_ref[0, h].T
            stb_ref[h] = s0_ref[1, h].T

    _gla_direction(lrf_ref, a2f_ref, abf_ref, qf_ref, kf_ref, vf_ref, of_ref, stf_ref,
                   tb=TB_GLA, backward=False)
    _gla_direction(lrb_ref, a2b_ref, abb_ref, qb_ref, kb_ref, vb_ref, ob_ref, stb_ref,
                   tb=TB_GLA, backward=True)

    @pl.when(last & is_ctx)
    def _():
        for h in range(N_GLA_HEADS):
            sfin_ref[0, h] = stf_ref[h].T
            sfin_ref[1, h] = stb_ref[h].T


def _gla_tables():
    tb = TB_GLA
    fwd, bwd, flag, seq = [], [], [], []
    for s, (row0, seqlen) in enumerate([(b * SEQ, SEQ) for b in range(BATCH)]
                                       + [(TP + b * DEC_SEQ, DEC_SEQ) for b in range(DEC_BATCH)]):
        nb = seqlen // tb
        for j in range(nb):
            fwd.append(row0 // tb + j)
            bwd.append(row0 // tb + nb - 1 - j)
            flag.append((1 if j == 0 else 0) | (2 if j == nb - 1 else 0))
            seq.append(s)
    return tuple(np.asarray(t, np.int32) for t in (fwd, bwd, flag, seq))


def _gla(proj, lr, a2f, a2b, abf, abb, s0, layer):
    tb = TB_GLA
    fwd_t, bwd_t, flag_t, seq_t = _gla_tables()
    nsteps = fwd_t.shape[0]

    def fwd(i, fw, bw, fl, sq):
        return fw[i]

    def bwd(i, fw, bw, fl, sq):
        return bw[i]

    def col_spec(rowfn, width, col):
        return pl.BlockSpec((tb, width), lambda i, *t: (rowfn(i, *t), col // width))

    const2 = lambda i, *t: (0, 0)
    state_block = (2, N_GLA_HEADS, GLA_DK_HEAD, GLA_DV_HEAD)
    in_specs = [
        pl.BlockSpec((tb, LR_PAD), lambda i, *t: (fwd(i, *t), 0)),
        pl.BlockSpec((tb, LR_PAD), lambda i, *t: (bwd(i, *t), 0)),
        pl.BlockSpec((LR_PAD, GLA_DK), const2),
        pl.BlockSpec((LR_PAD, GLA_DK), const2),
        pl.BlockSpec((1, GLA_DK), const2),
        pl.BlockSpec((1, GLA_DK), const2),
        col_spec(fwd, GLA_DK, C_Q), col_spec(fwd, GLA_DK, C_K), col_spec(fwd, GLA_DV, C_VG),
        col_spec(bwd, GLA_DK, C_Q), col_spec(bwd, GLA_DK, C_K), col_spec(bwd, GLA_DV, C_VG),
        pl.BlockSpec((None, None) + state_block,
                     lambda i, fw, bw, fl, sq: (jnp.maximum(sq[i] - BATCH, 0), layer, 0, 0, 0, 0)),
    ]
    out_shape = (jax.ShapeDtypeStruct((T, GLA_DV), F32),
                 jax.ShapeDtypeStruct((T, GLA_DV), F32),
                 jax.ShapeDtypeStruct((BATCH,) + state_block, F32))
    out_specs = (pl.BlockSpec((tb, GLA_DV), lambda i, *t: (fwd(i, *t), 0)),
                 pl.BlockSpec((tb, GLA_DV), lambda i, *t: (bwd(i, *t), 0)),
                 pl.BlockSpec((None,) + state_block,
                              lambda i, fw, bw, fl, sq: (jnp.minimum(sq[i], BATCH - 1), 0, 0, 0, 0)))
    return pl.pallas_call(
        _gla_kernel,
        out_shape=out_shape,
        grid_spec=pltpu.PrefetchScalarGridSpec(
            num_scalar_prefetch=4,
            grid=(nsteps,),
            in_specs=in_specs,
            out_specs=out_specs,
            scratch_shapes=[pltpu.VMEM((N_GLA_HEADS, GLA_DV_HEAD, GLA_DK_HEAD), F32),
                            pltpu.VMEM((N_GLA_HEADS, GLA_DV_HEAD, GLA_DK_HEAD), F32)],
        ),
        compiler_params=_cparams(("arbitrary",)),
        name="gla",
    )(fwd_t, bwd_t, flag_t, seq_t, lr, lr, a2f, a2b, abf, abb,
      proj, proj, proj, proj, proj, proj, s0)


def _gla_post_kernel(of_ref, ob_ref, pr_ref, g_ref, o_ref):
    for h in range(N_GLA_HEADS):
        vc = slice(h * GLA_DV_HEAD, (h + 1) * GLA_DV_HEAD)
        o = of_ref[:, vc] + ob_ref[:, vc]
        o = o * lax.rsqrt(jnp.mean(o * o, axis=-1, keepdims=True) + EPS)
        o_ref[:, vc] = (o * g_ref[:, vc] * _silu(pr_ref[:, vc])).astype(BF16)


def _gla_post(o_f, o_b, proj, norm_g):
    tm = TM_POST
    return pl.pallas_call(
        _gla_post_kernel,
        out_shape=jax.ShapeDtypeStruct((T, GLA_DV), BF16),
        grid=(T // tm,),
        in_specs=[
            pl.BlockSpec((tm, GLA_DV), lambda i: (i, 0)),
            pl.BlockSpec((tm, GLA_DV), lambda i: (i, 0)),
            pl.BlockSpec((tm, GLA_DV), lambda i: (i, C_R // GLA_DV)),
            pl.BlockSpec((1, GLA_DV), lambda i: (0, 0)),
        ],
        out_specs=pl.BlockSpec((tm, GLA_DV), lambda i: (i, 0)),
        compiler_params=_cparams(("parallel",)),
        name="gla_post",
    )(o_f, o_b, proj, norm_g)


def _merge_kernel(a_ref, o_ref, wa_ref, wb_ref, ga_ref, gb_ref, m_ref):
    ya = jnp.dot(a_ref[...], wa_ref[...], preferred_element_type=F32)
    yb = jnp.dot(o_ref[...], wb_ref[...], preferred_element_type=F32)
    m_ref[...] = (jax.nn.sigmoid(ga_ref[...]) * ya + jax.nn.sigmoid(gb_ref[...]) * yb).astype(BF16)


def _merge(a, o, w_a, w_b, proj):
    tm, tn = TM_MERGE, TN_MERGE
    return pl.pallas_call(
        _merge_kernel,
        out_shape=jax.ShapeDtypeStruct((T, D_MODEL), BF16),
        grid=(T // tm, D_MODEL // tn),
        in_specs=[
            pl.BlockSpec((tm, D_A), lambda m, n: (m, 0)),
            pl.BlockSpec((tm, GLA_DV), lambda m, n: (m, 0)),
            pl.BlockSpec((D_A, tn), lambda m, n: (0, n)),
            pl.BlockSpec((GLA_DV, tn), lambda m, n: (0, n)),
            pl.BlockSpec((tm, tn), lambda m, n: (m, C_GA // tn + n)),
            pl.BlockSpec((tm, tn), lambda m, n: (m, C_GB // tn + n)),
        ],
        out_specs=pl.BlockSpec((tm, tn), lambda m, n: (m, n)),
        compiler_params=_cparams(("parallel", "arbitrary")),
        name="merge",
    )(a, o, w_a, w_b, proj, proj)


def _outproj_kernel(m_ref, w_ref, x_ref, g_ref, o_ref):
    y = jnp.dot(m_ref[...], w_ref[...], preferred_element_type=F32)
    o_ref[...] = x_ref[...] + g_ref[...] * y


def _outproj(merged, w_out, x, mod, layer):
    tm, tn = TM_OUT, TN_OUT
    return pl.pallas_call(
        _outproj_kernel,
        out_shape=jax.ShapeDtypeStruct((T, D_MODEL), F32),
        grid=(T // tm, D_MODEL // tn),
        in_specs=[
            pl.BlockSpec((tm, D_MODEL), lambda m, n: (m, 0)),
            pl.BlockSpec((D_MODEL, tn), lambda m, n: (0, n)),
            pl.BlockSpec((tm, tn), lambda m, n: (m, n)),
            _mod_spec(layer, 2, tm, tn, col_of=lambda n: n),
        ],
        out_specs=pl.BlockSpec((tm, tn), lambda m, n: (m, n)),
        compiler_params=_cparams(("parallel", "arbitrary")),
        name="outproj",
    )(merged, w_out, x, mod)


def _ffn_kernel(x_ref, g_ref, sh_ref, sc_ref, g2_ref, w1_ref, w3_ref, w2_ref, o_ref, h_ref, *, nf):
    f = pl.program_id(1)

    @pl.when(f == 0)
    def _():
        _fill_modnorm(x_ref, g_ref[...], sc_ref[...], sh_ref[...], h_ref, TM_FFN)

    h = h_ref[...]
    u = _silu(jnp.dot(h, w1_ref[...], preferred_element_type=F32)) * jnp.dot(
        h, w3_ref[...], preferred_element_type=F32)
    y = jnp.dot(u.astype(BF16), w2_ref[...], preferred_element_type=F32)

    @pl.when(f == 0)
    def _():
        o_ref[...] = y

    @pl.when((f > 0) & (f < nf - 1))
    def _():
        o_ref[...] += y

    @pl.when(f == nf - 1)
    def _():
        o_ref[...] = x_ref[...] + g2_ref[...] * (o_ref[...] + y)


def _ffn(x, norm_g, mod, layer, w1, w3, w2):
    tm, tf = TM_FFN, TF_FFN
    nf = D_FF // tf
    return pl.pallas_call(
        functools.partial(_ffn_kernel, nf=nf),
        out_shape=jax.ShapeDtypeStruct((T, D_MODEL), F32),
        grid=(T // tm, nf),
        in_specs=[
            pl.BlockSpec((tm, D_MODEL), lambda m, f: (m, 0)),
            pl.BlockSpec((None, 1, D_MODEL), lambda m, f: (layer, 0, 0)),
            _mod_spec(layer, 3, tm, D_MODEL),
            _mod_spec(layer, 4, tm, D_MODEL),
            _mod_spec(layer, 5, tm, D_MODEL),
            pl.BlockSpec((D_MODEL, tf), lambda m, f: (0, f)),
            pl.BlockSpec((D_MODEL, tf), lambda m, f: (0, f)),
            pl.BlockSpec((tf, D_MODEL), lambda m, f: (f, 0)),
        ],
        out_specs=pl.BlockSpec((tm, D_MODEL), lambda m, f: (m, 0)),
        scratch_shapes=[pltpu.VMEM((tm, D_MODEL), BF16)],
        compiler_params=_cparams(("parallel", "arbitrary")),
        name="ffn",
    )(x, norm_g, mod, mod, mod, w1, w3, w2)


def _router_kernel(x_ref, g_ref, sh_ref, sc_ref, rwt_ref, rb_ref, h_ref, info_ref, p_ref, cnt_ref,
                   carry_ref):
    i = pl.program_id(0)
    tm = TM_ROUTE

    @pl.when(i == 0)
    def _():
        carry_ref[...] = jnp.zeros_like(carry_ref)

    _fill_modnorm(x_ref, g_ref[...], sc_ref[...], sh_ref[...], h_ref, tm)
    logits = lax.dot_general(rwt_ref[...], h_ref[...], _NT, preferred_element_type=F32,
                             precision=lax.Precision.HIGHEST) + rb_ref[:, 0:1]
    eidx = lax.broadcasted_iota(I32, (N_EXPERTS, tm), 0).astype(F32)
    none = float(N_EXPERTS)
    m1 = jnp.max(logits, axis=0, keepdims=True)
    i1 = jnp.min(jnp.where(logits == m1, eidx, none), axis=0, keepdims=True)
    rest = jnp.where(eidx == i1, -jnp.inf, logits)
    m2 = jnp.max(rest, axis=0, keepdims=True)
    i2 = jnp.min(jnp.where(rest == m2, eidx, none), axis=0, keepdims=True)
    e = jnp.exp(m2 - m1)
    den = 1.0 + e
    sel1 = eidx == i1
    sel2 = eidx == i2
    onehot = (sel1 | sel2).astype(BF16)
    r = lax.broadcasted_iota(I32, (tm, tm), 0)
    c = lax.broadcasted_iota(I32, (tm, tm), 1)
    upper = (r <= c).astype(BF16)
    cnt = jnp.dot(onehot, upper, preferred_element_type=F32) + carry_ref[:, 0:1]
    r1 = jnp.sum(jnp.where(sel1, cnt, 0.0), axis=0, keepdims=True) - 1.0
    r2 = jnp.sum(jnp.where(sel2, cnt, 0.0), axis=0, keepdims=True) - 1.0
    info_ref[...] = jnp.zeros_like(info_ref)
    info_ref[0:1, :] = i1.astype(I32)
    info_ref[1:2, :] = i2.astype(I32)
    info_ref[2:3, :] = r1.astype(I32)
    info_ref[3:4, :] = r2.astype(I32)
    p_ref[...] = jnp.zeros_like(p_ref)
    p_ref[0:1, :] = 1.0 / den
    p_ref[1:2, :] = e / den
    total = cnt[:, tm - 1:tm]
    carry_ref[...] = jnp.broadcast_to(total, carry_ref.shape)
    cnt_ref[...] = jnp.broadcast_to(total, cnt_ref.shape).astype(I32)


def _router(x, norm_g, mod, layer, rw_t, rb):
    tm = TM_ROUTE
    nt = T // tm
    return pl.pallas_call(
        _router_kernel,
        out_shape=(jax.ShapeDtypeStruct((T, D_MODEL), F32),
                   jax.ShapeDtypeStruct((nt, 8, tm), I32),
                   jax.ShapeDtypeStruct((nt, 8, tm), F32),
                   jax.ShapeDtypeStruct((N_EXPERTS, 128), I32)),
        grid=(nt,),
        in_specs=[
            pl.BlockSpec((tm, D_MODEL), lambda i: (i, 0)),
            pl.BlockSpec((None, 1, D_MODEL), lambda i: (layer, 0, 0)),
            pl.BlockSpec((None, None, None, 1, D_MODEL), lambda i: (layer, _mod_row(i, tm), 3, 0, 0)),
            pl.BlockSpec((None, None, None, 1, D_MODEL), lambda i: (layer, _mod_row(i, tm), 4, 0, 0)),
            pl.BlockSpec((N_EXPERTS, D_MODEL), lambda i: (0, 0)),
            pl.BlockSpec((N_EXPERTS, 128), lambda i: (0, 0)),
        ],
        out_specs=(pl.BlockSpec((tm, D_MODEL), lambda i: (i, 0)),
                   pl.BlockSpec((None, 8, tm), lambda i: (i, 0, 0)),
                   pl.BlockSpec((None, 8, tm), lambda i: (i, 0, 0)),
                   pl.BlockSpec((N_EXPERTS, 128), lambda i: (0, 0))),
        scratch_shapes=[pltpu.VMEM((N_EXPERTS, 128), F32)],
        compiler_params=_cparams(("arbitrary",)),
        name="router",
    )(x, norm_g, mod, mod, rw_t, rb)


def _dispatch_kernel(pos1_ref, pos2_ref, zstart_ref, zcount_ref, h_ref, o_hbm, zrow_ref, sem):
    i = pl.program_id(0)
    td = TD_DISP
    base = i * td

    def row_copy(src_row, dst_row):
        return pltpu.make_async_copy(h_ref.at[pl.ds(src_row, 1)], o_hbm.at[pl.ds(dst_row, 1)], sem)

    def zero_copy(dst_row):
        return pltpu.make_async_copy(zrow_ref, o_hbm.at[pl.ds(dst_row, 1)], sem)

    @pl.when(i == 0)
    def _():
        zrow_ref[...] = jnp.zeros_like(zrow_ref)
        for e in range(N_EXPERTS + 1):
            def zbody(r, carry):
                zero_copy(zstart_ref[e] + r).start()
                return carry
            lax.fori_loop(0, zcount_ref[e], zbody, 0)

            def zwait(r, carry):
                zero_copy(0).wait()
                return carry
            lax.fori_loop(0, zcount_ref[e], zwait, 0)

    def body(r, carry):
        row_copy(r, pos1_ref[base + r]).start()
        row_copy(r, pos2_ref[base + r]).start()
        return carry
    lax.fori_loop(0, td, body, 0)

    def wbody(r, carry):
        row_copy(0, 0).wait()
        row_copy(0, 0).wait()
        return carry
    lax.fori_loop(0, td, wbody, 0)


def _dispatch(pos1, pos2, zstart, zcount, h):
    td = TD_DISP
    return pl.pallas_call(
        _dispatch_kernel,
        out_shape=jax.ShapeDtypeStruct((ROWS_CAP, D_MODEL), F32),
        grid_spec=pltpu.PrefetchScalarGridSpec(
            num_scalar_prefetch=4,
            grid=(T // td,),
            in_specs=[pl.BlockSpec((td, D_MODEL), lambda i, *_: (i, 0))],
            out_specs=pl.BlockSpec(memory_space=pl.ANY),
            scratch_shapes=[pltpu.VMEM((1, D_MODEL), F32), pltpu.SemaphoreType.DMA],
        ),
        compiler_params=_cparams(("arbitrary",)),
        name="dispatch",
    )(pos1, pos2, zstart, zcount, h)


def _expert_kernel(te_ref, nact_ref, hs_ref, w1_ref, w3_ref, w2_ref, y_ref, hb_ref):
    i = pl.program_id(0)
    f = pl.program_id(1)

    @pl.when(i < nact_ref[0])
    def _():
        @pl.when(f == 0)
        def _():
            def body(s, carry):
                r = pl.multiple_of(s * SLAB, SLAB)
                hb_ref[pl.ds(r, SLAB), :] = hs_ref[pl.ds(r, SLAB), :].astype(BF16)
                return carry
            lax.fori_loop(0, TR_EXP // SLAB, body, 0)

        h = hb_ref[...]
        u = _silu(jnp.dot(h, w1_ref[...], preferred_element_type=F32)) * jnp.dot(
            h, w3_ref[...], preferred_element_type=F32)
        y = jnp.dot(u.astype(BF16), w2_ref[...], preferred_element_type=F32)

        @pl.when(f == 0)
        def _():
            y_ref[...] = y

        @pl.when(f > 0)
        def _():
            y_ref[...] += y

    @pl.when((i >= nact_ref[0]) & (f == 0))
    def _():
        y_ref[...] = jnp.zeros_like(y_ref)


def _experts(tile_expert, n_active, hs, w1, w3, w2):
    tr, tf = TR_EXP, TF_EXP
    nf = D_FF // tf

    def row_map(i, f, te, na):
        return (jnp.minimum(i, na[0] - 1), 0)

    def out_map(i, f, te, na):
        return (i, 0)

    def w13_map(i, f, te, na):
        return (te[i], 0, jnp.where(i < na[0], f, nf - 1))

    def w2_map(i, f, te, na):
        return (te[i], jnp.where(i < na[0], f, nf - 1), 0)

    return pl.pallas_call(
        _expert_kernel,
        out_shape=jax.ShapeDtypeStruct((ROWS_CAP, D_MODEL), F32),
        grid_spec=pltpu.PrefetchScalarGridSpec(
            num_scalar_prefetch=2,
            grid=(NT_EXP, nf),
            in_specs=[
                pl.BlockSpec((tr, D_MODEL), row_map),
                pl.BlockSpec((None, D_MODEL, tf), w13_map),
                pl.BlockSpec((None, D_MODEL, tf), w13_map),
                pl.BlockSpec((None, tf, D_MODEL), w2_map),
            ],
            out_specs=pl.BlockSpec((tr, D_MODEL), out_map),
            scratch_shapes=[pltpu.VMEM((tr, D_MODEL), BF16)],
        ),
        compiler_params=_cparams(("arbitrary", "arbitrary")),
        name="experts",
    )(tile_expert, n_active, hs, w1, w3, w2)


def _combine_kernel(pos1_ref, pos2_ref, x_ref, p1_ref, p2_ref, g2_ref, y_hbm, o_ref, b1_ref, b2_ref, sem):
    i = pl.program_id(0)
    tc = TC_COMB
    base = i * tc

    def row_copy(buf, src_row, dst_row):
        return pltpu.make_async_copy(y_hbm.at[pl.ds(src_row, 1)], buf.at[pl.ds(dst_row, 1)], sem)

    def body(r, carry):
        row_copy(b1_ref, pos1_ref[base + r], r).start()
        row_copy(b2_ref, pos2_ref[base + r], r).start()
        return carry
    lax.fori_loop(0, tc, body, 0)

    def wbody(r, carry):
        row_copy(b1_ref, 0, 0).wait()
        row_copy(b2_ref, 0, 0).wait()
        return carry
    lax.fori_loop(0, tc, wbody, 0)

    mix = p1_ref[...] * b1_ref[...] + p2_ref[...] * b2_ref[...]
    o_ref[...] = x_ref[...] + g2_ref[...] * mix


def _combine(pos1, pos2, x, p1, p2, mod, layer, y):
    tc = TC_COMB
    return pl.pallas_call(
        _combine_kernel,
        out_shape=jax.ShapeDtypeStruct((T, D_MODEL), F32),
        grid_spec=pltpu.PrefetchScalarGridSpec(
            num_scalar_prefetch=2,
            grid=(T // tc,),
            in_specs=[
                pl.BlockSpec((tc, D_MODEL), lambda i, *_: (i, 0)),
                pl.BlockSpec((tc, 1), lambda i, *_: (i, 0)),
                pl.BlockSpec((tc, 1), lambda i, *_: (i, 0)),
                pl.BlockSpec((None, None, None, 1, D_MODEL),
                             lambda i, *_: (layer, _mod_row(i, tc), 5, 0, 0)),
                pl.BlockSpec(memory_space=pl.ANY),
            ],
            out_specs=pl.BlockSpec((tc, D_MODEL), lambda i, *_: (i, 0)),
            scratch_shapes=[pltpu.VMEM((tc, D_MODEL), F32), pltpu.VMEM((tc, D_MODEL), F32),
                            pltpu.SemaphoreType.DMA],
        ),
        compiler_params=_cparams(("arbitrary",)),
        name="combine",
    )(pos1, pos2, x, p1, p2, mod, y)


def _moe(x, norm_g, mod, layer, rw_t, rb, w1, w3, w2):
    h, info, probs, counts = _router(x, norm_g, mod, layer, rw_t, rb)
    counts = counts[:, 0]
    cap = ((counts + TR_EXP - 1) // TR_EXP) * TR_EXP
    ends = jnp.cumsum(cap)
    starts = ends - cap
    e1 = info[:, 0, :].reshape(T)
    e2 = info[:, 1, :].reshape(T)
    pos1 = starts[e1] + info[:, 2, :].reshape(T)
    pos2 = starts[e2] + info[:, 3, :].reshape(T)
    tile_end = ends // TR_EXP
    n_active = tile_end[-1:]
    tiles = jnp.minimum(jnp.arange(NT_EXP, dtype=I32), n_active[0] - 1)
    tile_expert = jnp.sum((tiles[:, None] >= tile_end[None, :]).astype(I32), axis=1)
    zstart = jnp.concatenate([starts + counts, ends[-1:]])
    zcount = jnp.concatenate([cap - counts, ROWS_CAP - ends[-1:]])
    hs = _dispatch(pos1, pos2, zstart, zcount, h)
    y = _experts(tile_expert, n_active, hs, w1, w3, w2)
    p1 = probs[:, 0, :].reshape(T, 1)
    p2 = probs[:, 1, :].reshape(T, 1)
    return _combine(pos1, pos2, x, p1, p2, mod, layer, y)


def _final_kernel(x_ref, g_ref, o_ref):
    x = x_ref[...]
    o_ref[...] = x * lax.rsqrt(jnp.mean(x * x, axis=-1, keepdims=True) + EPS) * g_ref[...]


def _final_norm(x, g):
    tm = TM_NORM
    return pl.pallas_call(
        _final_kernel,
        out_shape=jax.ShapeDtypeStruct((T, D_MODEL), F32),
        grid=(T // tm,),
        in_specs=[pl.BlockSpec((tm, D_MODEL), lambda i: (i, 0)),
                  pl.BlockSpec((1, D_MODEL), lambda i: (0, 0))],
        out_specs=pl.BlockSpec((tm, D_MODEL), lambda i: (i, 0)),
        compiler_params=_cparams(("parallel",)),
        name="final_norm",
    )(x, g)


def _grid_pos_embed():
    rows = DEC_SEQ // GRID_W
    r, col = jnp.meshgrid(jnp.arange(rows, dtype=F32), jnp.arange(GRID_W, dtype=F32), indexing='ij')
    r = r.reshape(-1)
    col = col.reshape(-1)
    quarter = D_MODEL // 4
    freq = jnp.exp(-math.log(10000.0) * jnp.arange(quarter, dtype=F32) / quarter)
    ar = r[:, None] * freq
    ac = col[:, None] * freq
    return jnp.concatenate([jnp.sin(ar), jnp.cos(ar), jnp.sin(ac), jnp.cos(ac)], axis=-1)


def kernel(x_prompt, x_sample, state_gla, c, c_ctx, norm1_g, norm2_g, w_mod, b_mod, w_in,
           sgu_ln_g, sgu_ln_b, w_spatial, b_spatial, gla_a2, gla_ab, gla_norm_g,
           w_branch_a, w_branch_b, w_out, ffn_w1, ffn_w3, ffn_w2,
           moe_router, moe_router_b, moe_w1, moe_w3, moe_w2, final_g):
    x = _embed(x_prompt.reshape(TP, D_MODEL), x_sample.reshape(TS, D_MODEL), _grid_pos_embed())

    cond = jnp.zeros((COND_PAD, D_MODEL), F32).at[0].set(c_ctx).at[1:N_COND].set(c)
    mod = _mod_table(cond, w_mod, b_mod).reshape(DEPTH, COND_PAD, 6, 1, D_MODEL)

    norm1 = norm1_g.reshape(DEPTH, 1, D_MODEL)
    norm2 = norm2_g.reshape(DEPTH, 1, D_MODEL)
    c_lr = C_GA
    finals = []
    for l in range(DEPTH):
        w_l = w_in[l]
        w_main = jnp.concatenate([w_l[:, :c_lr], w_l[:, c_lr + 2 * GLA_RANK:]], axis=1).astype(BF16)
        w_lr = jnp.pad(w_l[:, c_lr:c_lr + 2 * GLA_RANK], ((0, 0), (0, LR_PAD - 2 * GLA_RANK))).astype(BF16)
        proj, lr = _proj(x, norm1, mod, l, w_main, w_lr)

        bs_full = jnp.repeat(b_spatial[l].T, A_GROUP, axis=1)
        a = _sgu(proj, sgu_ln_g[l][None], sgu_ln_b[l][None], w_spatial[l].astype(BF16), bs_full)

        a2f = jnp.zeros((LR_PAD, GLA_DK), F32).at[:GLA_RANK].set(gla_a2[l, 0]).astype(BF16)
        a2b = jnp.zeros((LR_PAD, GLA_DK), F32).at[GLA_RANK:2 * GLA_RANK].set(gla_a2[l, 1]).astype(BF16)
        abf = gla_ab[l, 0][None]
        abb = gla_ab[l, 1][None]
        o_f, o_b, s_fin = _gla(proj, lr, a2f, a2b, abf, abb, state_gla, l)
        finals.append(s_fin)
        o = _gla_post(o_f, o_b, proj, gla_norm_g[l][None])

        merged = _merge(a, o, w_branch_a[l].astype(BF16), w_branch_b[l].astype(BF16), proj)
        x = _outproj(merged, w_out[l].astype(BF16), x, mod, l)

        j = l // 2
        if l % 2 == 0:
            x = _ffn(x, norm2, mod, l, ffn_w1[j].astype(BF16), ffn_w3[j].astype(BF16),
                     ffn_w2[j].astype(BF16))
        else:
            rb = jnp.broadcast_to(moe_router_b[j][:, None], (N_EXPERTS, 128))
            x = _moe(x, norm2, mod, l, moe_router[j].T, rb, moe_w1[j].astype(BF16),
                     moe_w3[j].astype(BF16), moe_w2[j].astype(BF16))

    y = _final_norm(x, final_g[None])
    y_prompt = y[:TP].reshape(BATCH, SEQ, D_MODEL)
    y_sample = y[TP:].reshape(DEC_BATCH, DEC_SEQ, D_MODEL)
    new_state = jnp.stack(finals, axis=1)
    return (y_prompt, y_sample, new_state)
```

```python
import functools
import math

import numpy as np
import jax
import jax.numpy as jnp
from jax import lax
from jax.experimental import pallas as pl
from jax.experimental.pallas import tpu as pltpu

F32 = jnp.float32
BF16 = jnp.bfloat16
I32 = jnp.int32

D_MODEL = 2048
BATCH = 32
SEQ = 256
DEPTH = 4
DEC_BATCH = 8
DEC_SEQ = 2048
GRID_W = 64
CHUNK_A = 128
D_A = 1024
N_A_GROUPS = 8
A_GROUP = D_A // N_A_GROUPS
N_GLA_HEADS = 4
GLA_DK = D_MODEL // 4
GLA_DV = D_MODEL // 2
GLA_DK_HEAD = GLA_DK // N_GLA_HEADS
GLA_DV_HEAD = GLA_DV // N_GLA_HEADS
GLA_RANK = 16
GLA_TAU = 16.0
GLA_CHUNK = 64
D_FF = 5632
N_EXPERTS = 8
TOP_K = 2
EPS = 1e-6

TP = BATCH * SEQ
TS = DEC_BATCH * DEC_SEQ
T = TP + TS
N_COND = 1 + DEC_BATCH
COND_PAD = 16

W_IN_LR = 2 * D_A + 2 * GLA_DK + 2 * GLA_DV
C_GA, C_GB, C_U, C_V, C_Q, C_K, C_VG, C_R = 0, 2048, 4096, 5120, 6144, 6656, 7168, 8192
N_MAIN = 9216
LR_PAD = 128

VMEM_LIMIT = 56 * 1024 * 1024

TM_PROJ, TN_PROJ = 1024, 1024
TM_SGU = 512
TB_GLA = 256
TM_POST = 512
TM_MIX = 256
TM_FFN, TF_FFN = 512, 512
TM_ROUTE = 512
TD_DISP = 256
TR_EXP, TF_EXP = 512, 512
TC_COMB = 256
TM_NORM = 512
SLAB = 256
RS_SWIGLU = 512
DMA_UNROLL = 8

NT_EXP = (TOP_K * T) // TR_EXP + N_EXPERTS
ROWS_CAP = NT_EXP * TR_EXP


def _cparams(sem):
    return pltpu.CompilerParams(dimension_semantics=sem, vmem_limit_bytes=VMEM_LIMIT)


def _mod_row(i, tm):
    return jnp.maximum((i * tm) // DEC_SEQ - (TP // DEC_SEQ - 1), 0)


def _mod_spec(layer, j, tm, width=D_MODEL):
    return pl.BlockSpec((None, None, None, 1, width),
                        lambda m, *_: (layer, _mod_row(m, tm), j, 0, 0))


def _layer_spec(shape, layer):
    zeros = (0,) * len(shape)
    return pl.BlockSpec((None,) + tuple(shape), lambda *_: (layer,) + zeros)


def _modnorm(x, g, sc, sh):
    y = x * lax.rsqrt(jnp.mean(x * x, axis=-1, keepdims=True) + EPS)
    return (y * g) * (1.0 + sc) + sh


def _fill_modnorm(x_ref, g, sc, sh, h_ref, rows):
    def body(i, carry):
        r = pl.multiple_of(i * SLAB, SLAB)
        h_ref[pl.ds(r, SLAB), :] = _modnorm(x_ref[pl.ds(r, SLAB), :], g, sc, sh).astype(h_ref.dtype)
        return carry
    lax.fori_loop(0, rows // SLAB, body, 0)


def _silu(x):
    return x * jax.nn.sigmoid(x)


def _log_sigmoid(x):
    return jnp.minimum(x, 0.0) - jnp.log(1.0 + jnp.exp(-jnp.abs(x)))


def _swiglu_accumulate(h_ref, w1_ref, w3_ref, w2_ref, acc_ref, rows):
    for r in range(rows // RS_SWIGLU):
        sl = slice(r * RS_SWIGLU, (r + 1) * RS_SWIGLU)
        h = h_ref[sl, :]
        u = _silu(jnp.dot(h, w1_ref[...], preferred_element_type=F32)) * jnp.dot(
            h, w3_ref[...], preferred_element_type=F32)
        acc_ref[sl, :] += jnp.dot(u.astype(BF16), w2_ref[...], preferred_element_type=F32)


def _embed_kernel(xp_ref, xs_ref, emb_ref, o_ref, *, n_ctx_tiles):
    i = pl.program_id(0)

    @pl.when(i < n_ctx_tiles)
    def _():
        o_ref[...] = xp_ref[...]

    @pl.when(i >= n_ctx_tiles)
    def _():
        o_ref[...] = xs_ref[...] + emb_ref[...]


def _embed(xp, xs, emb):
    tm = 512
    nct = TP // tm
    per_seq = DEC_SEQ // tm
    return pl.pallas_call(
        functools.partial(_embed_kernel, n_ctx_tiles=nct),
        out_shape=jax.ShapeDtypeStruct((T, D_MODEL), F32),
        grid=(T // tm,),
        in_specs=[
            pl.BlockSpec((tm, D_MODEL), lambda i: (jnp.minimum(i, nct - 1), 0)),
            pl.BlockSpec((tm, D_MODEL), lambda i: (jnp.maximum(i - nct, 0), 0)),
            pl.BlockSpec((tm, D_MODEL), lambda i: (jnp.maximum(i - nct, 0) % per_seq, 0)),
        ],
        out_specs=pl.BlockSpec((tm, D_MODEL), lambda i: (i, 0)),
        compiler_params=_cparams(("parallel",)),
        name="embed",
    )(xp, xs, emb)


def _mod_kernel(c_ref, w_ref, b_ref, o_ref):
    c = c_ref[...]
    s = _silu(c).astype(BF16)
    o_ref[...] = jnp.dot(s, w_ref[...].astype(BF16), preferred_element_type=F32) + b_ref[...]


def _mod_table(cond, w_mod, b_mod):
    tn = 1024
    n6 = 6 * D_MODEL
    return pl.pallas_call(
        _mod_kernel,
        out_shape=jax.ShapeDtypeStruct((DEPTH, COND_PAD, n6), F32),
        grid=(DEPTH, n6 // tn),
        in_specs=[
            pl.BlockSpec((COND_PAD, D_MODEL), lambda l, n: (0, 0)),
            pl.BlockSpec((None, D_MODEL, tn), lambda l, n: (l, 0, n)),
            pl.BlockSpec((None, 1, tn), lambda l, n: (l, 0, n)),
        ],
        out_specs=pl.BlockSpec((None, COND_PAD, tn), lambda l, n: (l, 0, n)),
        compiler_params=_cparams(("parallel", "parallel")),
        name="mod_table",
    )(cond, w_mod, b_mod.reshape(DEPTH, 1, n6))


def _proj_kernel(x_ref, g_ref, sh_ref, sc_ref, w_ref, wlr_ref, o_ref, lr_ref, h_ref):
    @pl.when(pl.program_id(1) == 0)
    def _():
        _fill_modnorm(x_ref, g_ref[...], sc_ref[...], sh_ref[...], h_ref, TM_PROJ)
        lr_ref[...] = jnp.dot(h_ref[...], wlr_ref[...], preferred_element_type=F32)

    o_ref[...] = jnp.dot(h_ref[...], w_ref[...], preferred_element_type=F32).astype(o_ref.dtype)


def _proj(x, norm_g, mod, layer, w_main, w_lr):
    tm, tn = TM_PROJ, TN_PROJ
    return pl.pallas_call(
        _proj_kernel,
        out_shape=(jax.ShapeDtypeStruct((T, N_MAIN), BF16),
                   jax.ShapeDtypeStruct((T, LR_PAD), F32)),
        grid=(T // tm, N_MAIN // tn),
        in_specs=[
            pl.BlockSpec((tm, D_MODEL), lambda m, n: (m, 0)),
            _layer_spec((1, D_MODEL), layer),
            _mod_spec(layer, 0, tm),
            _mod_spec(layer, 1, tm),
            pl.BlockSpec((None, D_MODEL, tn), lambda m, n: (layer, 0, n)),
            _layer_spec((D_MODEL, LR_PAD), layer),
        ],
        out_specs=(pl.BlockSpec((tm, tn), lambda m, n: (m, n)),
                   pl.BlockSpec((tm, LR_PAD), lambda m, n: (m, 0))),
        scratch_shapes=[pltpu.VMEM((tm, D_MODEL), BF16)],
        compiler_params=_cparams(("parallel", "arbitrary")),
        name="proj",
    )(x, norm_g, mod, mod, w_main, w_lr)


def _sgu_kernel(pu_ref, pv_ref, lng_ref, lnb_ref, ws_ref, bs_ref, a_ref, vn_ref):
    gv = jax.nn.gelu(pv_ref[...].astype(F32), approximate=True)
    mu = jnp.mean(gv, axis=-1, keepdims=True)
    d = gv - mu
    var = jnp.mean(d * d, axis=-1, keepdims=True)
    vn_ref[...] = (d * lax.rsqrt(var + EPS) * lng_ref[...] + lnb_ref[...]).astype(BF16)
    for c in range(TM_SGU // CHUNK_A):
        rows = slice(c * CHUNK_A, (c + 1) * CHUNK_A)
        for g in range(N_A_GROUPS):
            cols = slice(g * A_GROUP, (g + 1) * A_GROUP)
            f = jnp.dot(ws_ref[g], vn_ref[rows, cols], preferred_element_type=F32) + bs_ref[:, cols]
            gu = jax.nn.gelu(pu_ref[rows, cols].astype(F32), approximate=True)
            a_ref[rows, cols] = (gu * f).astype(BF16)


def _sgu(proj, layer, ln_g, ln_b, w_s, bs_full):
    tm = TM_SGU
    return pl.pallas_call(
        _sgu_kernel,
        out_shape=jax.ShapeDtypeStruct((T, D_A), BF16),
        grid=(T // tm,),
        in_specs=[
            pl.BlockSpec((tm, D_A), lambda i: (i, C_U // D_A)),
            pl.BlockSpec((tm, D_A), lambda i: (i, C_V // D_A)),
            _layer_spec((1, D_A), layer),
            _layer_spec((1, D_A), layer),
            _layer_spec((N_A_GROUPS, CHUNK_A, CHUNK_A), layer),
            _layer_spec((CHUNK_A, D_A), layer),
        ],
        out_specs=pl.BlockSpec((tm, D_A), lambda i: (i, 0)),
        scratch_shapes=[pltpu.VMEM((tm, D_A), BF16)],
        compiler_params=_cparams(("parallel",)),
        name="sgu",
    )(proj, proj, ln_g, ln_b, w_s, bs_full)


def _split3(x):
    hi = x.astype(BF16)
    r = x - hi.astype(F32)
    mid = r.astype(BF16)
    lo = (r - mid.astype(F32)).astype(BF16)
    return hi, mid, lo


def _tri_cumsum(tri, x):
    hi, mid, lo = _split3(x)
    return (jnp.dot(tri, hi, preferred_element_type=F32)
            + jnp.dot(tri, mid, preferred_element_type=F32)
            + jnp.dot(tri, lo, preferred_element_type=F32))


_NT = (((1,), (1,)), ((), ()))
_TN = (((0,), (0,)), ((), ()))


def _gla_decay(lr_ref, a2_ref, ab_ref, tb, backward):
    z = jnp.dot(lr_ref[...].astype(BF16), a2_ref[...], preferred_element_type=F32) + ab_ref[...]
    la = _log_sigmoid(z) / GLA_TAU
    r = lax.broadcasted_iota(I32, (tb, tb), 0)
    c = lax.broadcasted_iota(I32, (tb, tb), 1)
    same = (r // GLA_CHUNK) == (c // GLA_CHUNK)
    tri = (same & ((c >= r) if backward else (c <= r))).astype(BF16)
    return _tri_cumsum(tri, la)


def _gla_block(dirs, tb):
    nchunk = tb // GLA_CHUNK
    scale = GLA_DK_HEAD ** -0.5
    rr = lax.broadcasted_iota(I32, (GLA_CHUNK, GLA_CHUNK), 0)
    cc = lax.broadcasted_iota(I32, (GLA_CHUNK, GLA_CHUNK), 1)
    units = []
    for (lr_ref, a2_ref, ab_ref, q_ref, k_ref, v_ref, o_ref, st_ref, backward) in dirs:
        b = _gla_decay(lr_ref, a2_ref, ab_ref, tb, backward)
        eb = jnp.exp(b)
        enb = jnp.exp(-b)
        mask = (cc >= rr) if backward else (cc <= rr)
        order = range(nchunk - 1, -1, -1) if backward else range(nchunk)
        for ci in order:
            rows = slice(ci * GLA_CHUNK, (ci + 1) * GLA_CHUNK)
            last = ci * GLA_CHUNK if backward else (ci + 1) * GLA_CHUNK - 1
            btot = b[last:last + 1, :]
            eke = jnp.exp(btot - b[rows, :])
            gdec = jnp.exp(btot)
            for h in range(N_GLA_HEADS):
                kc = slice(h * GLA_DK_HEAD, (h + 1) * GLA_DK_HEAD)
                vc = slice(h * GLA_DV_HEAD, (h + 1) * GLA_DV_HEAD)
                q = q_ref[rows, kc].astype(F32) * scale
                k = k_ref[rows, kc].astype(F32)
                qd = (q * eb[rows, kc]).astype(BF16)
                kd = (k * enb[rows, kc]).astype(BF16)
                ke = (k * eke[:, kc]).astype(BF16)
                vv = v_ref[rows, vc]
                att = lax.dot_general(qd, kd, _NT, preferred_element_type=F32)
                att = jnp.where(mask, att, 0.0).astype(BF16)
                dst = lax.dot_general(vv, ke, _TN, preferred_element_type=F32)
                units.append(dict(o_ref=o_ref, st_ref=st_ref, rows=rows, vc=vc, h=h, qd=qd, vv=vv,
                                  att=att, dst=dst, gdec=gdec[:, kc]))
    for u in units:
        u["o_intra"] = jnp.dot(u["att"], u["vv"], preferred_element_type=F32)
    states = {}
    for u in units:
        key = (id(u["st_ref"]), u["h"])
        st = states[key] if key in states else u["st_ref"][u["h"]]
        u["o_ref"][u["rows"], u["vc"]] = u["o_intra"] + lax.dot_general(
            u["qd"], st.astype(BF16), _NT, preferred_element_type=F32)
        states[key] = st * u["gdec"] + u["dst"]
    for (_, _, _, _, _, _, _, st_ref, _) in dirs:
        for h in range(N_GLA_HEADS):
            st_ref[h] = states[(id(st_ref), h)]


def _gla_kernel(fwd_ref, bwd_ref, flag_ref, seq_ref,
                lrf_ref, lrb_ref, a2f_ref, a2b_ref, abf_ref, abb_ref,
                qf_ref, kf_ref, vf_ref, qb_ref, kb_ref, vb_ref, s0_ref, sprev_ref,
                of_ref, ob_ref, sfin_ref, stf_ref, stb_ref):
    del sprev_ref
    i = pl.program_id(0)
    first = (flag_ref[i] & 1) == 1
    last = (flag_ref[i] & 2) == 2
    is_ctx = seq_ref[i] < BATCH

    @pl.when(first & is_ctx)
    def _():
        stf_ref[...] = jnp.zeros_like(stf_ref)
        stb_ref[...] = jnp.zeros_like(stb_ref)

    @pl.when(first & jnp.logical_not(is_ctx))
    def _():
        for h in range(N_GLA_HEADS):
            stf_ref[h] = s0_ref[0, h].T
            stb_ref[h] = s0_ref[1, h].T

    _gla_block([(lrf_ref, a2f_ref, abf_ref, qf_ref, kf_ref, vf_ref, of_ref, stf_ref, False),
                (lrb_ref, a2b_ref, abb_ref, qb_ref, kb_ref, vb_ref, ob_ref, stb_ref, True)], TB_GLA)

    @pl.when(last & is_ctx)
    def _():
        for h in range(N_GLA_HEADS):
            sfin_ref[0, h] = stf_ref[h].T
            sfin_ref[1, h] = stb_ref[h].T


def _gla_tables():
    tb = TB_GLA
    fwd, bwd, flag, seq = [], [], [], []
    for s, (row0, seqlen) in enumerate([(b * SEQ, SEQ) for b in range(BATCH)]
                                       + [(TP + b * DEC_SEQ, DEC_SEQ) for b in range(DEC_BATCH)]):
        nb = seqlen // tb
        for j in range(nb):
            fwd.append(row0 // tb + j)
            bwd.append(row0 // tb + nb - 1 - j)
            flag.append((1 if j == 0 else 0) | (2 if j == nb - 1 else 0))
            seq.append(s)
    return tuple(np.asarray(t, np.int32) for t in (fwd, bwd, flag, seq))


def _gla(proj, lr, a2f, a2b, ab, s0, s_all, layer):
    tb = TB_GLA
    fwd_t, bwd_t, flag_t, seq_t = _gla_tables()
    nsteps = fwd_t.shape[0]
    n_prefetch = 4

    def fwd(i, fw, bw, fl, sq):
        return fw[i]

    def bwd(i, fw, bw, fl, sq):
        return bw[i]

    def col_spec(rowfn, width, col):
        return pl.BlockSpec((tb, width), lambda i, *t: (rowfn(i, *t), col // width))

    state_block = (2, N_GLA_HEADS, GLA_DK_HEAD, GLA_DV_HEAD)
    in_specs = [
        pl.BlockSpec((tb, LR_PAD), lambda i, *t: (fwd(i, *t), 0)),
        pl.BlockSpec((tb, LR_PAD), lambda i, *t: (bwd(i, *t), 0)),
        _layer_spec((LR_PAD, GLA_DK), layer),
        _layer_spec((LR_PAD, GLA_DK), layer),
        pl.BlockSpec((None, None, 1, GLA_DK), lambda i, *t: (layer, 0, 0, 0)),
        pl.BlockSpec((None, None, 1, GLA_DK), lambda i, *t: (layer, 1, 0, 0)),
        col_spec(fwd, GLA_DK, C_Q), col_spec(fwd, GLA_DK, C_K), col_spec(fwd, GLA_DV, C_VG),
        col_spec(bwd, GLA_DK, C_Q), col_spec(bwd, GLA_DK, C_K), col_spec(bwd, GLA_DV, C_VG),
        pl.BlockSpec((None, None) + state_block,
                     lambda i, fw, bw, fl, sq: (jnp.maximum(sq[i] - BATCH, 0), layer, 0, 0, 0, 0)),
        pl.BlockSpec(memory_space=pl.ANY),
    ]
    out_shape = (jax.ShapeDtypeStruct((T, GLA_DV), F32),
                 jax.ShapeDtypeStruct((T, GLA_DV), F32),
                 jax.ShapeDtypeStruct((BATCH, DEPTH) + state_block, F32))
    out_specs = (pl.BlockSpec((tb, GLA_DV), lambda i, *t: (fwd(i, *t), 0)),
                 pl.BlockSpec((tb, GLA_DV), lambda i, *t: (bwd(i, *t), 0)),
                 pl.BlockSpec((None, None) + state_block,
                              lambda i, fw, bw, fl, sq: (jnp.minimum(sq[i], BATCH - 1), layer, 0, 0, 0, 0)))
    return pl.pallas_call(
        _gla_kernel,
        out_shape=out_shape,
        grid_spec=pltpu.PrefetchScalarGridSpec(
            num_scalar_prefetch=n_prefetch,
            grid=(nsteps,),
            in_specs=in_specs,
            out_specs=out_specs,
            scratch_shapes=[pltpu.VMEM((N_GLA_HEADS, GLA_DV_HEAD, GLA_DK_HEAD), F32),
                            pltpu.VMEM((N_GLA_HEADS, GLA_DV_HEAD, GLA_DK_HEAD), F32)],
        ),
        input_output_aliases={n_prefetch + len(in_specs) - 1: 2},
        compiler_params=_cparams(("arbitrary",)),
        name="gla",
    )(fwd_t, bwd_t, flag_t, seq_t, lr, lr, a2f, a2b, ab, ab,
      proj, proj, proj, proj, proj, proj, s0, s_all)


def _gla_post_kernel(of_ref, ob_ref, pr_ref, g_ref, o_ref):
    for h in range(N_GLA_HEADS):
        vc = slice(h * GLA_DV_HEAD, (h + 1) * GLA_DV_HEAD)
        o = of_ref[:, vc] + ob_ref[:, vc]
        o = o * lax.rsqrt(jnp.mean(o * o, axis=-1, keepdims=True) + EPS)
        o_ref[:, vc] = (o * g_ref[:, vc] * _silu(pr_ref[:, vc].astype(F32))).astype(BF16)


def _gla_post(o_f, o_b, proj, layer, norm_g):
    tm = TM_POST
    return pl.pallas_call(
        _gla_post_kernel,
        out_shape=jax.ShapeDtypeStruct((T, GLA_DV), BF16),
        grid=(T // tm,),
        in_specs=[
            pl.BlockSpec((tm, GLA_DV), lambda i: (i, 0)),
            pl.BlockSpec((tm, GLA_DV), lambda i: (i, 0)),
            pl.BlockSpec((tm, GLA_DV), lambda i: (i, C_R // GLA_DV)),
            _layer_spec((1, GLA_DV), layer),
        ],
        out_specs=pl.BlockSpec((tm, GLA_DV), lambda i: (i, 0)),
        compiler_params=_cparams(("parallel",)),
        name="gla_post",
    )(o_f, o_b, proj, norm_g)


def _mixout_kernel(a_ref, o_ref, ga_ref, gb_ref, x_ref, g1_ref, wa_ref, wb_ref, wo_ref, out_ref):
    ya = jnp.dot(a_ref[...], wa_ref[...], preferred_element_type=F32)
    yb = jnp.dot(o_ref[...], wb_ref[...], preferred_element_type=F32)
    merged = (jax.nn.sigmoid(ga_ref[...].astype(F32)) * ya
              + jax.nn.sigmoid(gb_ref[...].astype(F32)) * yb).astype(BF16)
    y = jnp.dot(merged, wo_ref[...], preferred_element_type=F32)
    out_ref[...] = x_ref[...] + g1_ref[...] * y


def _resident_spec(shape, layer):
    zeros = (0,) * len(shape)
    return pl.BlockSpec((None,) + tuple(shape), lambda *_: (layer,) + zeros,
                        pipeline_mode=pl.Buffered(1))


def _mixout(a, o, proj, x, mod, layer, w_a, w_b, w_o):
    tm = TM_MIX
    return pl.pallas_call(
        _mixout_kernel,
        out_shape=jax.ShapeDtypeStruct((T, D_MODEL), F32),
        grid=(T // tm,),
        in_specs=[
            pl.BlockSpec((tm, D_A), lambda m: (m, 0)),
            pl.BlockSpec((tm, GLA_DV), lambda m: (m, 0)),
            pl.BlockSpec((tm, D_MODEL), lambda m: (m, C_GA // D_MODEL)),
            pl.BlockSpec((tm, D_MODEL), lambda m: (m, C_GB // D_MODEL)),
            pl.BlockSpec((tm, D_MODEL), lambda m: (m, 0)),
            _mod_spec(layer, 2, tm),
            _resident_spec((D_A, D_MODEL), layer),
            _resident_spec((GLA_DV, D_MODEL), layer),
            _resident_spec((D_MODEL, D_MODEL), layer),
        ],
        out_specs=pl.BlockSpec((tm, D_MODEL), lambda m: (m, 0)),
        compiler_params=_cparams(("parallel",)),
        name="mixout",
    )(a, o, proj, proj, x, mod, w_a, w_b, w_o)


def _ffn_kernel(x_ref, g_ref, sh_ref, sc_ref, g2_ref, w1_ref, w3_ref, w2_ref, o_ref, h_ref, *, nf):
    f = pl.program_id(1)

    @pl.when(f == 0)
    def _():
        _fill_modnorm(x_ref, g_ref[...], sc_ref[...], sh_ref[...], h_ref, TM_FFN)
        o_ref[...] = jnp.zeros_like(o_ref)

    _swiglu_accumulate(h_ref, w1_ref, w3_ref, w2_ref, o_ref, TM_FFN)

    @pl.when(f == nf - 1)
    def _():
        g2 = g2_ref[...]

        def body(i, carry):
            r = pl.multiple_of(i * SLAB, SLAB)
            o_ref[pl.ds(r, SLAB), :] = x_ref[pl.ds(r, SLAB), :] + g2 * o_ref[pl.ds(r, SLAB), :]
            return carry
        lax.fori_loop(0, TM_FFN // SLAB, body, 0)


def _ffn(x, norm_g, mod, layer, j, w1, w3, w2):
    tm, tf = TM_FFN, TF_FFN
    nf = D_FF // tf
    return pl.pallas_call(
        functools.partial(_ffn_kernel, nf=nf),
        out_shape=jax.ShapeDtypeStruct((T, D_MODEL), F32),
        grid=(T // tm, nf),
        in_specs=[
            pl.BlockSpec((tm, D_MODEL), lambda m, f: (m, 0)),
            _layer_spec((1, D_MODEL), layer),
            _mod_spec(layer, 3, tm),
            _mod_spec(layer, 4, tm),
            _mod_spec(layer, 5, tm),
            pl.BlockSpec((None, D_MODEL, tf), lambda m, f: (j, 0, f)),
            pl.BlockSpec((None, D_MODEL, tf), lambda m, f: (j, 0, f)),
            pl.BlockSpec((None, tf, D_MODEL), lambda m, f: (j, f, 0)),
        ],
        out_specs=pl.BlockSpec((tm, D_MODEL), lambda m, f: (m, 0)),
        scratch_shapes=[pltpu.VMEM((tm, D_MODEL), BF16)],
        compiler_params=_cparams(("parallel", "arbitrary")),
        name="ffn",
    )(x, norm_g, mod, mod, mod, w1, w3, w2)


def _router_kernel(x_ref, g_ref, sh_ref, sc_ref, rwt_ref, rb_ref, h_ref, info_ref, p_ref, cnt_ref,
                   carry_ref):
    i = pl.program_id(0)
    tm = TM_ROUTE

    @pl.when(i == 0)
    def _():
        carry_ref[...] = jnp.zeros_like(carry_ref)

    _fill_modnorm(x_ref, g_ref[...], sc_ref[...], sh_ref[...], h_ref, tm)
    logits = lax.dot_general(rwt_ref[...], h_ref[...], _NT, preferred_element_type=F32,
                             precision=lax.Precision.HIGHEST) + rb_ref[:, 0:1]
    eidx = lax.broadcasted_iota(I32, (N_EXPERTS, tm), 0).astype(F32)
    none = float(N_EXPERTS)
    m1 = jnp.max(logits, axis=0, keepdims=True)
    i1 = jnp.min(jnp.where(logits == m1, eidx, none), axis=0, keepdims=True)
    rest = jnp.where(eidx == i1, -jnp.inf, logits)
    m2 = jnp.max(rest, axis=0, keepdims=True)
    i2 = jnp.min(jnp.where(rest == m2, eidx, none), axis=0, keepdims=True)
    e = jnp.exp(m2 - m1)
    den = 1.0 + e
    sel1 = eidx == i1
    sel2 = eidx == i2
    onehot = (sel1 | sel2).astype(BF16)
    r = lax.broadcasted_iota(I32, (tm, tm), 0)
    c = lax.broadcasted_iota(I32, (tm, tm), 1)
    upper = (r <= c).astype(BF16)
    cnt = jnp.dot(onehot, upper, preferred_element_type=F32) + carry_ref[:, 0:1]
    r1 = jnp.sum(jnp.where(sel1, cnt, 0.0), axis=0, keepdims=True) - 1.0
    r2 = jnp.sum(jnp.where(sel2, cnt, 0.0), axis=0, keepdims=True) - 1.0
    info_ref[...] = jnp.zeros_like(info_ref)
    info_ref[0:1, :] = i1.astype(I32)
    info_ref[1:2, :] = i2.astype(I32)
    info_ref[2:3, :] = r1.astype(I32)
    info_ref[3:4, :] = r2.astype(I32)
    p_ref[...] = jnp.zeros_like(p_ref)
    p_ref[0:1, :] = 1.0 / den
    p_ref[1:2, :] = e / den
    total = cnt[:, tm - 1:tm]
    carry_ref[...] = jnp.broadcast_to(total, carry_ref.shape)
    cnt_ref[...] = jnp.broadcast_to(total, cnt_ref.shape).astype(I32)


def _router(x, norm_g, mod, layer, j, rw_t, rb):
    tm = TM_ROUTE
    nt = T // tm
    return pl.pallas_call(
        _router_kernel,
        out_shape=(jax.ShapeDtypeStruct((T, D_MODEL), F32),
                   jax.ShapeDtypeStruct((nt, 8, tm), I32),
                   jax.ShapeDtypeStruct((nt, 8, tm), F32),
                   jax.ShapeDtypeStruct((N_EXPERTS, 128), I32)),
        grid=(nt,),
        in_specs=[
            pl.BlockSpec((tm, D_MODEL), lambda i: (i, 0)),
            _layer_spec((1, D_MODEL), layer),
            _mod_spec(layer, 3, tm),
            _mod_spec(layer, 4, tm),
            _layer_spec((N_EXPERTS, D_MODEL), j),
            _layer_spec((N_EXPERTS, 128), j),
        ],
        out_specs=(pl.BlockSpec((tm, D_MODEL), lambda i: (i, 0)),
                   pl.BlockSpec((None, 8, tm), lambda i: (i, 0, 0)),
                   pl.BlockSpec((None, 8, tm), lambda i: (i, 0, 0)),
                   pl.BlockSpec((N_EXPERTS, 128), lambda i: (0, 0))),
        scratch_shapes=[pltpu.VMEM((N_EXPERTS, 128), F32)],
        compiler_params=_cparams(("arbitrary",)),
        name="router",
    )(x, norm_g, mod, mod, rw_t, rb)


def _dispatch_kernel(pos1_ref, pos2_ref, zstart_ref, zcount_ref, h_ref, o_hbm, zrow_ref, sem):
    i = pl.program_id(0)
    td = TD_DISP
    base = i * td

    def row_copy(src_row, dst_row):
        return pltpu.make_async_copy(h_ref.at[pl.ds(src_row, 1)], o_hbm.at[pl.ds(dst_row, 1)], sem)

    def zero_copy(dst_row):
        return pltpu.make_async_copy(zrow_ref, o_hbm.at[pl.ds(dst_row, 1)], sem)

    @pl.when(i == 0)
    def _():
        zrow_ref[...] = jnp.zeros_like(zrow_ref)
        for e in range(N_EXPERTS + 1):
            def zbody(r, carry):
                zero_copy(zstart_ref[e] + r).start()
                return carry
            lax.fori_loop(0, zcount_ref[e], zbody, 0)

            def zwait(r, carry):
                zero_copy(0).wait()
                return carry
            lax.fori_loop(0, zcount_ref[e], zwait, 0)

    def body(r, carry):
        row_copy(r, pos1_ref[base + r]).start()
        row_copy(r, pos2_ref[base + r]).start()
        return carry
    lax.fori_loop(0, td, body, 0, unroll=DMA_UNROLL)

    tile_copy = pltpu.make_async_copy(h_ref, o_hbm.at[pl.ds(0, td)], sem)
    tile_copy.wait()
    tile_copy.wait()


def _dispatch(pos1, pos2, zstart, zcount, h):
    td = TD_DISP
    return pl.pallas_call(
        _dispatch_kernel,
        out_shape=jax.ShapeDtypeStruct((ROWS_CAP, D_MODEL), F32),
        grid_spec=pltpu.PrefetchScalarGridSpec(
            num_scalar_prefetch=4,
            grid=(T // td,),
            in_specs=[pl.BlockSpec((td, D_MODEL), lambda i, *_: (i, 0))],
            out_specs=pl.BlockSpec(memory_space=pl.ANY),
            scratch_shapes=[pltpu.VMEM((1, D_MODEL), F32), pltpu.SemaphoreType.DMA],
        ),
        compiler_params=_cparams(("arbitrary",)),
        name="dispatch",
    )(pos1, pos2, zstart, zcount, h)


def _expert_kernel(te_ref, nact_ref, hs_ref, w1_ref, w3_ref, w2_ref, y_ref, hb_ref):
    i = pl.program_id(0)
    f = pl.program_id(1)
    active = i < nact_ref[0]

    @pl.when(f == 0)
    def _():
        y_ref[...] = jnp.zeros_like(y_ref)

    @pl.when(active & (f == 0))
    def _():
        def body(s, carry):
            r = pl.multiple_of(s * SLAB, SLAB)
            hb_ref[pl.ds(r, SLAB), :] = hs_ref[pl.ds(r, SLAB), :].astype(BF16)
            return carry
        lax.fori_loop(0, TR_EXP // SLAB, body, 0)

    @pl.when(active)
    def _():
        _swiglu_accumulate(hb_ref, w1_ref, w3_ref, w2_ref, y_ref, TR_EXP)


def _experts(tile_expert, n_active, hs, j, w1, w3, w2):
    tr, tf = TR_EXP, TF_EXP
    nf = D_FF // tf

    def row_map(i, f, te, na):
        return (jnp.minimum(i, na[0] - 1), 0)

    def out_map(i, f, te, na):
        return (i, 0)

    def w13_map(i, f, te, na):
        return (j, te[i], 0, jnp.where(i < na[0], f, nf - 1))

    def w2_map(i, f, te, na):
        return (j, te[i], jnp.where(i < na[0], f, nf - 1), 0)

    return pl.pallas_call(
        _expert_kernel,
        out_shape=jax.ShapeDtypeStruct((ROWS_CAP, D_MODEL), F32),
        grid_spec=pltpu.PrefetchScalarGridSpec(
            num_scalar_prefetch=2,
            grid=(NT_EXP, nf),
            in_specs=[
                pl.BlockSpec((tr, D_MODEL), row_map),
                pl.BlockSpec((None, None, D_MODEL, tf), w13_map),
                pl.BlockSpec((None, None, D_MODEL, tf), w13_map),
                pl.BlockSpec((None, None, tf, D_MODEL), w2_map),
            ],
            out_specs=pl.BlockSpec((tr, D_MODEL), out_map),
            scratch_shapes=[pltpu.VMEM((tr, D_MODEL), BF16)],
        ),
        compiler_params=_cparams(("arbitrary", "arbitrary")),
        name="experts",
    )(tile_expert, n_active, hs, w1, w3, w2)


def _combine_kernel(pos1_ref, pos2_ref, x_ref, p1_ref, p2_ref, g2_ref, y_hbm, o_ref, b1_ref, b2_ref, sem):
    i = pl.program_id(0)
    tc = TC_COMB
    base = i * tc

    def row_copy(buf, src_row, dst_row):
        return pltpu.make_async_copy(y_hbm.at[pl.ds(src_row, 1)], buf.at[pl.ds(dst_row, 1)], sem)

    def body(r, carry):
        row_copy(b1_ref, pos1_ref[base + r], r).start()
        row_copy(b2_ref, pos2_ref[base + r], r).start()
        return carry
    lax.fori_loop(0, tc, body, 0, unroll=DMA_UNROLL)

    pltpu.make_async_copy(y_hbm.at[pl.ds(0, tc)], b1_ref, sem).wait()
    pltpu.make_async_copy(y_hbm.at[pl.ds(0, tc)], b2_ref, sem).wait()

    mix = p1_ref[...] * b1_ref[...] + p2_ref[...] * b2_ref[...]
    o_ref[...] = x_ref[...] + g2_ref[...] * mix


def _combine(pos1, pos2, x, p1, p2, mod, layer, y):
    tc = TC_COMB
    return pl.pallas_call(
        _combine_kernel,
        out_shape=jax.ShapeDtypeStruct((T, D_MODEL), F32),
        grid_spec=pltpu.PrefetchScalarGridSpec(
            num_scalar_prefetch=2,
            grid=(T // tc,),
            in_specs=[
                pl.BlockSpec((tc, D_MODEL), lambda i, *_: (i, 0)),
                pl.BlockSpec((tc, 1), lambda i, *_: (i, 0)),
                pl.BlockSpec((tc, 1), lambda i, *_: (i, 0)),
                _mod_spec(layer, 5, tc),
                pl.BlockSpec(memory_space=pl.ANY),
            ],
            out_specs=pl.BlockSpec((tc, D_MODEL), lambda i, *_: (i, 0)),
            scratch_shapes=[pltpu.VMEM((tc, D_MODEL), F32), pltpu.VMEM((tc, D_MODEL), F32),
                            pltpu.SemaphoreType.DMA],
        ),
        compiler_params=_cparams(("arbitrary",)),
        name="combine",
    )(pos1, pos2, x, p1, p2, mod, y)


def _moe(x, norm_g, mod, layer, j, rw_t, rb, w1, w3, w2):
    h, info, probs, counts = _router(x, norm_g, mod, layer, j, rw_t, rb)
    counts = counts[:, 0]
    cap = ((counts + TR_EXP - 1) // TR_EXP) * TR_EXP
    ends = jnp.cumsum(cap)
    starts = ends - cap
    e1 = info[:, 0, :].reshape(T)
    e2 = info[:, 1, :].reshape(T)
    pos1 = starts[e1] + info[:, 2, :].reshape(T)
    pos2 = starts[e2] + info[:, 3, :].reshape(T)
    tile_end = ends // TR_EXP
    n_active = tile_end[-1:]
    tiles = jnp.minimum(jnp.arange(NT_EXP, dtype=I32), n_active[0] - 1)
    tile_expert = jnp.sum((tiles[:, None] >= tile_end[None, :]).astype(I32), axis=1)
    zstart = jnp.concatenate([starts + counts, ends[-1:]])
    zcount = jnp.concatenate([cap - counts, ROWS_CAP - ends[-1:]])
    hs = _dispatch(pos1, pos2, zstart, zcount, h)
    y = _experts(tile_expert, n_active, hs, j, w1, w3, w2)
    p1 = probs[:, 0, :].reshape(T, 1)
    p2 = probs[:, 1, :].reshape(T, 1)
    return _combine(pos1, pos2, x, p1, p2, mod, layer, y)


def _final_kernel(x_ref, g_ref, yp_ref, ys_ref, *, n_ctx_tiles):
    i = pl.program_id(0)
    x = x_ref[...]
    y = x * lax.rsqrt(jnp.mean(x * x, axis=-1, keepdims=True) + EPS) * g_ref[...]

    @pl.when(i < n_ctx_tiles)
    def _():
        yp_ref[...] = y

    @pl.when(i >= n_ctx_tiles)
    def _():
        ys_ref[...] = y


def _final_norm(x, g):
    tm = TM_NORM
    nct = TP // tm
    return pl.pallas_call(
        functools.partial(_final_kernel, n_ctx_tiles=nct),
        out_shape=(jax.ShapeDtypeStruct((TP, D_MODEL), F32),
                   jax.ShapeDtypeStruct((TS, D_MODEL), F32)),
        grid=(T // tm,),
        in_specs=[pl.BlockSpec((tm, D_MODEL), lambda i: (i, 0)),
                  pl.BlockSpec((1, D_MODEL), lambda i: (0, 0))],
        out_specs=(pl.BlockSpec((tm, D_MODEL), lambda i: (jnp.minimum(i, nct - 1), 0)),
                   pl.BlockSpec((tm, D_MODEL), lambda i: (jnp.maximum(i - nct, 0), 0))),
        compiler_params=_cparams(("arbitrary",)),
        name="final_norm",
    )(x, g)


def _grid_pos_embed():
    rows = DEC_SEQ // GRID_W
    r, col = jnp.meshgrid(jnp.arange(rows, dtype=F32), jnp.arange(GRID_W, dtype=F32), indexing='ij')
    r = r.reshape(-1)
    col = col.reshape(-1)
    quarter = D_MODEL // 4
    freq = jnp.exp(-math.log(10000.0) * jnp.arange(quarter, dtype=F32) / quarter)
    ar = r[:, None] * freq
    ac = col[:, None] * freq
    return jnp.concatenate([jnp.sin(ar), jnp.cos(ar), jnp.sin(ac), jnp.cos(ac)], axis=-1)


def kernel(x_prompt, x_sample, state_gla, c, c_ctx, norm1_g, norm2_g, w_mod, b_mod, w_in,
           sgu_ln_g, sgu_ln_b, w_spatial, b_spatial, gla_a2, gla_ab, gla_norm_g,
           w_branch_a, w_branch_b, w_out, ffn_w1, ffn_w3, ffn_w2,
           moe_router, moe_router_b, moe_w1, moe_w3, moe_w2, final_g):
    x = _embed(x_prompt.reshape(TP, D_MODEL), x_sample.reshape(TS, D_MODEL), _grid_pos_embed())

    cond = jnp.zeros((COND_PAD, D_MODEL), F32).at[0].set(c_ctx).at[1:N_COND].set(c)
    mod = _mod_table(cond, w_mod, b_mod).reshape(DEPTH, COND_PAD, 6, 1, D_MODEL)

    lr_end = W_IN_LR + 2 * GLA_RANK
    w_main = jnp.concatenate([w_in[:, :, lr_end:], w_in[:, :, :W_IN_LR]], axis=2).astype(BF16)
    w_lr = jnp.pad(w_in[:, :, W_IN_LR:lr_end], ((0, 0), (0, 0), (0, LR_PAD - 2 * GLA_RANK))).astype(BF16)
    norm1 = norm1_g.reshape(DEPTH, 1, D_MODEL)
    norm2 = norm2_g.reshape(DEPTH, 1, D_MODEL)
    ln_g = sgu_ln_g.reshape(DEPTH, 1, D_A)
    ln_b = sgu_ln_b.reshape(DEPTH, 1, D_A)
    w_s = w_spatial.astype(BF16)
    bs_full = jnp.repeat(jnp.swapaxes(b_spatial, 1, 2), A_GROUP, axis=2)
    a2_pad = jnp.zeros((DEPTH, LR_PAD, GLA_DK), F32)
    a2f = a2_pad.at[:, :GLA_RANK].set(gla_a2[:, 0]).astype(BF16)
    a2b = a2_pad.at[:, GLA_RANK:2 * GLA_RANK].set(gla_a2[:, 1]).astype(BF16)
    ab = gla_ab.reshape(DEPTH, 2, 1, GLA_DK)
    gla_g = gla_norm_g.reshape(DEPTH, 1, GLA_DV)
    w_a = w_branch_a.astype(BF16)
    w_b = w_branch_b.astype(BF16)
    w_o = w_out.astype(BF16)
    f_w1, f_w3, f_w2 = ffn_w1.astype(BF16), ffn_w3.astype(BF16), ffn_w2.astype(BF16)
    m_w1, m_w3, m_w2 = moe_w1.astype(BF16), moe_w3.astype(BF16), moe_w2.astype(BF16)
    rw_t = jnp.swapaxes(moe_router, 1, 2)
    rb = jnp.broadcast_to(moe_router_b[:, :, None], moe_router_b.shape + (128,))

    new_state = jnp.zeros((BATCH, DEPTH, 2, N_GLA_HEADS, GLA_DK_HEAD, GLA_DV_HEAD), F32)
    for l in range(DEPTH):
        proj, lr = _proj(x, norm1, mod, l, w_main, w_lr)
        a = _sgu(proj, l, ln_g, ln_b, w_s, bs_full)
        o_f, o_b, new_state = _gla(proj, lr, a2f, a2b, ab, state_gla, new_state, l)
        o = _gla_post(o_f, o_b, proj, l, gla_g)
        x = _mixout(a, o, proj, x, mod, l, w_a, w_b, w_o)
        j = l // 2
        if l % 2 == 0:
            x = _ffn(x, norm2, mod, l, j, f_w1, f_w3, f_w2)
        else:
            x = _moe(x, norm2, mod, l, j, rw_t, rb, m_w1, m_w3, m_w2)

    y_prompt, y_sample = _final_norm(x, final_g[None])
    return (y_prompt.reshape(BATCH, SEQ, D_MODEL), y_sample.reshape(DEC_BATCH, DEC_SEQ, D_MODEL),
            new_state)
```

```python
import functools
import math

import numpy as np
import jax
import jax.numpy as jnp
from jax import lax
from jax.experimental import pallas as pl
from jax.experimental.pallas import tpu as pltpu

F32 = jnp.float32
BF16 = jnp.bfloat16
I32 = jnp.int32

D_MODEL = 2048
BATCH = 32
SEQ = 256
DEPTH = 4
DEC_BATCH = 8
DEC_SEQ = 2048
GRID_W = 64
CHUNK_A = 128
D_A = 1024
N_A_GROUPS = 8
A_GROUP = D_A // N_A_GROUPS
N_GLA_HEADS = 4
GLA_DK = D_MODEL // 4
GLA_DV = D_MODEL // 2
GLA_DK_HEAD = GLA_DK // N_GLA_HEADS
GLA_DV_HEAD = GLA_DV // N_GLA_HEADS
GLA_RANK = 16
GLA_TAU = 16.0
GLA_CHUNK = 64
D_FF = 5632
N_EXPERTS = 8
TOP_K = 2
EPS = 1e-6

TP = BATCH * SEQ
TS = DEC_BATCH * DEC_SEQ
T = TP + TS
N_COND = 1 + DEC_BATCH
COND_PAD = 16

W_IN_LR = 2 * D_A + 2 * GLA_DK + 2 * GLA_DV
C_GA, C_GB, C_U, C_V, C_Q, C_K, C_VG, C_R = 0, 2048, 4096, 5120, 6144, 6656, 7168, 8192
N_MAIN = 9216
LR_PAD = 128

VMEM_LIMIT = 56 * 1024 * 1024

TM_PROJ, TN_PROJ = 1024, 1024
TM_EMBED = 512
TM_SGU = 512
TB_GLA = 256
TM_MIX = 256
TM_FFN, TF_FFN = 512, 512
TM_ROUTE = 512
TD_DISP = 256
TR_EXP, TF_EXP = 512, 512
TC_COMB = 256
SLAB = 32
SLAB_UNROLL = 4
RS_SWIGLU = 512
DMA_UNROLL = 8

NT_EXP = (TOP_K * T) // TR_EXP + N_EXPERTS
ROWS_CAP = NT_EXP * TR_EXP


def _cparams(sem):
    return pltpu.CompilerParams(dimension_semantics=sem, vmem_limit_bytes=VMEM_LIMIT)


def _mod_row(i, tm):
    return jnp.maximum((i * tm) // DEC_SEQ - (TP // DEC_SEQ - 1), 0)


def _mod_spec(layer, j, tm, width=D_MODEL):
    return pl.BlockSpec((None, None, None, 1, width),
                        lambda m, *_: (layer, _mod_row(m, tm), j, 0, 0))


def _layer_spec(shape, layer):
    zeros = (0,) * len(shape)
    return pl.BlockSpec((None,) + tuple(shape), lambda *_: (layer,) + zeros)


def _modnorm(x, gain, sh):
    return x * lax.rsqrt(jnp.mean(x * x, axis=-1, keepdims=True) + EPS) * gain + sh


def _fill_modnorm(x_ref, g, sc, sh, h_ref, rows):
    gain = g * (1.0 + sc)

    def body(i, carry):
        r = pl.multiple_of(i * SLAB, SLAB)
        h_ref[pl.ds(r, SLAB), :] = _modnorm(x_ref[pl.ds(r, SLAB), :], gain, sh).astype(h_ref.dtype)
        return carry
    lax.fori_loop(0, rows // SLAB, body, 0, unroll=SLAB_UNROLL)


def _silu(x):
    return x * jax.nn.sigmoid(x)


def _log_sigmoid(x):
    return jnp.minimum(x, 0.0) - jnp.log(1.0 + jnp.exp(-jnp.abs(x)))


def _swiglu_accumulate(h_ref, w1_ref, w3_ref, w2_ref, acc_ref, rows):
    for r in range(rows // RS_SWIGLU):
        sl = slice(r * RS_SWIGLU, (r + 1) * RS_SWIGLU)
        h = h_ref[sl, :]
        u = _silu(jnp.dot(h, w1_ref[...], preferred_element_type=F32)) * jnp.dot(
            h, w3_ref[...], preferred_element_type=F32)
        acc_ref[sl, :] += jnp.dot(u.astype(BF16), w2_ref[...], preferred_element_type=F32)


def _embed_kernel(xp_ref, xs_ref, emb_ref, g_ref, sh_ref, sc_ref, o_ref, h_ref, *, n_ctx_tiles):
    i = pl.program_id(0)

    @pl.when(i < n_ctx_tiles)
    def _():
        o_ref[...] = xp_ref[...]

    @pl.when(i >= n_ctx_tiles)
    def _():
        o_ref[...] = xs_ref[...] + emb_ref[...]

    _fill_modnorm(o_ref, g_ref[...], sc_ref[...], sh_ref[...], h_ref, TM_EMBED)


def _embed(xp, xs, emb, norm_g, mod):
    tm = TM_EMBED
    nct = TP // tm
    per_seq = DEC_SEQ // tm
    return pl.pallas_call(
        functools.partial(_embed_kernel, n_ctx_tiles=nct),
        out_shape=(jax.ShapeDtypeStruct((T, D_MODEL), F32),
                   jax.ShapeDtypeStruct((T, D_MODEL), BF16)),
        grid=(T // tm,),
        in_specs=[
            pl.BlockSpec((tm, D_MODEL), lambda i: (jnp.minimum(i, nct - 1), 0)),
            pl.BlockSpec((tm, D_MODEL), lambda i: (jnp.maximum(i - nct, 0), 0)),
            pl.BlockSpec((tm, D_MODEL), lambda i: (jnp.maximum(i - nct, 0) % per_seq, 0)),
            _layer_spec((1, D_MODEL), 0),
            _mod_spec(0, 0, tm),
            _mod_spec(0, 1, tm),
        ],
        out_specs=(pl.BlockSpec((tm, D_MODEL), lambda i: (i, 0)),
                   pl.BlockSpec((tm, D_MODEL), lambda i: (i, 0))),
        compiler_params=_cparams(("parallel",)),
        name="embed",
    )(xp, xs, emb, norm_g, mod, mod)


def _mod_kernel(c_ref, w_ref, b_ref, o_ref):
    c = c_ref[...]
    s = _silu(c).astype(BF16)
    o_ref[...] = jnp.dot(s, w_ref[...].astype(BF16), preferred_element_type=F32) + b_ref[...]


def _mod_table(cond, w_mod, b_mod):
    tn = 1024
    n6 = 6 * D_MODEL
    return pl.pallas_call(
        _mod_kernel,
        out_shape=jax.ShapeDtypeStruct((DEPTH, COND_PAD, n6), F32),
        grid=(DEPTH, n6 // tn),
        in_specs=[
            pl.BlockSpec((COND_PAD, D_MODEL), lambda l, n: (0, 0)),
            pl.BlockSpec((None, D_MODEL, tn), lambda l, n: (l, 0, n)),
            pl.BlockSpec((None, 1, tn), lambda l, n: (l, 0, n)),
        ],
        out_specs=pl.BlockSpec((None, COND_PAD, tn), lambda l, n: (l, 0, n)),
        compiler_params=_cparams(("parallel", "parallel")),
        name="mod_table",
    )(cond, w_mod, b_mod.reshape(DEPTH, 1, n6))


def _proj_kernel(h_ref, w_ref, wlr_ref, o_ref, lr_ref):
    @pl.when(pl.program_id(1) == 0)
    def _():
        lr_ref[...] = jnp.dot(h_ref[...], wlr_ref[...], preferred_element_type=F32)

    o_ref[...] = jnp.dot(h_ref[...], w_ref[...], preferred_element_type=F32).astype(o_ref.dtype)


def _proj(h, layer, w_main, w_lr):
    tm, tn = TM_PROJ, TN_PROJ
    return pl.pallas_call(
        _proj_kernel,
        out_shape=(jax.ShapeDtypeStruct((T, N_MAIN), BF16),
                   jax.ShapeDtypeStruct((T, LR_PAD), F32)),
        grid=(T // tm, N_MAIN // tn),
        in_specs=[
            pl.BlockSpec((tm, D_MODEL), lambda m, n: (m, 0)),
            pl.BlockSpec((None, D_MODEL, tn), lambda m, n: (layer, 0, n)),
            _layer_spec((D_MODEL, LR_PAD), layer),
        ],
        out_specs=(pl.BlockSpec((tm, tn), lambda m, n: (m, n)),
                   pl.BlockSpec((tm, LR_PAD), lambda m, n: (m, 0))),
        compiler_params=_cparams(("parallel", "arbitrary")),
        name="proj",
    )(h, w_main, w_lr)


def _sgu_kernel(pu_ref, pv_ref, lng_ref, lnb_ref, ws_ref, bs_ref, a_ref, vn_ref):
    gv = jax.nn.gelu(pv_ref[...].astype(F32), approximate=True)
    mu = jnp.mean(gv, axis=-1, keepdims=True)
    d = gv - mu
    var = jnp.mean(d * d, axis=-1, keepdims=True)
    vn_ref[...] = (d * lax.rsqrt(var + EPS) * lng_ref[...] + lnb_ref[...]).astype(BF16)
    for c in range(TM_SGU // CHUNK_A):
        rows = slice(c * CHUNK_A, (c + 1) * CHUNK_A)
        for g in range(N_A_GROUPS):
            cols = slice(g * A_GROUP, (g + 1) * A_GROUP)
            f = jnp.dot(ws_ref[g], vn_ref[rows, cols], preferred_element_type=F32) + bs_ref[:, cols]
            gu = jax.nn.gelu(pu_ref[rows, cols].astype(F32), approximate=True)
            a_ref[rows, cols] = (gu * f).astype(BF16)


def _sgu(proj, layer, ln_g, ln_b, w_s, bs_full):
    tm = TM_SGU
    return pl.pallas_call(
        _sgu_kernel,
        out_shape=jax.ShapeDtypeStruct((T, D_A), BF16),
        grid=(T // tm,),
        in_specs=[
            pl.BlockSpec((tm, D_A), lambda i: (i, C_U // D_A)),
            pl.BlockSpec((tm, D_A), lambda i: (i, C_V // D_A)),
            _layer_spec((1, D_A), layer),
            _layer_spec((1, D_A), layer),
            _layer_spec((N_A_GROUPS, CHUNK_A, CHUNK_A), layer),
            _layer_spec((CHUNK_A, D_A), layer),
        ],
        out_specs=pl.BlockSpec((tm, D_A), lambda i: (i, 0)),
        scratch_shapes=[pltpu.VMEM((tm, D_A), BF16)],
        compiler_params=_cparams(("parallel",)),
        name="sgu",
    )(proj, proj, ln_g, ln_b, w_s, bs_full)


def _split3(x):
    hi = x.astype(BF16)
    r = x - hi.astype(F32)
    mid = r.astype(BF16)
    lo = (r - mid.astype(F32)).astype(BF16)
    return hi, mid, lo


def _tri_cumsum(tri, x):
    hi, mid, lo = _split3(x)
    return (jnp.dot(tri, hi, preferred_element_type=F32)
            + jnp.dot(tri, mid, preferred_element_type=F32)
            + jnp.dot(tri, lo, preferred_element_type=F32))


_NT = (((1,), (1,)), ((), ()))
_TN = (((0,), (0,)), ((), ()))


def _gla_decay(lr_ref, a2_ref, ab_ref, tb, backward):
    z = jnp.dot(lr_ref[...].astype(BF16), a2_ref[...], preferred_element_type=F32) + ab_ref[...]
    la = _log_sigmoid(z) / GLA_TAU
    r = lax.broadcasted_iota(I32, (tb, tb), 0)
    c = lax.broadcasted_iota(I32, (tb, tb), 1)
    same = (r // GLA_CHUNK) == (c // GLA_CHUNK)
    tri = (same & ((c >= r) if backward else (c <= r))).astype(BF16)
    return _tri_cumsum(tri, la)


def _gla_block(dirs, tb):
    nchunk = tb // GLA_CHUNK
    scale = GLA_DK_HEAD ** -0.5
    rr = lax.broadcasted_iota(I32, (GLA_CHUNK, GLA_CHUNK), 0)
    cc = lax.broadcasted_iota(I32, (GLA_CHUNK, GLA_CHUNK), 1)
    decays = [_gla_decay(d[0], d[1], d[2], tb, d[8]) for d in dirs]
    per_dir = []
    for (lr_ref, a2_ref, ab_ref, q_ref, k_ref, v_ref, o_ref, st_ref, backward), b in zip(dirs, decays):
        units = []
        per_dir.append(units)
        eb = jnp.exp(b)
        enb = jnp.exp(-b)
        mask = (cc >= rr) if backward else (cc <= rr)
        order = range(nchunk - 1, -1, -1) if backward else range(nchunk)
        for ci in order:
            rows = slice(ci * GLA_CHUNK, (ci + 1) * GLA_CHUNK)
            last = ci * GLA_CHUNK if backward else (ci + 1) * GLA_CHUNK - 1
            btot = b[last:last + 1, :]
            eke = jnp.exp(btot - b[rows, :])
            gdec = jnp.exp(btot)
            for h in range(N_GLA_HEADS):
                kc = slice(h * GLA_DK_HEAD, (h + 1) * GLA_DK_HEAD)
                vc = slice(h * GLA_DV_HEAD, (h + 1) * GLA_DV_HEAD)
                q = q_ref[rows, kc].astype(F32) * scale
                k = k_ref[rows, kc].astype(F32)
                qd = (q * eb[rows, kc]).astype(BF16)
                kd = (k * enb[rows, kc]).astype(BF16)
                ke = (k * eke[:, kc]).astype(BF16)
                vv = v_ref[rows, vc]
                att = lax.dot_general(qd, kd, _NT, preferred_element_type=F32)
                att = jnp.where(mask, att, 0.0).astype(BF16)
                dst = lax.dot_general(vv, ke, _TN, preferred_element_type=F32)
                units.append(dict(o_ref=o_ref, st_ref=st_ref, rows=rows, vc=vc, h=h, qd=qd, vv=vv,
                                  att=att, dst=dst, gdec=gdec[:, kc]))
    units = [u for group in zip(*per_dir) for u in group]
    for u in units:
        u["o_intra"] = jnp.dot(u["att"], u["vv"], preferred_element_type=F32)
    states = {}
    for u in units:
        key = (id(u["st_ref"]), u["h"])
        st = states[key] if key in states else u["st_ref"][u["h"]]
        u["o_ref"][u["rows"], u["vc"]] = u["o_intra"] + lax.dot_general(
            u["qd"], st.astype(BF16), _NT, preferred_element_type=F32)
        states[key] = st * u["gdec"] + u["dst"]
    for (_, _, _, _, _, _, _, st_ref, _) in dirs:
        for h in range(N_GLA_HEADS):
            st_ref[h] = states[(id(st_ref), h)]


def _gla_kernel(fwd_ref, bwd_ref, flag_ref, seq_ref,
                lrf_ref, lrb_ref, a2f_ref, a2b_ref, abf_ref, abb_ref,
                qf_ref, kf_ref, vf_ref, qb_ref, kb_ref, vb_ref, s0_ref, sprev_ref,
                of_ref, ob_ref, sfin_ref, stf_ref, stb_ref):
    del sprev_ref
    i = pl.program_id(0)
    first = (flag_ref[i] & 1) == 1
    last = (flag_ref[i] & 2) == 2
    is_ctx = seq_ref[i] < BATCH

    @pl.when(first & is_ctx)
    def _():
        stf_ref[...] = jnp.zeros_like(stf_ref)
        stb_ref[...] = jnp.zeros_like(stb_ref)

    @pl.when(first & jnp.logical_not(is_ctx))
    def _():
        for h in range(N_GLA_HEADS):
            stf_ref[h] = s0_ref[0, h].T
            stb_ref[h] = s0_ref[1, h].T

    _gla_block([(lrf_ref, a2f_ref, abf_ref, qf_ref, kf_ref, vf_ref, of_ref, stf_ref, False),
                (lrb_ref, a2b_ref, abb_ref, qb_ref, kb_ref, vb_ref, ob_ref, stb_ref, True)], TB_GLA)

    @pl.when(last & is_ctx)
    def _():
        for h in range(N_GLA_HEADS):
            sfin_ref[0, h] = stf_ref[h].T
            sfin_ref[1, h] = stb_ref[h].T


def _gla_tables():
    tb = TB_GLA
    fwd, bwd, flag, seq = [], [], [], []
    for s, (row0, seqlen) in enumerate([(b * SEQ, SEQ) for b in range(BATCH)]
                                       + [(TP + b * DEC_SEQ, DEC_SEQ) for b in range(DEC_BATCH)]):
        nb = seqlen // tb
        for j in range(nb):
            fwd.append(row0 // tb + j)
            bwd.append(row0 // tb + nb - 1 - j)
            flag.append((1 if j == 0 else 0) | (2 if j == nb - 1 else 0))
            seq.append(s)
    return tuple(np.asarray(t, np.int32) for t in (fwd, bwd, flag, seq))


def _gla(proj, lr, a2f, a2b, ab, s0, s_all, layer):
    tb = TB_GLA
    fwd_t, bwd_t, flag_t, seq_t = _gla_tables()
    nsteps = fwd_t.shape[0]
    n_prefetch = 4

    def fwd(i, fw, bw, fl, sq):
        return fw[i]

    def bwd(i, fw, bw, fl, sq):
        return bw[i]

    def col_spec(rowfn, width, col):
        return pl.BlockSpec((tb, width), lambda i, *t: (rowfn(i, *t), col // width))

    state_block = (2, N_GLA_HEADS, GLA_DK_HEAD, GLA_DV_HEAD)
    in_specs = [
        pl.BlockSpec((tb, LR_PAD), lambda i, *t: (fwd(i, *t), 0)),
        pl.BlockSpec((tb, LR_PAD), lambda i, *t: (bwd(i, *t), 0)),
        _layer_spec((LR_PAD, GLA_DK), layer),
        _layer_spec((LR_PAD, GLA_DK), layer),
        pl.BlockSpec((None, None, 1, GLA_DK), lambda i, *t: (layer, 0, 0, 0)),
        pl.BlockSpec((None, None, 1, GLA_DK), lambda i, *t: (layer, 1, 0, 0)),
        col_spec(fwd, GLA_DK, C_Q), col_spec(fwd, GLA_DK, C_K), col_spec(fwd, GLA_DV, C_VG),
        col_spec(bwd, GLA_DK, C_Q), col_spec(bwd, GLA_DK, C_K), col_spec(bwd, GLA_DV, C_VG),
        pl.BlockSpec((None, None) + state_block,
                     lambda i, fw, bw, fl, sq: (jnp.maximum(sq[i] - BATCH, 0), layer, 0, 0, 0, 0)),
        pl.BlockSpec(memory_space=pl.ANY),
    ]
    out_shape = (jax.ShapeDtypeStruct((T, GLA_DV), F32),
                 jax.ShapeDtypeStruct((T, GLA_DV), F32),
                 jax.ShapeDtypeStruct((BATCH, DEPTH) + state_block, F32))
    out_specs = (pl.BlockSpec((tb, GLA_DV), lambda i, *t: (fwd(i, *t), 0)),
                 pl.BlockSpec((tb, GLA_DV), lambda i, *t: (bwd(i, *t), 0)),
                 pl.BlockSpec((None, None) + state_block,
                              lambda i, fw, bw, fl, sq: (jnp.minimum(sq[i], BATCH - 1), layer, 0, 0, 0, 0)))
    return pl.pallas_call(
        _gla_kernel,
        out_shape=out_shape,
        grid_spec=pltpu.PrefetchScalarGridSpec(
            num_scalar_prefetch=n_prefetch,
            grid=(nsteps,),
            in_specs=in_specs,
            out_specs=out_specs,
            scratch_shapes=[pltpu.VMEM((N_GLA_HEADS, GLA_DV_HEAD, GLA_DK_HEAD), F32),
                            pltpu.VMEM((N_GLA_HEADS, GLA_DV_HEAD, GLA_DK_HEAD), F32)],
        ),
        input_output_aliases={n_prefetch + len(in_specs) - 1: 2},
        compiler_params=_cparams(("arbitrary",)),
        name="gla",
    )(fwd_t, bwd_t, flag_t, seq_t, lr, lr, a2f, a2b, ab, ab,
      proj, proj, proj, proj, proj, proj, s0, s_all)


def _mixout_kernel(a_ref, of_ref, ob_ref, pr_ref, gn_ref, ga_ref, gb_ref, x_ref, g1_ref,
                   wa_ref, wb_ref, wo_ref, out_ref, o_ref):
    for h in range(N_GLA_HEADS):
        vc = slice(h * GLA_DV_HEAD, (h + 1) * GLA_DV_HEAD)
        o = of_ref[:, vc] + ob_ref[:, vc]
        o = o * lax.rsqrt(jnp.mean(o * o, axis=-1, keepdims=True) + EPS)
        o_ref[:, vc] = (o * gn_ref[:, vc] * _silu(pr_ref[:, vc].astype(F32))).astype(BF16)
    ya = jnp.dot(a_ref[...], wa_ref[...], preferred_element_type=F32)
    yb = jnp.dot(o_ref[...], wb_ref[...], preferred_element_type=F32)
    merged = (jax.nn.sigmoid(ga_ref[...].astype(F32)) * ya
              + jax.nn.sigmoid(gb_ref[...].astype(F32)) * yb).astype(BF16)
    y = jnp.dot(merged, wo_ref[...], preferred_element_type=F32)
    out_ref[...] = x_ref[...] + g1_ref[...] * y


def _resident_spec(shape, layer):
    zeros = (0,) * len(shape)
    return pl.BlockSpec((None,) + tuple(shape), lambda *_: (layer,) + zeros,
                        pipeline_mode=pl.Buffered(1))


def _mixout(a, o_f, o_b, proj, gla_g, x, mod, layer, w_a, w_b, w_o):
    tm = TM_MIX
    return pl.pallas_call(
        _mixout_kernel,
        out_shape=jax.ShapeDtypeStruct((T, D_MODEL), F32),
        grid=(T // tm,),
        in_specs=[
            pl.BlockSpec((tm, D_A), lambda m: (m, 0)),
            pl.BlockSpec((tm, GLA_DV), lambda m: (m, 0)),
            pl.BlockSpec((tm, GLA_DV), lambda m: (m, 0)),
            pl.BlockSpec((tm, GLA_DV), lambda m: (m, C_R // GLA_DV)),
            _layer_spec((1, GLA_DV), layer),
            pl.BlockSpec((tm, D_MODEL), lambda m: (m, C_GA // D_MODEL)),
            pl.BlockSpec((tm, D_MODEL), lambda m: (m, C_GB // D_MODEL)),
            pl.BlockSpec((tm, D_MODEL), lambda m: (m, 0)),
            _mod_spec(layer, 2, tm),
            _resident_spec((D_A, D_MODEL), layer),
            _resident_spec((GLA_DV, D_MODEL), layer),
            _resident_spec((D_MODEL, D_MODEL), layer),
        ],
        out_specs=pl.BlockSpec((tm, D_MODEL), lambda m: (m, 0)),
        scratch_shapes=[pltpu.VMEM((tm, GLA_DV), BF16)],
        compiler_params=_cparams(("parallel",)),
        name="mixout",
    )(a, o_f, o_b, proj, gla_g, proj, proj, x, mod, w_a, w_b, w_o)


def _ffn_kernel(x_ref, g_ref, sh_ref, sc_ref, g2_ref, gn_ref, shn_ref, scn_ref, w1_ref, w3_ref, w2_ref,
                o_ref, hn_ref, h_ref, *, nf):
    f = pl.program_id(1)

    @pl.when(f == 0)
    def _():
        _fill_modnorm(x_ref, g_ref[...], sc_ref[...], sh_ref[...], h_ref, TM_FFN)
        o_ref[...] = jnp.zeros_like(o_ref)

    _swiglu_accumulate(h_ref, w1_ref, w3_ref, w2_ref, o_ref, TM_FFN)

    @pl.when(f == nf - 1)
    def _():
        g2 = g2_ref[...]
        gain_n, shn = gn_ref[...] * (1.0 + scn_ref[...]), shn_ref[...]

        def body(i, carry):
            r = pl.multiple_of(i * SLAB, SLAB)
            xn = x_ref[pl.ds(r, SLAB), :] + g2 * o_ref[pl.ds(r, SLAB), :]
            o_ref[pl.ds(r, SLAB), :] = xn
            hn_ref[pl.ds(r, SLAB), :] = _modnorm(xn, gain_n, shn).astype(BF16)
            return carry
        lax.fori_loop(0, TM_FFN // SLAB, body, 0, unroll=SLAB_UNROLL)


def _ffn(x, norm2_g, norm1_g, mod, layer, j, w1, w3, w2):
    tm, tf = TM_FFN, TF_FFN
    nf = D_FF // tf
    return pl.pallas_call(
        functools.partial(_ffn_kernel, nf=nf),
        out_shape=(jax.ShapeDtypeStruct((T, D_MODEL), F32),
                   jax.ShapeDtypeStruct((T, D_MODEL), BF16)),
        grid=(T // tm, nf),
        in_specs=[
            pl.BlockSpec((tm, D_MODEL), lambda m, f: (m, 0)),
            _layer_spec((1, D_MODEL), layer),
            _mod_spec(layer, 3, tm),
            _mod_spec(layer, 4, tm),
            _mod_spec(layer, 5, tm),
            _layer_spec((1, D_MODEL), layer + 1),
            _mod_spec(layer + 1, 0, tm),
            _mod_spec(layer + 1, 1, tm),
            pl.BlockSpec((None, D_MODEL, tf), lambda m, f: (j, 0, f)),
            pl.BlockSpec((None, D_MODEL, tf), lambda m, f: (j, 0, f)),
            pl.BlockSpec((None, tf, D_MODEL), lambda m, f: (j, f, 0)),
        ],
        out_specs=(pl.BlockSpec((tm, D_MODEL), lambda m, f: (m, 0)),
                   pl.BlockSpec((tm, D_MODEL), lambda m, f: (m, 0))),
        scratch_shapes=[pltpu.VMEM((tm, D_MODEL), BF16)],
        compiler_params=_cparams(("parallel", "arbitrary")),
        name="ffn",
    )(x, norm2_g, mod, mod, mod, norm1_g, mod, mod, w1, w3, w2)


def _router_kernel(x_ref, g_ref, sh_ref, sc_ref, rwt_ref, rb_ref, h_ref, info_ref, p_ref, cnt_ref,
                   carry_ref):
    i = pl.program_id(0)
    tm = TM_ROUTE

    @pl.when(i == 0)
    def _():
        carry_ref[...] = jnp.zeros_like(carry_ref)

    _fill_modnorm(x_ref, g_ref[...], sc_ref[...], sh_ref[...], h_ref, tm)
    logits = lax.dot_general(rwt_ref[...], h_ref[...], _NT, preferred_element_type=F32,
                             precision=lax.Precision.HIGHEST) + rb_ref[:, 0:1]
    eidx = lax.broadcasted_iota(I32, (N_EXPERTS, tm), 0).astype(F32)
    none = float(N_EXPERTS)
    m1 = jnp.max(logits, axis=0, keepdims=True)
    i1 = jnp.min(jnp.where(logits == m1, eidx, none), axis=0, keepdims=True)
    rest = jnp.where(eidx == i1, -jnp.inf, logits)
    m2 = jnp.max(rest, axis=0, keepdims=True)
    i2 = jnp.min(jnp.where(rest == m2, eidx, none), axis=0, keepdims=True)
    e = jnp.exp(m2 - m1)
    den = 1.0 + e
    sel1 = eidx == i1
    sel2 = eidx == i2
    onehot = (sel1 | sel2).astype(BF16)
    r = lax.broadcasted_iota(I32, (tm, tm), 0)
    c = lax.broadcasted_iota(I32, (tm, tm), 1)
    upper = (r <= c).astype(BF16)
    cnt = jnp.dot(onehot, upper, preferred_element_type=F32) + carry_ref[:, 0:1]
    r1 = jnp.sum(jnp.where(sel1, cnt, 0.0), axis=0, keepdims=True) - 1.0
    r2 = jnp.sum(jnp.where(sel2, cnt, 0.0), axis=0, keepdims=True) - 1.0
    info_ref[...] = jnp.zeros_like(info_ref)
    info_ref[0:1, :] = i1.astype(I32)
    info_ref[1:2, :] = i2.astype(I32)
    info_ref[2:3, :] = r1.astype(I32)
    info_ref[3:4, :] = r2.astype(I32)
    p_ref[...] = jnp.zeros_like(p_ref)
    p_ref[0:1, :] = 1.0 / den
    p_ref[1:2, :] = e / den
    total = cnt[:, tm - 1:tm]
    carry_ref[...] = jnp.broadcast_to(total, carry_ref.shape)
    cnt_ref[...] = jnp.broadcast_to(total, cnt_ref.shape).astype(I32)


def _router(x, norm_g, mod, layer, j, rw_t, rb):
    tm = TM_ROUTE
    nt = T // tm
    return pl.pallas_call(
        _router_kernel,
        out_shape=(jax.ShapeDtypeStruct((T, D_MODEL), F32),
                   jax.ShapeDtypeStruct((nt, 8, tm), I32),
                   jax.ShapeDtypeStruct((nt, 8, tm), F32),
                   jax.ShapeDtypeStruct((N_EXPERTS, 128), I32)),
        grid=(nt,),
        in_specs=[
            pl.BlockSpec((tm, D_MODEL), lambda i: (i, 0)),
            _layer_spec((1, D_MODEL), layer),
            _mod_spec(layer, 3, tm),
            _mod_spec(layer, 4, tm),
            _layer_spec((N_EXPERTS, D_MODEL), j),
            _layer_spec((N_EXPERTS, 128), j),
        ],
        out_specs=(pl.BlockSpec((tm, D_MODEL), lambda i: (i, 0)),
                   pl.BlockSpec((None, 8, tm), lambda i: (i, 0, 0)),
                   pl.BlockSpec((None, 8, tm), lambda i: (i, 0, 0)),
                   pl.BlockSpec((N_EXPERTS, 128), lambda i: (0, 0))),
        scratch_shapes=[pltpu.VMEM((N_EXPERTS, 128), F32)],
        compiler_params=_cparams(("arbitrary",)),
        name="router",
    )(x, norm_g, mod, mod, rw_t, rb)


def _dispatch_kernel(pos1_ref, pos2_ref, zstart_ref, zcount_ref, h_ref, o_hbm, zrow_ref, sem):
    i = pl.program_id(0)
    td = TD_DISP
    base = i * td

    def row_copy(src_row, dst_row):
        return pltpu.make_async_copy(h_ref.at[pl.ds(src_row, 1)], o_hbm.at[pl.ds(dst_row, 1)], sem)

    def zero_copy(dst_row):
        return pltpu.make_async_copy(zrow_ref, o_hbm.at[pl.ds(dst_row, 1)], sem)

    @pl.when(i == 0)
    def _():
        zrow_ref[...] = jnp.zeros_like(zrow_ref)
        for e in range(N_EXPERTS + 1):
            def zbody(r, carry):
                zero_copy(zstart_ref[e] + r).start()
                return carry
            lax.fori_loop(0, zcount_ref[e], zbody, 0)

            def zwait(r, carry):
                zero_copy(0).wait()
                return carry
            lax.fori_loop(0, zcount_ref[e], zwait, 0)

    def body(r, carry):
        row_copy(r, pos1_ref[base + r]).start()
        row_copy(r, pos2_ref[base + r]).start()
        return carry
    lax.fori_loop(0, td, body, 0, unroll=DMA_UNROLL)

    tile_copy = pltpu.make_async_copy(h_ref, o_hbm.at[pl.ds(0, td)], sem)
    tile_copy.wait()
    tile_copy.wait()


def _dispatch(pos1, pos2, zstart, zcount, h):
    td = TD_DISP
    return pl.pallas_call(
        _dispatch_kernel,
        out_shape=jax.ShapeDtypeStruct((ROWS_CAP, D_MODEL), F32),
        grid_spec=pltpu.PrefetchScalarGridSpec(
            num_scalar_prefetch=4,
            grid=(T // td,),
            in_specs=[pl.BlockSpec((td, D_MODEL), lambda i, *_: (i, 0))],
            out_specs=pl.BlockSpec(memory_space=pl.ANY),
            scratch_shapes=[pltpu.VMEM((1, D_MODEL), F32), pltpu.SemaphoreType.DMA],
        ),
        compiler_params=_cparams(("arbitrary",)),
        name="dispatch",
    )(pos1, pos2, zstart, zcount, h)


def _expert_kernel(te_ref, nact_ref, hs_ref, w1_ref, w3_ref, w2_ref, y_ref, hb_ref):
    i = pl.program_id(0)
    f = pl.program_id(1)
    active = i < nact_ref[0]

    @pl.when(f == 0)
    def _():
        y_ref[...] = jnp.zeros_like(y_ref)

    @pl.when(active & (f == 0))
    def _():
        def body(s, carry):
            r = pl.multiple_of(s * SLAB, SLAB)
            hb_ref[pl.ds(r, SLAB), :] = hs_ref[pl.ds(r, SLAB), :].astype(BF16)
            return carry
        lax.fori_loop(0, TR_EXP // SLAB, body, 0, unroll=SLAB_UNROLL)

    @pl.when(active)
    def _():
        _swiglu_accumulate(hb_ref, w1_ref, w3_ref, w2_ref, y_ref, TR_EXP)


def _experts(tile_expert, n_active, hs, j, w1, w3, w2):
    tr, tf = TR_EXP, TF_EXP
    nf = D_FF // tf

    def row_map(i, f, te, na):
        return (jnp.minimum(i, na[0] - 1), 0)

    def out_map(i, f, te, na):
        return (i, 0)

    def w13_map(i, f, te, na):
        return (j, te[i], 0, jnp.where(i < na[0], f, nf - 1))

    def w2_map(i, f, te, na):
        return (j, te[i], jnp.where(i < na[0], f, nf - 1), 0)

    return pl.pallas_call(
        _expert_kernel,
        out_shape=jax.ShapeDtypeStruct((ROWS_CAP, D_MODEL), F32),
        grid_spec=pltpu.PrefetchScalarGridSpec(
            num_scalar_prefetch=2,
            grid=(NT_EXP, nf),
            in_specs=[
                pl.BlockSpec((tr, D_MODEL), row_map),
                pl.BlockSpec((None, None, D_MODEL, tf), w13_map),
                pl.BlockSpec((None, None, D_MODEL, tf), w13_map),
                pl.BlockSpec((None, None, tf, D_MODEL), w2_map),
            ],
            out_specs=pl.BlockSpec((tr, D_MODEL), out_map),
            scratch_shapes=[pltpu.VMEM((tr, D_MODEL), BF16)],
        ),
        compiler_params=_cparams(("arbitrary", "arbitrary")),
        name="experts",
    )(tile_expert, n_active, hs, w1, w3, w2)


def _gather_residual(pos1_ref, pos2_ref, x_ref, p1_ref, p2_ref, g2_ref, y_hbm, buf_ref, sem):
    i = pl.program_id(0)
    n = pl.num_programs(0)
    tc = TC_COMB
    slot = i % 2

    def start_gathers(tile, s):
        base = tile * tc

        def body(r, carry):
            pltpu.make_async_copy(y_hbm.at[pl.ds(pos1_ref[base + r], 1)],
                                  buf_ref.at[s, 0, pl.ds(r, 1)], sem.at[s]).start()
            pltpu.make_async_copy(y_hbm.at[pl.ds(pos2_ref[base + r], 1)],
                                  buf_ref.at[s, 1, pl.ds(r, 1)], sem.at[s]).start()
            return carry
        lax.fori_loop(0, tc, body, 0, unroll=DMA_UNROLL)

    @pl.when(i == 0)
    def _():
        start_gathers(0, 0)

    @pl.when(i + 1 < n)
    def _():
        start_gathers(i + 1, 1 - slot)

    pltpu.make_async_copy(y_hbm.at[pl.ds(0, tc)], buf_ref.at[slot, 0], sem.at[slot]).wait()
    pltpu.make_async_copy(y_hbm.at[pl.ds(0, tc)], buf_ref.at[slot, 1], sem.at[slot]).wait()
    mix = p1_ref[...] * buf_ref[slot, 0] + p2_ref[...] * buf_ref[slot, 1]
    return x_ref[...] + g2_ref[...] * mix


def _combine_kernel(pos1_ref, pos2_ref, x_ref, p1_ref, p2_ref, g2_ref, gn_ref, shn_ref, scn_ref, y_hbm,
                    o_ref, hn_ref, buf_ref, sem):
    xn = _gather_residual(pos1_ref, pos2_ref, x_ref, p1_ref, p2_ref, g2_ref, y_hbm, buf_ref, sem)
    o_ref[...] = xn
    hn_ref[...] = _modnorm(xn, gn_ref[...] * (1.0 + scn_ref[...]), shn_ref[...]).astype(BF16)


def _combine_final_kernel(pos1_ref, pos2_ref, x_ref, p1_ref, p2_ref, g2_ref, gf_ref, y_hbm,
                          yp_ref, ys_ref, buf_ref, sem, *, n_ctx_tiles):
    xn = _gather_residual(pos1_ref, pos2_ref, x_ref, p1_ref, p2_ref, g2_ref, y_hbm, buf_ref, sem)
    y = xn * lax.rsqrt(jnp.mean(xn * xn, axis=-1, keepdims=True) + EPS) * gf_ref[...]
    i = pl.program_id(0)

    @pl.when(i < n_ctx_tiles)
    def _():
        yp_ref[...] = y

    @pl.when(i >= n_ctx_tiles)
    def _():
        ys_ref[...] = y


def _combine_call(body, extra_specs, out_shape, out_specs, name, args):
    tc = TC_COMB
    row_tile = pl.BlockSpec((tc, D_MODEL), lambda i, *_: (i, 0))
    col = pl.BlockSpec((tc, 1), lambda i, *_: (i, 0))
    return pl.pallas_call(
        body,
        out_shape=out_shape,
        grid_spec=pltpu.PrefetchScalarGridSpec(
            num_scalar_prefetch=2,
            grid=(T // tc,),
            in_specs=[row_tile, col, col] + extra_specs + [pl.BlockSpec(memory_space=pl.ANY)],
            out_specs=out_specs,
            scratch_shapes=[pltpu.VMEM((2, 2, tc, D_MODEL), F32), pltpu.SemaphoreType.DMA((2,))],
        ),
        compiler_params=_cparams(("arbitrary",)),
        name=name,
    )(*args)


def _combine(pos1, pos2, x, p1, p2, mod, layer, norm1_g, y):
    tc = TC_COMB
    row_tile = pl.BlockSpec((tc, D_MODEL), lambda i, *_: (i, 0))
    return _combine_call(
        _combine_kernel,
        [_mod_spec(layer, 5, tc), _layer_spec((1, D_MODEL), layer + 1),
         _mod_spec(layer + 1, 0, tc), _mod_spec(layer + 1, 1, tc)],
        (jax.ShapeDtypeStruct((T, D_MODEL), F32), jax.ShapeDtypeStruct((T, D_MODEL), BF16)),
        (row_tile, row_tile), "combine",
        (pos1, pos2, x, p1, p2, mod, norm1_g, mod, mod, y))


def _combine_final(pos1, pos2, x, p1, p2, mod, layer, final_g, y):
    tc = TC_COMB
    nct = TP // tc
    return _combine_call(
        functools.partial(_combine_final_kernel, n_ctx_tiles=nct),
        [_mod_spec(layer, 5, tc), pl.BlockSpec((1, D_MODEL), lambda i, *_: (0, 0))],
        (jax.ShapeDtypeStruct((TP, D_MODEL), F32), jax.ShapeDtypeStruct((TS, D_MODEL), F32)),
        (pl.BlockSpec((tc, D_MODEL), lambda i, *_: (jnp.minimum(i, nct - 1), 0)),
         pl.BlockSpec((tc, D_MODEL), lambda i, *_: (jnp.maximum(i - nct, 0), 0))),
        "combine_final",
        (pos1, pos2, x, p1, p2, mod, final_g, y))


def _moe(x, norm_g, mod, layer, j, rw_t, rb, w1, w3, w2, norm1_g, final_g):
    h, info, probs, counts = _router(x, norm_g, mod, layer, j, rw_t, rb)
    counts = counts[:, 0]
    cap = ((counts + TR_EXP - 1) // TR_EXP) * TR_EXP
    ends = jnp.cumsum(cap)
    starts = ends - cap
    e1 = info[:, 0, :].reshape(T)
    e2 = info[:, 1, :].reshape(T)
    pos1 = starts[e1] + info[:, 2, :].reshape(T)
    pos2 = starts[e2] + info[:, 3, :].reshape(T)
    tile_end = ends // TR_EXP
    n_active = tile_end[-1:]
    tiles = jnp.minimum(jnp.arange(NT_EXP, dtype=I32), n_active[0] - 1)
    tile_expert = jnp.sum((tiles[:, None] >= tile_end[None, :]).astype(I32), axis=1)
    zstart = jnp.concatenate([starts + counts, ends[-1:]])
    zcount = jnp.concatenate([cap - counts, ROWS_CAP - ends[-1:]])
    hs = _dispatch(pos1, pos2, zstart, zcount, h)
    y = _experts(tile_expert, n_active, hs, j, w1, w3, w2)
    p1 = probs[:, 0, :].reshape(T, 1)
    p2 = probs[:, 1, :].reshape(T, 1)
    if layer == DEPTH - 1:
        return _combine_final(pos1, pos2, x, p1, p2, mod, layer, final_g, y)
    return _combine(pos1, pos2, x, p1, p2, mod, layer, norm1_g, y)


def _grid_pos_embed():
    rows = DEC_SEQ // GRID_W
    r, col = jnp.meshgrid(jnp.arange(rows, dtype=F32), jnp.arange(GRID_W, dtype=F32), indexing='ij')
    r = r.reshape(-1)
    col = col.reshape(-1)
    quarter = D_MODEL // 4
    freq = jnp.exp(-math.log(10000.0) * jnp.arange(quarter, dtype=F32) / quarter)
    ar = r[:, None] * freq
    ac = col[:, None] * freq
    return jnp.concatenate([jnp.sin(ar), jnp.cos(ar), jnp.sin(ac), jnp.cos(ac)], axis=-1)


def kernel(x_prompt, x_sample, state_gla, c, c_ctx, norm1_g, norm2_g, w_mod, b_mod, w_in,
           sgu_ln_g, sgu_ln_b, w_spatial, b_spatial, gla_a2, gla_ab, gla_norm_g,
           w_branch_a, w_branch_b, w_out, ffn_w1, ffn_w3, ffn_w2,
           moe_router, moe_router_b, moe_w1, moe_w3, moe_w2, final_g):
    cond = jnp.zeros((COND_PAD, D_MODEL), F32).at[0].set(c_ctx).at[1:N_COND].set(c)
    mod = _mod_table(cond, w_mod, b_mod).reshape(DEPTH, COND_PAD, 6, 1, D_MODEL)

    lr_end = W_IN_LR + 2 * GLA_RANK
    w_main = jnp.concatenate([w_in[:, :, lr_end:], w_in[:, :, :W_IN_LR]], axis=2).astype(BF16)
    w_lr = jnp.pad(w_in[:, :, W_IN_LR:lr_end], ((0, 0), (0, 0), (0, LR_PAD - 2 * GLA_RANK))).astype(BF16)
    norm1 = norm1_g.reshape(DEPTH, 1, D_MODEL)
    norm2 = norm2_g.reshape(DEPTH, 1, D_MODEL)
    ln_g = sgu_ln_g.reshape(DEPTH, 1, D_A)
    ln_b = sgu_ln_b.reshape(DEPTH, 1, D_A)
    w_s = w_spatial.astype(BF16)
    bs_full = jnp.repeat(jnp.swapaxes(b_spatial, 1, 2), A_GROUP, axis=2)
    a2_pad = jnp.zeros((DEPTH, LR_PAD, GLA_DK), F32)
    a2f = a2_pad.at[:, :GLA_RANK].set(gla_a2[:, 0]).astype(BF16)
    a2b = a2_pad.at[:, GLA_RANK:2 * GLA_RANK].set(gla_a2[:, 1]).astype(BF16)
    ab = gla_ab.reshape(DEPTH, 2, 1, GLA_DK)
    gla_g = gla_norm_g.reshape(DEPTH, 1, GLA_DV)
    w_a = w_branch_a.astype(BF16)
    w_b = w_branch_b.astype(BF16)
    w_o = w_out.astype(BF16)
    f_w1, f_w3, f_w2 = ffn_w1.astype(BF16), ffn_w3.astype(BF16), ffn_w2.astype(BF16)
    m_w1, m_w3, m_w2 = moe_w1.astype(BF16), moe_w3.astype(BF16), moe_w2.astype(BF16)
    rw_t = jnp.swapaxes(moe_router, 1, 2)
    rb = jnp.broadcast_to(moe_router_b[:, :, None], moe_router_b.shape + (128,))

    x, h = _embed(x_prompt.reshape(TP, D_MODEL), x_sample.reshape(TS, D_MODEL), _grid_pos_embed(),
                  norm1, mod)
    new_state = jnp.zeros((BATCH, DEPTH, 2, N_GLA_HEADS, GLA_DK_HEAD, GLA_DV_HEAD), F32)
    for l in range(DEPTH):
        proj, lr = _proj(h, l, w_main, w_lr)
        a = _sgu(proj, l, ln_g, ln_b, w_s, bs_full)
        o_f, o_b, new_state = _gla(proj, lr, a2f, a2b, ab, state_gla, new_state, l)
        x = _mixout(a, o_f, o_b, proj, gla_g, x, mod, l, w_a, w_b, w_o)
        j = l // 2
        if l % 2 == 0:
            x, h = _ffn(x, norm2, norm1, mod, l, j, f_w1, f_w3, f_w2)
        elif l < DEPTH - 1:
            x, h = _moe(x, norm2, mod, l, j, rw_t, rb, m_w1, m_w3, m_w2, norm1, final_g[None])
        else:
            y_prompt, y_sample = _moe(x, norm2, mod, l, j, rw_t, rb, m_w1, m_w3, m_w2, norm1,
                                      final_g[None])

    return (y_prompt.reshape(BATCH, SEQ, D_MODEL), y_sample.reshape(DEC_BATCH, DEC_SEQ, D_MODEL),
            new_state)
```

```python
import functools
import math

import numpy as np
import jax
import jax.numpy as jnp
from jax import lax
from jax.experimental import pallas as pl
from jax.experimental.pallas import tpu as pltpu

F32 = jnp.float32
BF16 = jnp.bfloat16
I32 = jnp.int32

D_MODEL = 2048
BATCH = 32
SEQ = 256
DEPTH = 4
DEC_BATCH = 8
DEC_SEQ = 2048
GRID_W = 64
CHUNK_A = 128
D_A = 1024
N_A_GROUPS = 8
A_GROUP = D_A // N_A_GROUPS
N_GLA_HEADS = 4
GLA_DK = D_MODEL // 4
GLA_DV = D_MODEL // 2
GLA_DK_HEAD = GLA_DK // N_GLA_HEADS
GLA_DV_HEAD = GLA_DV // N_GLA_HEADS
GLA_RANK = 16
GLA_TAU = 16.0
GLA_CHUNK = 64
D_FF = 5632
N_EXPERTS = 8
TOP_K = 2
EPS = 1e-6

TP = BATCH * SEQ
TS = DEC_BATCH * DEC_SEQ
T = TP + TS
N_COND = 1 + DEC_BATCH
COND_PAD = 16

W_IN_LR = 2 * D_A + 2 * GLA_DK + 2 * GLA_DV
C_GA, C_GB, C_U, C_V, C_Q, C_K, C_VG, C_R = 0, 2048, 4096, 5120, 6144, 6656, 7168, 8192
N_MAIN = 9216
LR_PAD = 128

VMEM_LIMIT = 56 * 1024 * 1024

TM_PROJ, TN_PROJ = 1024, 2304
TM_EMBED = 512
TM_SGU = 512
TB_GLA = 256
TM_MIX = 256
TM_FFN, TF_FFN = 512, 512
TM_ROUTE = 512
TD_DISP = 256
TR_EXP, TF_EXP = 512, 512
TC_COMB = 256
SLAB = 32
SLAB_UNROLL = 4
RS_SWIGLU = 512
DMA_UNROLL = 8

NT_EXP = (TOP_K * T) // TR_EXP + N_EXPERTS
ROWS_CAP = NT_EXP * TR_EXP


def _cparams(sem):
    return pltpu.CompilerParams(dimension_semantics=sem, vmem_limit_bytes=VMEM_LIMIT)


def _mod_row(i, tm):
    return jnp.maximum((i * tm) // DEC_SEQ - (TP // DEC_SEQ - 1), 0)


def _mod_spec(layer, j, tm, width=D_MODEL):
    return pl.BlockSpec((None, None, None, 1, width),
                        lambda m, *_: (layer, _mod_row(m, tm), j, 0, 0))


def _layer_spec(shape, layer):
    zeros = (0,) * len(shape)
    return pl.BlockSpec((None,) + tuple(shape), lambda *_: (layer,) + zeros)


def _modnorm(x, gain, sh):
    return x * lax.rsqrt(jnp.mean(x * x, axis=-1, keepdims=True) + EPS) * gain + sh


def _fill_modnorm(x_ref, g, sc, sh, h_ref, rows):
    gain = g * (1.0 + sc)

    def body(i, carry):
        r = pl.multiple_of(i * SLAB, SLAB)
        h_ref[pl.ds(r, SLAB), :] = _modnorm(x_ref[pl.ds(r, SLAB), :], gain, sh).astype(h_ref.dtype)
        return carry
    lax.fori_loop(0, rows // SLAB, body, 0, unroll=SLAB_UNROLL)


def _silu(x):
    return x * jax.nn.sigmoid(x)


def _log_sigmoid(x):
    return jnp.minimum(x, 0.0) - jnp.log(1.0 + jnp.exp(-jnp.abs(x)))


def _swiglu_accumulate(h_ref, w1_ref, w3_ref, w2_ref, acc_ref, rows):
    for r in range(rows // RS_SWIGLU):
        sl = slice(r * RS_SWIGLU, (r + 1) * RS_SWIGLU)
        h = h_ref[sl, :]
        u = _silu(jnp.dot(h, w1_ref[...], preferred_element_type=F32)) * jnp.dot(
            h, w3_ref[...], preferred_element_type=F32)
        acc_ref[sl, :] += jnp.dot(u.astype(BF16), w2_ref[...], preferred_element_type=F32)


def _embed_kernel(xp_ref, xs_ref, emb_ref, g_ref, sh_ref, sc_ref, o_ref, h_ref, *, n_ctx_tiles):
    i = pl.program_id(0)

    @pl.when(i < n_ctx_tiles)
    def _():
        o_ref[...] = xp_ref[...]

    @pl.when(i >= n_ctx_tiles)
    def _():
        o_ref[...] = xs_ref[...] + emb_ref[...]

    _fill_modnorm(o_ref, g_ref[...], sc_ref[...], sh_ref[...], h_ref, TM_EMBED)


def _embed(xp, xs, emb, norm_g, mod):
    tm = TM_EMBED
    nct = TP // tm
    per_seq = DEC_SEQ // tm
    return pl.pallas_call(
        functools.partial(_embed_kernel, n_ctx_tiles=nct),
        out_shape=(jax.ShapeDtypeStruct((T, D_MODEL), F32),
                   jax.ShapeDtypeStruct((T, D_MODEL), BF16)),
        grid=(T // tm,),
        in_specs=[
            pl.BlockSpec((tm, D_MODEL), lambda i: (jnp.minimum(i, nct - 1), 0)),
            pl.BlockSpec((tm, D_MODEL), lambda i: (jnp.maximum(i - nct, 0), 0)),
            pl.BlockSpec((tm, D_MODEL), lambda i: (jnp.maximum(i - nct, 0) % per_seq, 0)),
            _layer_spec((1, D_MODEL), 0),
            _mod_spec(0, 0, tm),
            _mod_spec(0, 1, tm),
        ],
        out_specs=(pl.BlockSpec((tm, D_MODEL), lambda i: (i, 0)),
                   pl.BlockSpec((tm, D_MODEL), lambda i: (i, 0))),
        compiler_params=_cparams(("parallel",)),
        name="embed",
    )(xp, xs, emb, norm_g, mod, mod)


def _mod_kernel(c_ref, w_ref, b_ref, o_ref):
    c = c_ref[...]
    s = _silu(c).astype(BF16)
    o_ref[...] = jnp.dot(s, w_ref[...].astype(BF16), preferred_element_type=F32) + b_ref[...]


def _mod_table(cond, w_mod, b_mod):
    tn = 1024
    n6 = 6 * D_MODEL
    return pl.pallas_call(
        _mod_kernel,
        out_shape=jax.ShapeDtypeStruct((DEPTH, COND_PAD, n6), F32),
        grid=(DEPTH, n6 // tn),
        in_specs=[
            pl.BlockSpec((COND_PAD, D_MODEL), lambda l, n: (0, 0)),
            pl.BlockSpec((None, D_MODEL, tn), lambda l, n: (l, 0, n)),
            pl.BlockSpec((None, 1, tn), lambda l, n: (l, 0, n)),
        ],
        out_specs=pl.BlockSpec((None, COND_PAD, tn), lambda l, n: (l, 0, n)),
        compiler_params=_cparams(("parallel", "parallel")),
        name="mod_table",
    )(cond, w_mod, b_mod.reshape(DEPTH, 1, n6))


def _proj_kernel(h_ref, w_ref, wlr_ref, o_ref, lr_ref):
    @pl.when(pl.program_id(1) == 0)
    def _():
        lr_ref[...] = jnp.dot(h_ref[...], wlr_ref[...], preferred_element_type=F32)

    o_ref[...] = jnp.dot(h_ref[...], w_ref[...], preferred_element_type=F32).astype(o_ref.dtype)


def _proj(h, layer, w_main, w_lr):
    tm, tn = TM_PROJ, TN_PROJ
    return pl.pallas_call(
        _proj_kernel,
        out_shape=(jax.ShapeDtypeStruct((T, N_MAIN), BF16),
                   jax.ShapeDtypeStruct((T, LR_PAD), F32)),
        grid=(T // tm, N_MAIN // tn),
        in_specs=[
            pl.BlockSpec((tm, D_MODEL), lambda m, n: (m, 0)),
            pl.BlockSpec((None, D_MODEL, tn), lambda m, n: (layer, 0, n)),
            _layer_spec((D_MODEL, LR_PAD), layer),
        ],
        out_specs=(pl.BlockSpec((tm, tn), lambda m, n: (m, n)),
                   pl.BlockSpec((tm, LR_PAD), lambda m, n: (m, 0))),
        compiler_params=_cparams(("parallel", "arbitrary")),
        name="proj",
    )(h, w_main, w_lr)


def _sgu_kernel(pu_ref, pv_ref, lng_ref, lnb_ref, ws_ref, bs_ref, a_ref, vn_ref):
    gv = jax.nn.gelu(pv_ref[...].astype(F32), approximate=True)
    mu = jnp.mean(gv, axis=-1, keepdims=True)
    d = gv - mu
    var = jnp.mean(d * d, axis=-1, keepdims=True)
    vn_ref[...] = (d * lax.rsqrt(var + EPS) * lng_ref[...] + lnb_ref[...]).astype(BF16)
    for c in range(TM_SGU // CHUNK_A):
        rows = slice(c * CHUNK_A, (c + 1) * CHUNK_A)
        for g in range(N_A_GROUPS):
            cols = slice(g * A_GROUP, (g + 1) * A_GROUP)
            f = jnp.dot(ws_ref[g], vn_ref[rows, cols], preferred_element_type=F32) + bs_ref[:, cols]
            gu = jax.nn.gelu(pu_ref[rows, cols].astype(F32), approximate=True)
            a_ref[rows, cols] = (gu * f).astype(BF16)


def _sgu(proj, layer, ln_g, ln_b, w_s, bs_full):
    tm = TM_SGU
    return pl.pallas_call(
        _sgu_kernel,
        out_shape=jax.ShapeDtypeStruct((T, D_A), BF16),
        grid=(T // tm,),
        in_specs=[
            pl.BlockSpec((tm, D_A), lambda i: (i, C_U // D_A)),
            pl.BlockSpec((tm, D_A), lambda i: (i, C_V // D_A)),
            _layer_spec((1, D_A), layer),
            _layer_spec((1, D_A), layer),
            _layer_spec((N_A_GROUPS, CHUNK_A, CHUNK_A), layer),
            _layer_spec((CHUNK_A, D_A), layer),
        ],
        out_specs=pl.BlockSpec((tm, D_A), lambda i: (i, 0)),
        scratch_shapes=[pltpu.VMEM((tm, D_A), BF16)],
        compiler_params=_cparams(("parallel",)),
        name="sgu",
    )(proj, proj, ln_g, ln_b, w_s, bs_full)


def _split3(x):
    hi = x.astype(BF16)
    r = x - hi.astype(F32)
    mid = r.astype(BF16)
    lo = (r - mid.astype(F32)).astype(BF16)
    return hi, mid, lo


def _tri_cumsum(tri, x):
    hi, mid, lo = _split3(x)
    return (jnp.dot(tri, hi, preferred_element_type=F32)
            + jnp.dot(tri, mid, preferred_element_type=F32)
            + jnp.dot(tri, lo, preferred_element_type=F32))


_NT = (((1,), (1,)), ((), ()))
_TN = (((0,), (0,)), ((), ()))


def _gla_decay(lr_ref, a2_ref, ab_ref, tb, backward):
    z = jnp.dot(lr_ref[...].astype(BF16), a2_ref[...], preferred_element_type=F32) + ab_ref[...]
    la = _log_sigmoid(z) / GLA_TAU
    r = lax.broadcasted_iota(I32, (tb, tb), 0)
    c = lax.broadcasted_iota(I32, (tb, tb), 1)
    same = (r // GLA_CHUNK) == (c // GLA_CHUNK)
    tri = (same & ((c >= r) if backward else (c <= r))).astype(BF16)
    return _tri_cumsum(tri, la)


def _gla_block(dirs, tb):
    nchunk = tb // GLA_CHUNK
    scale = GLA_DK_HEAD ** -0.5
    rr = lax.broadcasted_iota(I32, (GLA_CHUNK, GLA_CHUNK), 0)
    cc = lax.broadcasted_iota(I32, (GLA_CHUNK, GLA_CHUNK), 1)
    decays = [_gla_decay(d[0], d[1], d[2], tb, d[8]) for d in dirs]
    per_dir = []
    for (lr_ref, a2_ref, ab_ref, q_ref, k_ref, v_ref, o_ref, st_ref, backward), b in zip(dirs, decays):
        units = []
        per_dir.append(units)
        eb = jnp.exp(b)
        enb = jnp.exp(-b)
        mask = (cc >= rr) if backward else (cc <= rr)
        order = range(nchunk - 1, -1, -1) if backward else range(nchunk)
        for ci in order:
            rows = slice(ci * GLA_CHUNK, (ci + 1) * GLA_CHUNK)
            last = ci * GLA_CHUNK if backward else (ci + 1) * GLA_CHUNK - 1
            btot = b[last:last + 1, :]
            eke = jnp.exp(btot - b[rows, :])
            gdec = jnp.exp(btot)
            for h in range(N_GLA_HEADS):
                kc = slice(h * GLA_DK_HEAD, (h + 1) * GLA_DK_HEAD)
                vc = slice(h * GLA_DV_HEAD, (h + 1) * GLA_DV_HEAD)
                q = q_ref[rows, kc].astype(F32) * scale
                k = k_ref[rows, kc].astype(F32)
                qd = (q * eb[rows, kc]).astype(BF16)
                kd = (k * enb[rows, kc]).astype(BF16)
                ke = (k * eke[:, kc]).astype(BF16)
                vv = v_ref[rows, vc]
                att = lax.dot_general(qd, kd, _NT, preferred_element_type=F32)
                att = jnp.where(mask, att, 0.0).astype(BF16)
                dst = lax.dot_general(vv, ke, _TN, preferred_element_type=F32)
                units.append(dict(o_ref=o_ref, st_ref=st_ref, rows=rows, vc=vc, h=h, qd=qd, vv=vv,
                                  att=att, dst=dst, gdec=gdec[:, kc]))
    units = [u for group in zip(*per_dir) for u in group]
    for u in units:
        u["o_intra"] = jnp.dot(u["att"], u["vv"], preferred_element_type=F32)
    states = {}
    for u in units:
        key = (id(u["st_ref"]), u["h"])
        st = states[key] if key in states else u["st_ref"][u["h"]]
        u["o_ref"][u["rows"], u["vc"]] = u["o_intra"] + lax.dot_general(
            u["qd"], st.astype(BF16), _NT, preferred_element_type=F32)
        states[key] = st * u["gdec"] + u["dst"]
    for (_, _, _, _, _, _, _, st_ref, _) in dirs:
        for h in range(N_GLA_HEADS):
            st_ref[h] = states[(id(st_ref), h)]


def _gla_kernel(fwd_ref, bwd_ref, flag_ref, seq_ref,
                lrf_ref, lrb_ref, a2f_ref, a2b_ref, abf_ref, abb_ref,
                qf_ref, kf_ref, vf_ref, qb_ref, kb_ref, vb_ref, s0_ref, sprev_ref,
                of_ref, ob_ref, sfin_ref, stf_ref, stb_ref):
    del sprev_ref
    i = pl.program_id(0)
    first = (flag_ref[i] & 1) == 1
    last = (flag_ref[i] & 2) == 2
    is_ctx = seq_ref[i] < BATCH

    @pl.when(first & is_ctx)
    def _():
        stf_ref[...] = jnp.zeros_like(stf_ref)
        stb_ref[...] = jnp.zeros_like(stb_ref)

    @pl.when(first & jnp.logical_not(is_ctx))
    def _():
        for h in range(N_GLA_HEADS):
            stf_ref[h] = s0_ref[0, h].T
            stb_ref[h] = s0_ref[1, h].T

    _gla_block([(lrf_ref, a2f_ref, abf_ref, qf_ref, kf_ref, vf_ref, of_ref, stf_ref, False),
                (lrb_ref, a2b_ref, abb_ref, qb_ref, kb_ref, vb_ref, ob_ref, stb_ref, True)], TB_GLA)

    @pl.when(last & is_ctx)
    def _():
        for h in range(N_GLA_HEADS):
            sfin_ref[0, h] = stf_ref[h].T
            sfin_ref[1, h] = stb_ref[h].T


def _gla_tables():
    tb = TB_GLA
    fwd, bwd, flag, seq = [], [], [], []
    for s, (row0, seqlen) in enumerate([(b * SEQ, SEQ) for b in range(BATCH)]
                                       + [(TP + b * DEC_SEQ, DEC_SEQ) for b in range(DEC_BATCH)]):
        nb = seqlen // tb
        for j in range(nb):
            fwd.append(row0 // tb + j)
            bwd.append(row0 // tb + nb - 1 - j)
            flag.append((1 if j == 0 else 0) | (2 if j == nb - 1 else 0))
            seq.append(s)
    return tuple(np.asarray(t, np.int32) for t in (fwd, bwd, flag, seq))


def _gla(proj, lr, a2f, a2b, ab, s0, s_all, layer):
    tb = TB_GLA
    fwd_t, bwd_t, flag_t, seq_t = _gla_tables()
    nsteps = fwd_t.shape[0]
    n_prefetch = 4

    def fwd(i, fw, bw, fl, sq):
        return fw[i]

    def bwd(i, fw, bw, fl, sq):
        return bw[i]

    def col_spec(rowfn, width, col):
        return pl.BlockSpec((tb, width), lambda i, *t: (rowfn(i, *t), col // width))

    state_block = (2, N_GLA_HEADS, GLA_DK_HEAD, GLA_DV_HEAD)
    in_specs = [
        pl.BlockSpec((tb, LR_PAD), lambda i, *t: (fwd(i, *t), 0)),
        pl.BlockSpec((tb, LR_PAD), lambda i, *t: (bwd(i, *t), 0)),
        _layer_spec((LR_PAD, GLA_DK), layer),
        _layer_spec((LR_PAD, GLA_DK), layer),
        pl.BlockSpec((None, None, 1, GLA_DK), lambda i, *t: (layer, 0, 0, 0)),
        pl.BlockSpec((None, None, 1, GLA_DK), lambda i, *t: (layer, 1, 0, 0)),
        col_spec(fwd, GLA_DK, C_Q), col_spec(fwd, GLA_DK, C_K), col_spec(fwd, GLA_DV, C_VG),
        col_spec(bwd, GLA_DK, C_Q), col_spec(bwd, GLA_DK, C_K), col_spec(bwd, GLA_DV, C_VG),
        pl.BlockSpec((None, None) + state_block,
                     lambda i, fw, bw, fl, sq: (jnp.maximum(sq[i] - BATCH, 0), layer, 0, 0, 0, 0)),
        pl.BlockSpec(memory_space=pl.ANY),
    ]
    out_shape = (jax.ShapeDtypeStruct((T, GLA_DV), F32),
                 jax.ShapeDtypeStruct((T, GLA_DV), F32),
                 jax.ShapeDtypeStruct((BATCH, DEPTH) + state_block, F32))
    out_specs = (pl.BlockSpec((tb, GLA_DV), lambda i, *t: (fwd(i, *t), 0)),
                 pl.BlockSpec((tb, GLA_DV), lambda i, *t: (bwd(i, *t), 0)),
                 pl.BlockSpec((None, None) + state_block,
                              lambda i, fw, bw, fl, sq: (jnp.minimum(sq[i], BATCH - 1), layer, 0, 0, 0, 0)))
    return pl.pallas_call(
        _gla_kernel,
        out_shape=out_shape,
        grid_spec=pltpu.PrefetchScalarGridSpec(
            num_scalar_prefetch=n_prefetch,
            grid=(nsteps,),
            in_specs=in_specs,
            out_specs=out_specs,
            scratch_shapes=[pltpu.VMEM((N_GLA_HEADS, GLA_DV_HEAD, GLA_DK_HEAD), F32),
                            pltpu.VMEM((N_GLA_HEADS, GLA_DV_HEAD, GLA_DK_HEAD), F32)],
        ),
        input_output_aliases={n_prefetch + len(in_specs) - 1: 2},
        compiler_params=_cparams(("arbitrary",)),
        name="gla",
    )(fwd_t, bwd_t, flag_t, seq_t, lr, lr, a2f, a2b, ab, ab,
      proj, proj, proj, proj, proj, proj, s0, s_all)


def _mixout_kernel(a_ref, of_ref, ob_ref, pr_ref, gn_ref, ga_ref, gb_ref, x_ref, g1_ref,
                   wa_ref, wb_ref, wo_ref, out_ref, o_ref):
    for h in range(N_GLA_HEADS):
        vc = slice(h * GLA_DV_HEAD, (h + 1) * GLA_DV_HEAD)
        o = of_ref[:, vc] + ob_ref[:, vc]
        o = o * lax.rsqrt(jnp.mean(o * o, axis=-1, keepdims=True) + EPS)
        o_ref[:, vc] = (o * gn_ref[:, vc] * _silu(pr_ref[:, vc].astype(F32))).astype(BF16)
    ya = jnp.dot(a_ref[...], wa_ref[...], preferred_element_type=F32)
    yb = jnp.dot(o_ref[...], wb_ref[...], preferred_element_type=F32)
    merged = (jax.nn.sigmoid(ga_ref[...].astype(F32)) * ya
              + jax.nn.sigmoid(gb_ref[...].astype(F32)) * yb).astype(BF16)
    y = jnp.dot(merged, wo_ref[...], preferred_element_type=F32)
    out_ref[...] = x_ref[...] + g1_ref[...] * y


def _resident_spec(shape, layer):
    zeros = (0,) * len(shape)
    return pl.BlockSpec((None,) + tuple(shape), lambda *_: (layer,) + zeros,
                        pipeline_mode=pl.Buffered(1))


def _mixout(a, o_f, o_b, proj, gla_g, x, mod, layer, w_a, w_b, w_o):
    tm = TM_MIX
    return pl.pallas_call(
        _mixout_kernel,
        out_shape=jax.ShapeDtypeStruct((T, D_MODEL), F32),
        grid=(T // tm,),
        in_specs=[
            pl.BlockSpec((tm, D_A), lambda m: (m, 0)),
            pl.BlockSpec((tm, GLA_DV), lambda m: (m, 0)),
            pl.BlockSpec((tm, GLA_DV), lambda m: (m, 0)),
            pl.BlockSpec((tm, GLA_DV), lambda m: (m, C_R // GLA_DV)),
            _layer_spec((1, GLA_DV), layer),
            pl.BlockSpec((tm, D_MODEL), lambda m: (m, C_GA // D_MODEL)),
            pl.BlockSpec((tm, D_MODEL), lambda m: (m, C_GB // D_MODEL)),
            pl.BlockSpec((tm, D_MODEL), lambda m: (m, 0)),
            _mod_spec(layer, 2, tm),
            _resident_spec((D_A, D_MODEL), layer),
            _resident_spec((GLA_DV, D_MODEL), layer),
            _resident_spec((D_MODEL, D_MODEL), layer),
        ],
        out_specs=pl.BlockSpec((tm, D_MODEL), lambda m: (m, 0)),
        scratch_shapes=[pltpu.VMEM((tm, GLA_DV), BF16)],
        compiler_params=_cparams(("parallel",)),
        name="mixout",
    )(a, o_f, o_b, proj, gla_g, proj, proj, x, mod, w_a, w_b, w_o)


def _ffn_kernel(x_ref, g_ref, sh_ref, sc_ref, g2_ref, gn_ref, shn_ref, scn_ref,
                w1a_ref, w3a_ref, w2a_ref, w1b_ref, w3b_ref, w2b_ref, o_ref, hn_ref, h_ref, *, nf):
    s = pl.program_id(1)

    @pl.when(s == 0)
    def _():
        _fill_modnorm(x_ref, g_ref[...], sc_ref[...], sh_ref[...], h_ref, TM_FFN)
        o_ref[...] = jnp.zeros_like(o_ref)

    _swiglu_accumulate(h_ref, w1a_ref, w3a_ref, w2a_ref, o_ref, TM_FFN)

    @pl.when(2 * s + 1 < nf)
    def _():
        _swiglu_accumulate(h_ref, w1b_ref, w3b_ref, w2b_ref, o_ref, TM_FFN)

    @pl.when(s == pl.num_programs(1) - 1)
    def _():
        g2 = g2_ref[...]
        gain_n, shn = gn_ref[...] * (1.0 + scn_ref[...]), shn_ref[...]

        def body(i, carry):
            r = pl.multiple_of(i * SLAB, SLAB)
            xn = x_ref[pl.ds(r, SLAB), :] + g2 * o_ref[pl.ds(r, SLAB), :]
            o_ref[pl.ds(r, SLAB), :] = xn
            hn_ref[pl.ds(r, SLAB), :] = _modnorm(xn, gain_n, shn).astype(BF16)
            return carry
        lax.fori_loop(0, TM_FFN // SLAB, body, 0, unroll=SLAB_UNROLL)


def _ffn(x, norm2_g, norm1_g, mod, layer, j, w1, w3, w2):
    tm, tf = TM_FFN, TF_FFN
    nf = D_FF // tf

    def fa(s):
        return 2 * s

    def fb(s):
        return jnp.minimum(2 * s + 1, nf - 1)

    return pl.pallas_call(
        functools.partial(_ffn_kernel, nf=nf),
        out_shape=(jax.ShapeDtypeStruct((T, D_MODEL), F32),
                   jax.ShapeDtypeStruct((T, D_MODEL), BF16)),
        grid=(T // tm, (nf + 1) // 2),
        in_specs=[
            pl.BlockSpec((tm, D_MODEL), lambda m, f: (m, 0)),
            _layer_spec((1, D_MODEL), layer),
            _mod_spec(layer, 3, tm),
            _mod_spec(layer, 4, tm),
            _mod_spec(layer, 5, tm),
            _layer_spec((1, D_MODEL), layer + 1),
            _mod_spec(layer + 1, 0, tm),
            _mod_spec(layer + 1, 1, tm),
            pl.BlockSpec((None, D_MODEL, tf), lambda m, s: (j, 0, fa(s))),
            pl.BlockSpec((None, D_MODEL, tf), lambda m, s: (j, 0, fa(s))),
            pl.BlockSpec((None, tf, D_MODEL), lambda m, s: (j, fa(s), 0)),
            pl.BlockSpec((None, D_MODEL, tf), lambda m, s: (j, 0, fb(s))),
            pl.BlockSpec((None, D_MODEL, tf), lambda m, s: (j, 0, fb(s))),
            pl.BlockSpec((None, tf, D_MODEL), lambda m, s: (j, fb(s), 0)),
        ],
        out_specs=(pl.BlockSpec((tm, D_MODEL), lambda m, s: (m, 0)),
                   pl.BlockSpec((tm, D_MODEL), lambda m, s: (m, 0))),
        scratch_shapes=[pltpu.VMEM((tm, D_MODEL), BF16)],
        compiler_params=_cparams(("parallel", "arbitrary")),
        name="ffn",
    )(x, norm2_g, mod, mod, mod, norm1_g, mod, mod, w1, w3, w2, w1, w3, w2)


def _router_kernel(x_ref, g_ref, sh_ref, sc_ref, rwt_ref, rb_ref, h_ref, info_ref, p_ref, cnt_ref,
                   carry_ref):
    i = pl.program_id(0)
    tm = TM_ROUTE

    @pl.when(i == 0)
    def _():
        carry_ref[...] = jnp.zeros_like(carry_ref)

    _fill_modnorm(x_ref, g_ref[...], sc_ref[...], sh_ref[...], h_ref, tm)
    logits = lax.dot_general(rwt_ref[...], h_ref[...], _NT, preferred_element_type=F32,
                             precision=lax.Precision.HIGHEST) + rb_ref[:, 0:1]
    eidx = lax.broadcasted_iota(I32, (N_EXPERTS, tm), 0).astype(F32)
    none = float(N_EXPERTS)
    m1 = jnp.max(logits, axis=0, keepdims=True)
    i1 = jnp.min(jnp.where(logits == m1, eidx, none), axis=0, keepdims=True)
    rest = jnp.where(eidx == i1, -jnp.inf, logits)
    m2 = jnp.max(rest, axis=0, keepdims=True)
    i2 = jnp.min(jnp.where(rest == m2, eidx, none), axis=0, keepdims=True)
    e = jnp.exp(m2 - m1)
    den = 1.0 + e
    sel1 = eidx == i1
    sel2 = eidx == i2
    onehot = (sel1 | sel2).astype(BF16)
    r = lax.broadcasted_iota(I32, (tm, tm), 0)
    c = lax.broadcasted_iota(I32, (tm, tm), 1)
    upper = (r <= c).astype(BF16)
    cnt = jnp.dot(onehot, upper, preferred_element_type=F32) + carry_ref[:, 0:1]
    r1 = jnp.sum(jnp.where(sel1, cnt, 0.0), axis=0, keepdims=True) - 1.0
    r2 = jnp.sum(jnp.where(sel2, cnt, 0.0), axis=0, keepdims=True) - 1.0
    info_ref[...] = jnp.zeros_like(info_ref)
    info_ref[0:1, :] = i1.astype(I32)
    info_ref[1:2, :] = i2.astype(I32)
    info_ref[2:3, :] = r1.astype(I32)
    info_ref[3:4, :] = r2.astype(I32)
    p_ref[...] = jnp.zeros_like(p_ref)
    p_ref[0:1, :] = 1.0 / den
    p_ref[1:2, :] = e / den
    total = cnt[:, tm - 1:tm]
    carry_ref[...] = jnp.broadcast_to(total, carry_ref.shape)
    cnt_ref[...] = jnp.broadcast_to(total, cnt_ref.shape).astype(I32)


def _router(x, norm_g, mod, layer, j, rw_t, rb):
    tm = TM_ROUTE
    nt = T // tm
    return pl.pallas_call(
        _router_kernel,
        out_shape=(jax.ShapeDtypeStruct((T, D_MODEL), F32),
                   jax.ShapeDtypeStruct((nt, 8, tm), I32),
                   jax.ShapeDtypeStruct((nt, 8, tm), F32),
                   jax.ShapeDtypeStruct((N_EXPERTS, 128), I32)),
        grid=(nt,),
        in_specs=[
            pl.BlockSpec((tm, D_MODEL), lambda i: (i, 0)),
            _layer_spec((1, D_MODEL), layer),
            _mod_spec(layer, 3, tm),
            _mod_spec(layer, 4, tm),
            _layer_spec((N_EXPERTS, D_MODEL), j),
            _layer_spec((N_EXPERTS, 128), j),
        ],
        out_specs=(pl.BlockSpec((tm, D_MODEL), lambda i: (i, 0)),
                   pl.BlockSpec((None, 8, tm), lambda i: (i, 0, 0)),
                   pl.BlockSpec((None, 8, tm), lambda i: (i, 0, 0)),
                   pl.BlockSpec((N_EXPERTS, 128), lambda i: (0, 0))),
        scratch_shapes=[pltpu.VMEM((N_EXPERTS, 128), F32)],
        compiler_params=_cparams(("arbitrary",)),
        name="router",
    )(x, norm_g, mod, mod, rw_t, rb)


def _dispatch_kernel(pos1_ref, pos2_ref, zstart_ref, zcount_ref, h_ref, o_hbm, zrow_ref, sem):
    i = pl.program_id(0)
    td = TD_DISP
    base = i * td

    def row_copy(src_row, dst_row):
        return pltpu.make_async_copy(h_ref.at[pl.ds(src_row, 1)], o_hbm.at[pl.ds(dst_row, 1)], sem)

    def zero_copy(dst_row):
        return pltpu.make_async_copy(zrow_ref, o_hbm.at[pl.ds(dst_row, 1)], sem)

    @pl.when(i == 0)
    def _():
        zrow_ref[...] = jnp.zeros_like(zrow_ref)
        for e in range(N_EXPERTS + 1):
            def zbody(r, carry):
                zero_copy(zstart_ref[e] + r).start()
                return carry
            lax.fori_loop(0, zcount_ref[e], zbody, 0)

            def zwait(r, carry):
                zero_copy(0).wait()
                return carry
            lax.fori_loop(0, zcount_ref[e], zwait, 0)

    def body(r, carry):
        row_copy(r, pos1_ref[base + r]).start()
        row_copy(r, pos2_ref[base + r]).start()
        return carry
    lax.fori_loop(0, td, body, 0, unroll=DMA_UNROLL)

    tile_copy = pltpu.make_async_copy(h_ref, o_hbm.at[pl.ds(0, td)], sem)
    tile_copy.wait()
    tile_copy.wait()


def _dispatch(pos1, pos2, zstart, zcount, h):
    td = TD_DISP
    return pl.pallas_call(
        _dispatch_kernel,
        out_shape=jax.ShapeDtypeStruct((ROWS_CAP, D_MODEL), F32),
        grid_spec=pltpu.PrefetchScalarGridSpec(
            num_scalar_prefetch=4,
            grid=(T // td,),
            in_specs=[pl.BlockSpec((td, D_MODEL), lambda i, *_: (i, 0))],
            out_specs=pl.BlockSpec(memory_space=pl.ANY),
            scratch_shapes=[pltpu.VMEM((1, D_MODEL), F32), pltpu.SemaphoreType.DMA],
        ),
        compiler_params=_cparams(("arbitrary",)),
        name="dispatch",
    )(pos1, pos2, zstart, zcount, h)


def _expert_kernel(te_ref, nact_ref, hs_ref, w1a_ref, w3a_ref, w2a_ref, w1b_ref, w3b_ref, w2b_ref,
                   y_ref, hb_ref, *, nf):
    i = pl.program_id(0)
    s = pl.program_id(1)
    active = i < nact_ref[0]

    @pl.when(s == 0)
    def _():
        y_ref[...] = jnp.zeros_like(y_ref)

    @pl.when(active & (s == 0))
    def _():
        def body(k, carry):
            r = pl.multiple_of(k * SLAB, SLAB)
            hb_ref[pl.ds(r, SLAB), :] = hs_ref[pl.ds(r, SLAB), :].astype(BF16)
            return carry
        lax.fori_loop(0, TR_EXP // SLAB, body, 0, unroll=SLAB_UNROLL)

    @pl.when(active)
    def _():
        _swiglu_accumulate(hb_ref, w1a_ref, w3a_ref, w2a_ref, y_ref, TR_EXP)

    @pl.when(active & (2 * s + 1 < nf))
    def _():
        _swiglu_accumulate(hb_ref, w1b_ref, w3b_ref, w2b_ref, y_ref, TR_EXP)


def _experts(tile_expert, n_active, hs, j, w1, w3, w2):
    tr, tf = TR_EXP, TF_EXP
    nf = D_FF // tf

    def row_map(i, s, te, na):
        return (jnp.minimum(i, na[0] - 1), 0)

    def out_map(i, s, te, na):
        return (i, 0)

    def fa(i, s, na):
        return jnp.where(i < na[0], 2 * s, nf - 1)

    def fb(i, s, na):
        return jnp.where(i < na[0], jnp.minimum(2 * s + 1, nf - 1), nf - 1)

    def w13(fsel):
        return pl.BlockSpec((None, None, D_MODEL, tf), lambda i, s, te, na: (j, te[i], 0, fsel(i, s, na)))

    def w2s(fsel):
        return pl.BlockSpec((None, None, tf, D_MODEL), lambda i, s, te, na: (j, te[i], fsel(i, s, na), 0))

    return pl.pallas_call(
        functools.partial(_expert_kernel, nf=nf),
        out_shape=jax.ShapeDtypeStruct((ROWS_CAP, D_MODEL), F32),
        grid_spec=pltpu.PrefetchScalarGridSpec(
            num_scalar_prefetch=2,
            grid=(NT_EXP, (nf + 1) // 2),
            in_specs=[
                pl.BlockSpec((tr, D_MODEL), row_map),
                w13(fa), w13(fa), w2s(fa), w13(fb), w13(fb), w2s(fb),
            ],
            out_specs=pl.BlockSpec((tr, D_MODEL), out_map),
            scratch_shapes=[pltpu.VMEM((tr, D_MODEL), BF16)],
        ),
        compiler_params=_cparams(("arbitrary", "arbitrary")),
        name="experts",
    )(tile_expert, n_active, hs, w1, w3, w2, w1, w3, w2)


def _gather_residual(pos1_ref, pos2_ref, x_ref, p1_ref, p2_ref, g2_ref, y_hbm, buf_ref, sem):
    i = pl.program_id(0)
    n = pl.num_programs(0)
    tc = TC_COMB
    slot = i % 2

    def start_gathers(tile, s):
        base = tile * tc

        def body(r, carry):
            pltpu.make_async_copy(y_hbm.at[pl.ds(pos1_ref[base + r], 1)],
                                  buf_ref.at[s, 0, pl.ds(r, 1)], sem.at[s]).start()
            pltpu.make_async_copy(y_hbm.at[pl.ds(pos2_ref[base + r], 1)],
                                  buf_ref.at[s, 1, pl.ds(r, 1)], sem.at[s]).start()
            return carry
        lax.fori_loop(0, tc, body, 0, unroll=DMA_UNROLL)

    @pl.when(i == 0)
    def _():
        start_gathers(0, 0)

    @pl.when(i + 1 < n)
    def _():
        start_gathers(i + 1, 1 - slot)

    pltpu.make_async_copy(y_hbm.at[pl.ds(0, tc)], buf_ref.at[slot, 0], sem.at[slot]).wait()
    pltpu.make_async_copy(y_hbm.at[pl.ds(0, tc)], buf_ref.at[slot, 1], sem.at[slot]).wait()
    mix = p1_ref[...] * buf_ref[slot, 0] + p2_ref[...] * buf_ref[slot, 1]
    return x_ref[...] + g2_ref[...] * mix


def _combine_kernel(pos1_ref, pos2_ref, x_ref, p1_ref, p2_ref, g2_ref, gn_ref, shn_ref, scn_ref, y_hbm,
                    o_ref, hn_ref, buf_ref, sem):
    xn = _gather_residual(pos1_ref, pos2_ref, x_ref, p1_ref, p2_ref, g2_ref, y_hbm, buf_ref, sem)
    o_ref[...] = xn
    hn_ref[...] = _modnorm(xn, gn_ref[...] * (1.0 + scn_ref[...]), shn_ref[...]).astype(BF16)


def _combine_final_kernel(pos1_ref, pos2_ref, x_ref, p1_ref, p2_ref, g2_ref, gf_ref, y_hbm,
                          yp_ref, ys_ref, buf_ref, sem, *, n_ctx_tiles):
    xn = _gather_residual(pos1_ref, pos2_ref, x_ref, p1_ref, p2_ref, g2_ref, y_hbm, buf_ref, sem)
    y = xn * lax.rsqrt(jnp.mean(xn * xn, axis=-1, keepdims=True) + EPS) * gf_ref[...]
    i = pl.program_id(0)

    @pl.when(i < n_ctx_tiles)
    def _():
        yp_ref[...] = y

    @pl.when(i >= n_ctx_tiles)
    def _():
        ys_ref[...] = y


def _combine_call(body, extra_specs, out_shape, out_specs, name, args):
    tc = TC_COMB
    row_tile = pl.BlockSpec((tc, D_MODEL), lambda i, *_: (i, 0))
    col = pl.BlockSpec((tc, 1), lambda i, *_: (i, 0))
    return pl.pallas_call(
        body,
        out_shape=out_shape,
        grid_spec=pltpu.PrefetchScalarGridSpec(
            num_scalar_prefetch=2,
            grid=(T // tc,),
            in_specs=[row_tile, col, col] + extra_specs + [pl.BlockSpec(memory_space=pl.ANY)],
            out_specs=out_specs,
            scratch_shapes=[pltpu.VMEM((2, 2, tc, D_MODEL), F32), pltpu.SemaphoreType.DMA((2,))],
        ),
        compiler_params=_cparams(("arbitrary",)),
        name=name,
    )(*args)


def _combine(pos1, pos2, x, p1, p2, mod, layer, norm1_g, y):
    tc = TC_COMB
    row_tile = pl.BlockSpec((tc, D_MODEL), lambda i, *_: (i, 0))
    return _combine_call(
        _combine_kernel,
        [_mod_spec(layer, 5, tc), _layer_spec((1, D_MODEL), layer + 1),
         _mod_spec(layer + 1, 0, tc), _mod_spec(layer + 1, 1, tc)],
        (jax.ShapeDtypeStruct((T, D_MODEL), F32), jax.ShapeDtypeStruct((T, D_MODEL), BF16)),
        (row_tile, row_tile), "combine",
        (pos1, pos2, x, p1, p2, mod, norm1_g, mod, mod, y))


def _combine_final(pos1, pos2, x, p1, p2, mod, layer, final_g, y):
    tc = TC_COMB
    nct = TP // tc
    return _combine_call(
        functools.partial(_combine_final_kernel, n_ctx_tiles=nct),
        [_mod_spec(layer, 5, tc), pl.BlockSpec((1, D_MODEL), lambda i, *_: (0, 0))],
        (jax.ShapeDtypeStruct((TP, D_MODEL), F32), jax.ShapeDtypeStruct((TS, D_MODEL), F32)),
        (pl.BlockSpec((tc, D_MODEL), lambda i, *_: (jnp.minimum(i, nct - 1), 0)),
         pl.BlockSpec((tc, D_MODEL), lambda i, *_: (jnp.maximum(i - nct, 0), 0))),
        "combine_final",
        (pos1, pos2, x, p1, p2, mod, final_g, y))


def _moe(x, norm_g, mod, layer, j, rw_t, rb, w1, w3, w2, norm1_g, final_g):
    h, info, probs, counts = _router(x, norm_g, mod, layer, j, rw_t, rb)
    counts = counts[:, 0]
    cap = ((counts + TR_EXP - 1) // TR_EXP) * TR_EXP
    ends = jnp.cumsum(cap)
    starts = ends - cap
    e1 = info[:, 0, :].reshape(T)
    e2 = info[:, 1, :].reshape(T)
    pos1 = starts[e1] + info[:, 2, :].reshape(T)
    pos2 = starts[e2] + info[:, 3, :].reshape(T)
    tile_end = ends // TR_EXP
    n_active = tile_end[-1:]
    tiles = jnp.minimum(jnp.arange(NT_EXP, dtype=I32), n_active[0] - 1)
    tile_expert = jnp.sum((tiles[:, None] >= tile_end[None, :]).astype(I32), axis=1)
    zstart = jnp.concatenate([starts + counts, ends[-1:]])
    zcount = jnp.concatenate([cap - counts, ROWS_CAP - ends[-1:]])
    hs = _dispatch(pos1, pos2, zstart, zcount, h)
    y = _experts(tile_expert, n_active, hs, j, w1, w3, w2)
    p1 = probs[:, 0, :].reshape(T, 1)
    p2 = probs[:, 1, :].reshape(T, 1)
    if layer == DEPTH - 1:
        return _combine_final(pos1, pos2, x, p1, p2, mod, layer, final_g, y)
    return _combine(pos1, pos2, x, p1, p2, mod, layer, norm1_g, y)


def _grid_pos_embed():
    rows = DEC_SEQ // GRID_W
    r, col = jnp.meshgrid(jnp.arange(rows, dtype=F32), jnp.arange(GRID_W, dtype=F32), indexing='ij')
    r = r.reshape(-1)
    col = col.reshape(-1)
    quarter = D_MODEL // 4
    freq = jnp.exp(-math.log(10000.0) * jnp.arange(quarter, dtype=F32) / quarter)
    ar = r[:, None] * freq
    ac = col[:, None] * freq
    return jnp.concatenate([jnp.sin(ar), jnp.cos(ar), jnp.sin(ac), jnp.cos(ac)], axis=-1)


def kernel(x_prompt, x_sample, state_gla, c, c_ctx, norm1_g, norm2_g, w_mod, b_mod, w_in,
           sgu_ln_g, sgu_ln_b, w_spatial, b_spatial, gla_a2, gla_ab, gla_norm_g,
           w_branch_a, w_branch_b, w_out, ffn_w1, ffn_w3, ffn_w2,
           moe_router, moe_router_b, moe_w1, moe_w3, moe_w2, final_g):
    cond = jnp.zeros((COND_PAD, D_MODEL), F32).at[0].set(c_ctx).at[1:N_COND].set(c)
    mod = _mod_table(cond, w_mod, b_mod).reshape(DEPTH, COND_PAD, 6, 1, D_MODEL)

    lr_end = W_IN_LR + 2 * GLA_RANK
    w_main = jnp.concatenate([w_in[:, :, lr_end:], w_in[:, :, :W_IN_LR]], axis=2).astype(BF16)
    w_lr = jnp.pad(w_in[:, :, W_IN_LR:lr_end], ((0, 0), (0, 0), (0, LR_PAD - 2 * GLA_RANK))).astype(BF16)
    norm1 = norm1_g.reshape(DEPTH, 1, D_MODEL)
    norm2 = norm2_g.reshape(DEPTH, 1, D_MODEL)
    ln_g = sgu_ln_g.reshape(DEPTH, 1, D_A)
    ln_b = sgu_ln_b.reshape(DEPTH, 1, D_A)
    w_s = w_spatial.astype(BF16)
    bs_full = jnp.repeat(jnp.swapaxes(b_spatial, 1, 2), A_GROUP, axis=2)
    a2_pad = jnp.zeros((DEPTH, LR_PAD, GLA_DK), F32)
    a2f = a2_pad.at[:, :GLA_RANK].set(gla_a2[:, 0]).astype(BF16)
    a2b = a2_pad.at[:, GLA_RANK:2 * GLA_RANK].set(gla_a2[:, 1]).astype(BF16)
    ab = gla_ab.reshape(DEPTH, 2, 1, GLA_DK)
    gla_g = gla_norm_g.reshape(DEPTH, 1, GLA_DV)
    w_a = w_branch_a.astype(BF16)
    w_b = w_branch_b.astype(BF16)
    w_o = w_out.astype(BF16)
    f_w1, f_w3, f_w2 = ffn_w1.astype(BF16), ffn_w3.astype(BF16), ffn_w2.astype(BF16)
    m_w1, m_w3, m_w2 = moe_w1.astype(BF16), moe_w3.astype(BF16), moe_w2.astype(BF16)
    rw_t = jnp.swapaxes(moe_router, 1, 2)
    rb = jnp.broadcast_to(moe_router_b[:, :, None], moe_router_b.shape + (128,))

    x, h = _embed(x_prompt.reshape(TP, D_MODEL), x_sample.reshape(TS, D_MODEL), _grid_pos_embed(),
                  norm1, mod)
    new_state = jnp.zeros((BATCH, DEPTH, 2, N_GLA_HEADS, GLA_DK_HEAD, GLA_DV_HEAD), F32)
    for l in range(DEPTH):
        proj, lr = _proj(h, l, w_main, w_lr)
        a = _sgu(proj, l, ln_g, ln_b, w_s, bs_full)
        o_f, o_b, new_state = _gla(proj, lr, a2f, a2b, ab, state_gla, new_state, l)
        x = _mixout(a, o_f, o_b, proj, gla_g, x, mod, l, w_a, w_b, w_o)
        j = l // 2
        if l % 2 == 0:
            x, h = _ffn(x, norm2, norm1, mod, l, j, f_w1, f_w3, f_w2)
        elif l < DEPTH - 1:
            x, h = _moe(x, norm2, mod, l, j, rw_t, rb, m_w1, m_w3, m_w2, norm1, final_g[None])
        else:
            y_prompt, y_sample = _moe(x, norm2, mod, l, j, rw_t, rb, m_w1, m_w3, m_w2, norm1,
                                      final_g[None])

    return (y_prompt.reshape(BATCH, SEQ, D_MODEL), y_sample.reshape(DEC_BATCH, DEC_SEQ, D_MODEL),
            new_state)
```

```python
import functools
import math

import numpy as np
import jax
import jax.numpy as jnp
from jax import lax
from jax.experimental import pallas as pl
from jax.experimental.pallas import tpu as pltpu

F32 = jnp.float32
BF16 = jnp.bfloat16
I32 = jnp.int32

D_MODEL = 2048
BATCH = 32
SEQ = 256
DEPTH = 4
DEC_BATCH = 8
DEC_SEQ = 2048
GRID_W = 64
CHUNK_A = 128
D_A = 1024
N_A_GROUPS = 8
A_GROUP = D_A // N_A_GROUPS
N_GLA_HEADS = 4
GLA_DK = D_MODEL // 4
GLA_DV = D_MODEL // 2
GLA_DK_HEAD = GLA_DK // N_GLA_HEADS
GLA_DV_HEAD = GLA_DV // N_GLA_HEADS
GLA_RANK = 16
GLA_TAU = 16.0
GLA_CHUNK = 64
D_FF = 5632
N_EXPERTS = 8
TOP_K = 2
EPS = 1e-6

TP = BATCH * SEQ
TS = DEC_BATCH * DEC_SEQ
T = TP + TS
N_COND = 1 + DEC_BATCH
COND_PAD = 16

W_IN_LR = 2 * D_A + 2 * GLA_DK + 2 * GLA_DV
C_GA, C_GB, C_U, C_V, C_Q, C_K, C_VG, C_R = 0, 2048, 4096, 5120, 6144, 6656, 7168, 8192
N_MAIN = 9216
LR_PAD = 128

VMEM_LIMIT = 56 * 1024 * 1024

TM_PROJ, TN_PROJ = 1024, 2304
TM_EMBED = 512
TM_SGU = 512
TB_GLA = 256
TM_MIX = 256
TM_FFN, TF_FFN = 512, 512
TM_ROUTE = 512
TD_DISP = 256
TR_EXP, TF_EXP = 512, 512
TC_COMB = 256
SLAB = 32
SLAB_UNROLL = 4
RS_SWIGLU = 512
COMB_SLAB = 16

NT_EXP = (TOP_K * T) // TR_EXP + N_EXPERTS
ROWS_CAP = NT_EXP * TR_EXP


def _cparams(sem):
    return pltpu.CompilerParams(dimension_semantics=sem, vmem_limit_bytes=VMEM_LIMIT)


def _mod_row(i, tm):
    return jnp.maximum((i * tm) // DEC_SEQ - (TP // DEC_SEQ - 1), 0)


def _mod_spec(layer, j, tm, width=D_MODEL):
    return pl.BlockSpec((None, None, None, 1, width),
                        lambda m, *_: (layer, _mod_row(m, tm), j, 0, 0))


def _layer_spec(shape, layer):
    zeros = (0,) * len(shape)
    return pl.BlockSpec((None,) + tuple(shape), lambda *_: (layer,) + zeros)


def _modnorm(x, gain, sh):
    return x * lax.rsqrt(jnp.mean(x * x, axis=-1, keepdims=True) + EPS) * gain + sh


def _fill_modnorm(x_ref, g, sc, sh, h_ref, rows):
    gain = g * (1.0 + sc)

    def body(i, carry):
        r = pl.multiple_of(i * SLAB, SLAB)
        h_ref[pl.ds(r, SLAB), :] = _modnorm(x_ref[pl.ds(r, SLAB), :], gain, sh).astype(h_ref.dtype)
        return carry
    lax.fori_loop(0, rows // SLAB, body, 0, unroll=SLAB_UNROLL)


def _silu(x):
    return x * jax.nn.sigmoid(x)


def _gelu_tanh(x):
    return jax.nn.gelu(x, approximate=True)


def _log_sigmoid(x):
    return jnp.minimum(x, 0.0) - jnp.log(1.0 + jnp.exp(-jnp.abs(x)))


def _swiglu_accumulate(h_ref, w1_ref, w3_ref, w2_ref, acc_ref, rows):
    for r in range(rows // RS_SWIGLU):
        sl = slice(r * RS_SWIGLU, (r + 1) * RS_SWIGLU)
        h = h_ref[sl, :]
        u = _silu(jnp.dot(h, w1_ref[...], preferred_element_type=F32)) * jnp.dot(
            h, w3_ref[...], preferred_element_type=F32)
        acc_ref[sl, :] += jnp.dot(u.astype(BF16), w2_ref[...], preferred_element_type=F32)


def _embed_kernel(xp_ref, xs_ref, emb_ref, g_ref, sh_ref, sc_ref, o_ref, h_ref, *, n_ctx_tiles):
    i = pl.program_id(0)

    @pl.when(i < n_ctx_tiles)
    def _():
        o_ref[...] = xp_ref[...]

    @pl.when(i >= n_ctx_tiles)
    def _():
        o_ref[...] = xs_ref[...] + emb_ref[...]

    _fill_modnorm(o_ref, g_ref[...], sc_ref[...], sh_ref[...], h_ref, TM_EMBED)


def _embed(xp, xs, emb, norm_g, mod):
    tm = TM_EMBED
    nct = TP // tm
    per_seq = DEC_SEQ // tm
    return pl.pallas_call(
        functools.partial(_embed_kernel, n_ctx_tiles=nct),
        out_shape=(jax.ShapeDtypeStruct((T, D_MODEL), F32),
                   jax.ShapeDtypeStruct((T, D_MODEL), BF16)),
        grid=(T // tm,),
        in_specs=[
            pl.BlockSpec((tm, D_MODEL), lambda i: (jnp.minimum(i, nct - 1), 0)),
            pl.BlockSpec((tm, D_MODEL), lambda i: (jnp.maximum(i - nct, 0), 0)),
            pl.BlockSpec((tm, D_MODEL), lambda i: (jnp.maximum(i - nct, 0) % per_seq, 0)),
            _layer_spec((1, D_MODEL), 0),
            _mod_spec(0, 0, tm),
            _mod_spec(0, 1, tm),
        ],
        out_specs=(pl.BlockSpec((tm, D_MODEL), lambda i: (i, 0)),
                   pl.BlockSpec((tm, D_MODEL), lambda i: (i, 0))),
        compiler_params=_cparams(("parallel",)),
        name="embed",
    )(xp, xs, emb, norm_g, mod, mod)


def _mod_kernel(c_ref, w_ref, b_ref, o_ref):
    c = c_ref[...]
    s = _silu(c).astype(BF16)
    o_ref[...] = jnp.dot(s, w_ref[...].astype(BF16), preferred_element_type=F32) + b_ref[...]


def _mod_table(cond, w_mod, b_mod):
    tn = 1024
    n6 = 6 * D_MODEL
    return pl.pallas_call(
        _mod_kernel,
        out_shape=jax.ShapeDtypeStruct((DEPTH, COND_PAD, n6), F32),
        grid=(DEPTH, n6 // tn),
        in_specs=[
            pl.BlockSpec((COND_PAD, D_MODEL), lambda l, n: (0, 0)),
            pl.BlockSpec((None, D_MODEL, tn), lambda l, n: (l, 0, n)),
            pl.BlockSpec((None, 1, tn), lambda l, n: (l, 0, n)),
        ],
        out_specs=pl.BlockSpec((None, COND_PAD, tn), lambda l, n: (l, 0, n)),
        compiler_params=_cparams(("parallel", "parallel")),
        name="mod_table",
    )(cond, w_mod, b_mod.reshape(DEPTH, 1, n6))


def _proj_kernel(h_ref, w_ref, wlr_ref, o_ref, lr_ref):
    @pl.when(pl.program_id(1) == 0)
    def _():
        lr_ref[...] = jnp.dot(h_ref[...], wlr_ref[...], preferred_element_type=F32)

    o_ref[...] = jnp.dot(h_ref[...], w_ref[...], preferred_element_type=F32).astype(o_ref.dtype)


def _proj(h, layer, w_main, w_lr):
    tm, tn = TM_PROJ, TN_PROJ
    return pl.pallas_call(
        _proj_kernel,
        out_shape=(jax.ShapeDtypeStruct((T, N_MAIN), BF16),
                   jax.ShapeDtypeStruct((T, LR_PAD), F32)),
        grid=(T // tm, N_MAIN // tn),
        in_specs=[
            pl.BlockSpec((tm, D_MODEL), lambda m, n: (m, 0)),
            pl.BlockSpec((None, D_MODEL, tn), lambda m, n: (layer, 0, n)),
            _layer_spec((D_MODEL, LR_PAD), layer),
        ],
        out_specs=(pl.BlockSpec((tm, tn), lambda m, n: (m, n)),
                   pl.BlockSpec((tm, LR_PAD), lambda m, n: (m, 0))),
        compiler_params=_cparams(("parallel", "arbitrary")),
        name="proj",
    )(h, w_main, w_lr)


def _sgu_kernel(pu_ref, pv_ref, lng_ref, lnb_ref, ws_ref, bs_ref, a_ref, vn_ref):
    gv = _gelu_tanh(pv_ref[...].astype(F32))
    mu = jnp.mean(gv, axis=-1, keepdims=True)
    d = gv - mu
    var = jnp.mean(d * d, axis=-1, keepdims=True)
    vn_ref[...] = (d * lax.rsqrt(var + EPS) * lng_ref[...] + lnb_ref[...]).astype(BF16)
    for c in range(TM_SGU // CHUNK_A):
        rows = slice(c * CHUNK_A, (c + 1) * CHUNK_A)
        for g in range(N_A_GROUPS):
            cols = slice(g * A_GROUP, (g + 1) * A_GROUP)
            f = jnp.dot(ws_ref[g], vn_ref[rows, cols], preferred_element_type=F32) + bs_ref[:, cols]
            gu = _gelu_tanh(pu_ref[rows, cols].astype(F32))
            a_ref[rows, cols] = (gu * f).astype(BF16)


def _sgu(proj, layer, ln_g, ln_b, w_s, bs_full):
    tm = TM_SGU
    return pl.pallas_call(
        _sgu_kernel,
        out_shape=jax.ShapeDtypeStruct((T, D_A), BF16),
        grid=(T // tm,),
        in_specs=[
            pl.BlockSpec((tm, D_A), lambda i: (i, C_U // D_A)),
            pl.BlockSpec((tm, D_A), lambda i: (i, C_V // D_A)),
            _layer_spec((1, D_A), layer),
            _layer_spec((1, D_A), layer),
            _layer_spec((N_A_GROUPS, CHUNK_A, CHUNK_A), layer),
            _layer_spec((CHUNK_A, D_A), layer),
        ],
        out_specs=pl.BlockSpec((tm, D_A), lambda i: (i, 0)),
        scratch_shapes=[pltpu.VMEM((tm, D_A), BF16)],
        compiler_params=_cparams(("parallel",)),
        name="sgu",
    )(proj, proj, ln_g, ln_b, w_s, bs_full)


def _split3(x):
    hi = x.astype(BF16)
    r = x - hi.astype(F32)
    mid = r.astype(BF16)
    lo = (r - mid.astype(F32)).astype(BF16)
    return hi, mid, lo


def _tri_cumsum(tri, x):
    hi, mid, lo = _split3(x)
    return (jnp.dot(tri, hi, preferred_element_type=F32)
            + jnp.dot(tri, mid, preferred_element_type=F32)
            + jnp.dot(tri, lo, preferred_element_type=F32))


_NT = (((1,), (1,)), ((), ()))
_TN = (((0,), (0,)), ((), ()))


def _gla_decay(lr_ref, a2_ref, ab_ref, tb, backward):
    z = jnp.dot(lr_ref[...].astype(BF16), a2_ref[...], preferred_element_type=F32) + ab_ref[...]
    la = _log_sigmoid(z) / GLA_TAU
    r = lax.broadcasted_iota(I32, (tb, tb), 0)
    c = lax.broadcasted_iota(I32, (tb, tb), 1)
    same = (r // GLA_CHUNK) == (c // GLA_CHUNK)
    tri = (same & ((c >= r) if backward else (c <= r))).astype(BF16)
    return _tri_cumsum(tri, la)


def _gla_block(dirs, tb):
    nchunk = tb // GLA_CHUNK
    scale = GLA_DK_HEAD ** -0.5
    rr = lax.broadcasted_iota(I32, (GLA_CHUNK, GLA_CHUNK), 0)
    cc = lax.broadcasted_iota(I32, (GLA_CHUNK, GLA_CHUNK), 1)
    decays = [_gla_decay(d[0], d[1], d[2], tb, d[8]) for d in dirs]
    per_dir = []
    for (lr_ref, a2_ref, ab_ref, q_ref, k_ref, v_ref, o_ref, st_ref, backward), b in zip(dirs, decays):
        units = []
        per_dir.append(units)
        eb = jnp.exp(b)
        enb = jnp.exp(-b)
        mask = (cc >= rr) if backward else (cc <= rr)
        order = range(nchunk - 1, -1, -1) if backward else range(nchunk)
        for ci in order:
            rows = slice(ci * GLA_CHUNK, (ci + 1) * GLA_CHUNK)
            last = ci * GLA_CHUNK if backward else (ci + 1) * GLA_CHUNK - 1
            btot = b[last:last + 1, :]
            eke = jnp.exp(btot - b[rows, :])
            gdec = jnp.exp(btot)
            for h in range(N_GLA_HEADS):
                kc = slice(h * GLA_DK_HEAD, (h + 1) * GLA_DK_HEAD)
                vc = slice(h * GLA_DV_HEAD, (h + 1) * GLA_DV_HEAD)
                q = q_ref[rows, kc].astype(F32) * scale
                k = k_ref[rows, kc].astype(F32)
                qd = (q * eb[rows, kc]).astype(BF16)
                kd = (k * enb[rows, kc]).astype(BF16)
                ke = (k * eke[:, kc]).astype(BF16)
                vv = v_ref[rows, vc]
                att = lax.dot_general(qd, kd, _NT, preferred_element_type=F32)
                att = jnp.where(mask, att, 0.0).astype(BF16)
                dst = lax.dot_general(vv, ke, _TN, preferred_element_type=F32)
                units.append(dict(o_ref=o_ref, st_ref=st_ref, rows=rows, vc=vc, h=h, qd=qd, vv=vv,
                                  att=att, dst=dst, gdec=gdec[:, kc]))
    units = [u for group in zip(*per_dir) for u in group]
    for u in units:
        u["o_intra"] = jnp.dot(u["att"], u["vv"], preferred_element_type=F32)
    states = {}
    for u in units:
        key = (id(u["st_ref"]), u["h"])
        st = states[key] if key in states else u["st_ref"][u["h"]]
        u["o_ref"][u["rows"], u["vc"]] = u["o_intra"] + lax.dot_general(
            u["qd"], st.astype(BF16), _NT, preferred_element_type=F32)
        states[key] = st * u["gdec"] + u["dst"]
    for (_, _, _, _, _, _, _, st_ref, _) in dirs:
        for h in range(N_GLA_HEADS):
            st_ref[h] = states[(id(st_ref), h)]


def _gla_kernel(fwd_ref, bwd_ref, flag_ref, seq_ref,
                lrf_ref, lrb_ref, a2f_ref, a2b_ref, abf_ref, abb_ref,
                qf_ref, kf_ref, vf_ref, qb_ref, kb_ref, vb_ref, s0_ref, sprev_ref,
                of_ref, ob_ref, sfin_ref, stf_ref, stb_ref):
    del sprev_ref
    i = pl.program_id(0)
    first = (flag_ref[i] & 1) == 1
    last = (flag_ref[i] & 2) == 2
    is_ctx = seq_ref[i] < BATCH

    @pl.when(first & is_ctx)
    def _():
        stf_ref[...] = jnp.zeros_like(stf_ref)
        stb_ref[...] = jnp.zeros_like(stb_ref)

    @pl.when(first & jnp.logical_not(is_ctx))
    def _():
        for h in range(N_GLA_HEADS):
            stf_ref[h] = s0_ref[0, h].T
            stb_ref[h] = s0_ref[1, h].T

    _gla_block([(lrf_ref, a2f_ref, abf_ref, qf_ref, kf_ref, vf_ref, of_ref, stf_ref, False),
                (lrb_ref, a2b_ref, abb_ref, qb_ref, kb_ref, vb_ref, ob_ref, stb_ref, True)], TB_GLA)

    @pl.when(last & is_ctx)
    def _():
        for h in range(N_GLA_HEADS):
            sfin_ref[0, h] = stf_ref[h].T
            sfin_ref[1, h] = stb_ref[h].T


def _gla_tables():
    tb = TB_GLA
    fwd, bwd, flag, seq = [], [], [], []
    for s, (row0, seqlen) in enumerate([(b * SEQ, SEQ) for b in range(BATCH)]
                                       + [(TP + b * DEC_SEQ, DEC_SEQ) for b in range(DEC_BATCH)]):
        nb = seqlen // tb
        for j in range(nb):
            fwd.append(row0 // tb + j)
            bwd.append(row0 // tb + nb - 1 - j)
            flag.append((1 if j == 0 else 0) | (2 if j == nb - 1 else 0))
            seq.append(s)
    return tuple(np.asarray(t, np.int32) for t in (fwd, bwd, flag, seq))


def _gla(proj, lr, a2f, a2b, ab, s0, s_all, layer):
    tb = TB_GLA
    fwd_t, bwd_t, flag_t, seq_t = _gla_tables()
    nsteps = fwd_t.shape[0]
    n_prefetch = 4

    def fwd(i, fw, bw, fl, sq):
        return fw[i]

    def bwd(i, fw, bw, fl, sq):
        return bw[i]

    def col_spec(rowfn, width, col):
        return pl.BlockSpec((tb, width), lambda i, *t: (rowfn(i, *t), col // width))

    state_block = (2, N_GLA_HEADS, GLA_DK_HEAD, GLA_DV_HEAD)
    in_specs = [
        pl.BlockSpec((tb, LR_PAD), lambda i, *t: (fwd(i, *t), 0)),
        pl.BlockSpec((tb, LR_PAD), lambda i, *t: (bwd(i, *t), 0)),
        _layer_spec((LR_PAD, GLA_DK), layer),
        _layer_spec((LR_PAD, GLA_DK), layer),
        pl.BlockSpec((None, None, 1, GLA_DK), lambda i, *t: (layer, 0, 0, 0)),
        pl.BlockSpec((None, None, 1, GLA_DK), lambda i, *t: (layer, 1, 0, 0)),
        col_spec(fwd, GLA_DK, C_Q), col_spec(fwd, GLA_DK, C_K), col_spec(fwd, GLA_DV, C_VG),
        col_spec(bwd, GLA_DK, C_Q), col_spec(bwd, GLA_DK, C_K), col_spec(bwd, GLA_DV, C_VG),
        pl.BlockSpec((None, None) + state_block,
                     lambda i, fw, bw, fl, sq: (jnp.maximum(sq[i] - BATCH, 0), layer, 0, 0, 0, 0)),
        pl.BlockSpec(memory_space=pl.ANY),
    ]
    out_shape = (jax.ShapeDtypeStruct((T, GLA_DV), F32),
                 jax.ShapeDtypeStruct((T, GLA_DV), F32),
                 jax.ShapeDtypeStruct((BATCH, DEPTH) + state_block, F32))
    out_specs = (pl.BlockSpec((tb, GLA_DV), lambda i, *t: (fwd(i, *t), 0)),
                 pl.BlockSpec((tb, GLA_DV), lambda i, *t: (bwd(i, *t), 0)),
                 pl.BlockSpec((None, None) + state_block,
                              lambda i, fw, bw, fl, sq: (jnp.minimum(sq[i], BATCH - 1), layer, 0, 0, 0, 0)))
    return pl.pallas_call(
        _gla_kernel,
        out_shape=out_shape,
        grid_spec=pltpu.PrefetchScalarGridSpec(
            num_scalar_prefetch=n_prefetch,
            grid=(nsteps,),
            in_specs=in_specs,
            out_specs=out_specs,
            scratch_shapes=[pltpu.VMEM((N_GLA_HEADS, GLA_DV_HEAD, GLA_DK_HEAD), F32),
                            pltpu.VMEM((N_GLA_HEADS, GLA_DV_HEAD, GLA_DK_HEAD), F32)],
        ),
        input_output_aliases={n_prefetch + len(in_specs) - 1: 2},
        compiler_params=_cparams(("arbitrary",)),
        name="gla",
    )(fwd_t, bwd_t, flag_t, seq_t, lr, lr, a2f, a2b, ab, ab,
      proj, proj, proj, proj, proj, proj, s0, s_all)


def _mixout_kernel(a_ref, of_ref, ob_ref, pr_ref, gn_ref, ga_ref, gb_ref, x_ref, g1_ref,
                   wa_ref, wb_ref, wo_ref, out_ref, o_ref):
    for h in range(N_GLA_HEADS):
        vc = slice(h * GLA_DV_HEAD, (h + 1) * GLA_DV_HEAD)
        o = of_ref[:, vc] + ob_ref[:, vc]
        o = o * lax.rsqrt(jnp.mean(o * o, axis=-1, keepdims=True) + EPS)
        o_ref[:, vc] = (o * gn_ref[:, vc] * _silu(pr_ref[:, vc].astype(F32))).astype(BF16)
    ya = jnp.dot(a_ref[...], wa_ref[...], preferred_element_type=F32)
    yb = jnp.dot(o_ref[...], wb_ref[...], preferred_element_type=F32)
    merged = (jax.nn.sigmoid(ga_ref[...].astype(F32)) * ya
              + jax.nn.sigmoid(gb_ref[...].astype(F32)) * yb).astype(BF16)
    y = jnp.dot(merged, wo_ref[...], preferred_element_type=F32)
    out_ref[...] = x_ref[...] + g1_ref[...] * y


def _resident_spec(shape, layer):
    zeros = (0,) * len(shape)
    return pl.BlockSpec((None,) + tuple(shape), lambda *_: (layer,) + zeros,
                        pipeline_mode=pl.Buffered(1))


def _mixout(a, o_f, o_b, proj, gla_g, x, mod, layer, w_a, w_b, w_o):
    tm = TM_MIX
    return pl.pallas_call(
        _mixout_kernel,
        out_shape=jax.ShapeDtypeStruct((T, D_MODEL), F32),
        grid=(T // tm,),
        in_specs=[
            pl.BlockSpec((tm, D_A), lambda m: (m, 0)),
            pl.BlockSpec((tm, GLA_DV), lambda m: (m, 0)),
            pl.BlockSpec((tm, GLA_DV), lambda m: (m, 0)),
            pl.BlockSpec((tm, GLA_DV), lambda m: (m, C_R // GLA_DV)),
            _layer_spec((1, GLA_DV), layer),
            pl.BlockSpec((tm, D_MODEL), lambda m: (m, C_GA // D_MODEL)),
            pl.BlockSpec((tm, D_MODEL), lambda m: (m, C_GB // D_MODEL)),
            pl.BlockSpec((tm, D_MODEL), lambda m: (m, 0)),
            _mod_spec(layer, 2, tm),
            _resident_spec((D_A, D_MODEL), layer),
            _resident_spec((GLA_DV, D_MODEL), layer),
            _resident_spec((D_MODEL, D_MODEL), layer),
        ],
        out_specs=pl.BlockSpec((tm, D_MODEL), lambda m: (m, 0)),
        scratch_shapes=[pltpu.VMEM((tm, GLA_DV), BF16)],
        compiler_params=_cparams(("parallel",)),
        name="mixout",
    )(a, o_f, o_b, proj, gla_g, proj, proj, x, mod, w_a, w_b, w_o)


def _ffn_kernel(x_ref, g_ref, sh_ref, sc_ref, g2_ref, gn_ref, shn_ref, scn_ref, w1_ref, w3_ref, w2_ref,
                o_ref, hn_ref, h_ref, *, nf):
    f = pl.program_id(1)

    @pl.when(f == 0)
    def _():
        _fill_modnorm(x_ref, g_ref[...], sc_ref[...], sh_ref[...], h_ref, TM_FFN)
        o_ref[...] = jnp.zeros_like(o_ref)

    _swiglu_accumulate(h_ref, w1_ref, w3_ref, w2_ref, o_ref, TM_FFN)

    @pl.when(f == nf - 1)
    def _():
        g2 = g2_ref[...]
        gain_n, shn = gn_ref[...] * (1.0 + scn_ref[...]), shn_ref[...]

        def body(i, carry):
            r = pl.multiple_of(i * SLAB, SLAB)
            xn = x_ref[pl.ds(r, SLAB), :] + g2 * o_ref[pl.ds(r, SLAB), :]
            o_ref[pl.ds(r, SLAB), :] = xn
            hn_ref[pl.ds(r, SLAB), :] = _modnorm(xn, gain_n, shn).astype(BF16)
            return carry
        lax.fori_loop(0, TM_FFN // SLAB, body, 0, unroll=SLAB_UNROLL)


def _ffn(x, norm2_g, norm1_g, mod, layer, j, w1, w3, w2):
    tm, tf = TM_FFN, TF_FFN
    nf = D_FF // tf
    return pl.pallas_call(
        functools.partial(_ffn_kernel, nf=nf),
        out_shape=(jax.ShapeDtypeStruct((T, D_MODEL), F32),
                   jax.ShapeDtypeStruct((T, D_MODEL), BF16)),
        grid=(T // tm, nf),
        in_specs=[
            pl.BlockSpec((tm, D_MODEL), lambda m, f: (m, 0)),
            _layer_spec((1, D_MODEL), layer),
            _mod_spec(layer, 3, tm),
            _mod_spec(layer, 4, tm),
            _mod_spec(layer, 5, tm),
            _layer_spec((1, D_MODEL), layer + 1),
            _mod_spec(layer + 1, 0, tm),
            _mod_spec(layer + 1, 1, tm),
            pl.BlockSpec((None, D_MODEL, tf), lambda m, f: (j, 0, f)),
            pl.BlockSpec((None, D_MODEL, tf), lambda m, f: (j, 0, f)),
            pl.BlockSpec((None, tf, D_MODEL), lambda m, f: (j, f, 0)),
        ],
        out_specs=(pl.BlockSpec((tm, D_MODEL), lambda m, f: (m, 0)),
                   pl.BlockSpec((tm, D_MODEL), lambda m, f: (m, 0))),
        scratch_shapes=[pltpu.VMEM((tm, D_MODEL), BF16)],
        compiler_params=_cparams(("parallel", "arbitrary")),
        name="ffn",
    )(x, norm2_g, mod, mod, mod, norm1_g, mod, mod, w1, w3, w2)


def _router_kernel(x_ref, g_ref, sh_ref, sc_ref, rwa_ref, rwb_ref, rb_ref, h_ref, info_ref, p_ref, cnt_ref,
                   carry_ref, hhi_ref, hlo_ref, upper_ref):
    i = pl.program_id(0)
    tm = TM_ROUTE

    @pl.when(i == 0)
    def _():
        carry_ref[...] = jnp.zeros_like(carry_ref)
        r = lax.broadcasted_iota(I32, (tm, tm), 0)
        c = lax.broadcasted_iota(I32, (tm, tm), 1)
        upper_ref[...] = (r <= c).astype(BF16)

    gain = g_ref[...] * (1.0 + sc_ref[...])
    sh = sh_ref[...]

    def fill(k, carry):
        rows = pl.ds(pl.multiple_of(k * SLAB, SLAB), SLAB)
        y = _modnorm(x_ref[rows, :], gain, sh)
        hi = y.astype(BF16)
        h_ref[rows, :] = y
        hhi_ref[rows, :] = hi
        hlo_ref[rows, :] = (y - hi.astype(F32)).astype(BF16)
        return carry
    lax.fori_loop(0, tm // SLAB, fill, 0, unroll=SLAB_UNROLL)

    t = lax.dot_general(rwa_ref[...], hhi_ref[...], _NT, preferred_element_type=F32)
    u = lax.dot_general(rwb_ref[...], hlo_ref[...], _NT, preferred_element_type=F32)
    logits = t[0:N_EXPERTS] + t[N_EXPERTS:2 * N_EXPERTS] + u[0:N_EXPERTS] + rb_ref[:, 0:1]
    eidx = lax.broadcasted_iota(I32, (N_EXPERTS, tm), 0).astype(F32)
    none = float(N_EXPERTS)
    m1 = jnp.max(logits, axis=0, keepdims=True)
    i1 = jnp.min(jnp.where(logits == m1, eidx, none), axis=0, keepdims=True)
    rest = jnp.where(eidx == i1, -jnp.inf, logits)
    m2 = jnp.max(rest, axis=0, keepdims=True)
    i2 = jnp.min(jnp.where(rest == m2, eidx, none), axis=0, keepdims=True)
    e = jnp.exp(m2 - m1)
    den = 1.0 + e
    sel1 = eidx == i1
    sel2 = eidx == i2
    onehot = (sel1 | sel2).astype(BF16)
    cnt = jnp.dot(onehot, upper_ref[...], preferred_element_type=F32) + carry_ref[:, 0:1]
    r1 = jnp.sum(jnp.where(sel1, cnt, 0.0), axis=0, keepdims=True) - 1.0
    r2 = jnp.sum(jnp.where(sel2, cnt, 0.0), axis=0, keepdims=True) - 1.0
    info_ref[...] = jnp.zeros_like(info_ref)
    info_ref[0:1, :] = i1.astype(I32)
    info_ref[1:2, :] = i2.astype(I32)
    info_ref[2:3, :] = r1.astype(I32)
    info_ref[3:4, :] = r2.astype(I32)
    p_ref[...] = jnp.zeros_like(p_ref)
    p_ref[0:1, :] = 1.0 / den
    p_ref[1:2, :] = e / den
    total = cnt[:, tm - 1:tm]
    carry_ref[...] = jnp.broadcast_to(total, carry_ref.shape)
    cnt_ref[...] = jnp.broadcast_to(total, cnt_ref.shape).astype(I32)


def _router(x, norm_g, mod, layer, j, rw_a, rw_b, rb):
    tm = TM_ROUTE
    nt = T // tm
    return pl.pallas_call(
        _router_kernel,
        out_shape=(jax.ShapeDtypeStruct((T, D_MODEL), F32),
                   jax.ShapeDtypeStruct((nt, 8, tm), I32),
                   jax.ShapeDtypeStruct((nt, 8, tm), F32),
                   jax.ShapeDtypeStruct((N_EXPERTS, 128), I32)),
        grid=(nt,),
        in_specs=[
            pl.BlockSpec((tm, D_MODEL), lambda i: (i, 0)),
            _layer_spec((1, D_MODEL), layer),
            _mod_spec(layer, 3, tm),
            _mod_spec(layer, 4, tm),
            _layer_spec((2 * N_EXPERTS, D_MODEL), j),
            _layer_spec((2 * N_EXPERTS, D_MODEL), j),
            _layer_spec((N_EXPERTS, 128), j),
        ],
        out_specs=(pl.BlockSpec((tm, D_MODEL), lambda i: (i, 0)),
                   pl.BlockSpec((None, 8, tm), lambda i: (i, 0, 0)),
                   pl.BlockSpec((None, 8, tm), lambda i: (i, 0, 0)),
                   pl.BlockSpec((N_EXPERTS, 128), lambda i: (0, 0))),
        scratch_shapes=[pltpu.VMEM((N_EXPERTS, 128), F32), pltpu.VMEM((tm, D_MODEL), BF16),
                        pltpu.VMEM((tm, D_MODEL), BF16), pltpu.VMEM((tm, tm), BF16)],
        compiler_params=_cparams(("arbitrary",)),
        name="router",
    )(x, norm_g, mod, mod, rw_a, rw_b, rb)


def _dispatch_kernel(pos1_ref, pos2_ref, zstart_ref, zcount_ref, h_ref, o_hbm, zrow_ref, sem):
    i = pl.program_id(0)
    td = TD_DISP
    base = i * td

    def row_copy(src_row, dst_row):
        return pltpu.make_async_copy(h_ref.at[pl.ds(src_row, 1)], o_hbm.at[pl.ds(dst_row, 1)], sem)

    def zero_copy(dst_row):
        return pltpu.make_async_copy(zrow_ref, o_hbm.at[pl.ds(dst_row, 1)], sem)

    @pl.when(i == 0)
    def _():
        zrow_ref[...] = jnp.zeros_like(zrow_ref)
        for e in range(N_EXPERTS + 1):
            def zbody(r, carry):
                zero_copy(zstart_ref[e] + r).start()
                return carry
            lax.fori_loop(0, zcount_ref[e], zbody, 0)

            def zwait(r, carry):
                zero_copy(0).wait()
                return carry
            lax.fori_loop(0, zcount_ref[e], zwait, 0)

    for r in range(td):
        row_copy(r, pos1_ref[base + r]).start()
        row_copy(r, pos2_ref[base + r]).start()

    tile_copy = pltpu.make_async_copy(h_ref, o_hbm.at[pl.ds(0, td)], sem)
    tile_copy.wait()
    tile_copy.wait()


def _dispatch(pos1, pos2, zstart, zcount, h):
    td = TD_DISP
    return pl.pallas_call(
        _dispatch_kernel,
        out_shape=jax.ShapeDtypeStruct((ROWS_CAP, D_MODEL), F32),
        grid_spec=pltpu.PrefetchScalarGridSpec(
            num_scalar_prefetch=4,
            grid=(T // td,),
            in_specs=[pl.BlockSpec((td, D_MODEL), lambda i, *_: (i, 0))],
            out_specs=pl.BlockSpec(memory_space=pl.ANY),
            scratch_shapes=[pltpu.VMEM((1, D_MODEL), F32), pltpu.SemaphoreType.DMA],
        ),
        compiler_params=_cparams(("arbitrary",)),
        name="dispatch",
    )(pos1, pos2, zstart, zcount, h)


def _expert_kernel(te_ref, nact_ref, hs_ref, w1_ref, w3_ref, w2_ref, y_ref, hb_ref):
    i = pl.program_id(0)
    f = pl.program_id(1)
    active = i < nact_ref[0]

    @pl.when(f == 0)
    def _():
        y_ref[...] = jnp.zeros_like(y_ref)

    @pl.when(active & (f == 0))
    def _():
        def body(s, carry):
            r = pl.multiple_of(s * SLAB, SLAB)
            hb_ref[pl.ds(r, SLAB), :] = hs_ref[pl.ds(r, SLAB), :].astype(BF16)
            return carry
        lax.fori_loop(0, TR_EXP // SLAB, body, 0, unroll=SLAB_UNROLL)

    @pl.when(active)
    def _():
        _swiglu_accumulate(hb_ref, w1_ref, w3_ref, w2_ref, y_ref, TR_EXP)


def _experts(tile_expert, n_active, hs, j, w1, w3, w2):
    tr, tf = TR_EXP, TF_EXP
    nf = D_FF // tf

    def row_map(i, f, te, na):
        return (jnp.minimum(i, na[0] - 1), 0)

    def out_map(i, f, te, na):
        return (i, 0)

    def w13_map(i, f, te, na):
        return (j, te[i], 0, jnp.where(i < na[0], f, nf - 1))

    def w2_map(i, f, te, na):
        return (j, te[i], jnp.where(i < na[0], f, nf - 1), 0)

    return pl.pallas_call(
        _expert_kernel,
        out_shape=jax.ShapeDtypeStruct((ROWS_CAP, D_MODEL), F32),
        grid_spec=pltpu.PrefetchScalarGridSpec(
            num_scalar_prefetch=2,
            grid=(NT_EXP, nf),
            in_specs=[
                pl.BlockSpec((tr, D_MODEL), row_map),
                pl.BlockSpec((None, None, D_MODEL, tf), w13_map),
                pl.BlockSpec((None, None, D_MODEL, tf), w13_map),
                pl.BlockSpec((None, None, tf, D_MODEL), w2_map),
            ],
            out_specs=pl.BlockSpec((tr, D_MODEL), out_map),
            scratch_shapes=[pltpu.VMEM((tr, D_MODEL), BF16)],
        ),
        compiler_params=_cparams(("arbitrary", "arbitrary")),
        name="experts",
    )(tile_expert, n_active, hs, w1, w3, w2)


def _gather_residual(pos1_ref, pos2_ref, x_ref, p1_ref, p2_ref, g2_ref, y_hbm, buf_ref, sem, emit):
    i = pl.program_id(0)
    n = pl.num_programs(0)
    tc = TC_COMB
    slot = i % 2

    def start_rows(tile, s, r0):
        base = tile * tc
        for d in range(COMB_SLAB):
            r = r0 + d
            pltpu.make_async_copy(y_hbm.at[pl.ds(pos1_ref[base + r], 1)],
                                  buf_ref.at[s, 0, pl.ds(r, 1)], sem.at[s]).start()
            pltpu.make_async_copy(y_hbm.at[pl.ds(pos2_ref[base + r], 1)],
                                  buf_ref.at[s, 1, pl.ds(r, 1)], sem.at[s]).start()

    @pl.when(i == 0)
    def _():
        def body(k, carry):
            start_rows(0, 0, k * COMB_SLAB)
            return carry
        lax.fori_loop(0, tc // COMB_SLAB, body, 0)

    pltpu.make_async_copy(y_hbm.at[pl.ds(0, tc)], buf_ref.at[slot, 0], sem.at[slot]).wait()
    pltpu.make_async_copy(y_hbm.at[pl.ds(0, tc)], buf_ref.at[slot, 1], sem.at[slot]).wait()
    g2 = g2_ref[...]

    def tile_pass(prefetch):
        for k in range(tc // COMB_SLAB):
            r0 = k * COMB_SLAB
            if prefetch:
                start_rows(i + 1, 1 - slot, r0)
            rows = slice(r0, r0 + COMB_SLAB)
            mix = (p1_ref[rows, :] * buf_ref[slot, 0, rows, :]
                   + p2_ref[rows, :] * buf_ref[slot, 1, rows, :])
            emit(rows, x_ref[rows, :] + g2 * mix)

    @pl.when(i + 1 < n)
    def _():
        tile_pass(True)

    @pl.when(i + 1 >= n)
    def _():
        tile_pass(False)


def _combine_kernel(pos1_ref, pos2_ref, x_ref, p1_ref, p2_ref, g2_ref, gn_ref, shn_ref, scn_ref, y_hbm,
                    o_ref, hn_ref, buf_ref, sem):
    gain_n, shn = gn_ref[...] * (1.0 + scn_ref[...]), shn_ref[...]

    def emit(rows, xn):
        o_ref[rows, :] = xn
        hn_ref[rows, :] = _modnorm(xn, gain_n, shn).astype(BF16)

    _gather_residual(pos1_ref, pos2_ref, x_ref, p1_ref, p2_ref, g2_ref, y_hbm, buf_ref, sem, emit)


def _combine_final_kernel(pos1_ref, pos2_ref, x_ref, p1_ref, p2_ref, g2_ref, gf_ref, y_hbm,
                          yp_ref, ys_ref, buf_ref, sem, yt_ref, *, n_ctx_tiles):
    gf = gf_ref[...]

    def emit(rows, xn):
        yt_ref[rows, :] = xn * lax.rsqrt(jnp.mean(xn * xn, axis=-1, keepdims=True) + EPS) * gf

    _gather_residual(pos1_ref, pos2_ref, x_ref, p1_ref, p2_ref, g2_ref, y_hbm, buf_ref, sem, emit)
    i = pl.program_id(0)

    @pl.when(i < n_ctx_tiles)
    def _():
        yp_ref[...] = yt_ref[...]

    @pl.when(i >= n_ctx_tiles)
    def _():
        ys_ref[...] = yt_ref[...]


def _combine_call(body, extra_specs, out_shape, out_specs, name, args, extra_scratch=()):
    tc = TC_COMB
    row_tile = pl.BlockSpec((tc, D_MODEL), lambda i, *_: (i, 0))
    col = pl.BlockSpec((tc, 1), lambda i, *_: (i, 0))
    return pl.pallas_call(
        body,
        out_shape=out_shape,
        grid_spec=pltpu.PrefetchScalarGridSpec(
            num_scalar_prefetch=2,
            grid=(T // tc,),
            in_specs=[row_tile, col, col] + extra_specs + [pl.BlockSpec(memory_space=pl.ANY)],
            out_specs=out_specs,
            scratch_shapes=[pltpu.VMEM((2, 2, tc, D_MODEL), F32), pltpu.SemaphoreType.DMA((2,))]
            + list(extra_scratch),
        ),
        compiler_params=_cparams(("arbitrary",)),
        name=name,
    )(*args)


def _combine(pos1, pos2, x, p1, p2, mod, layer, norm1_g, y):
    tc = TC_COMB
    row_tile = pl.BlockSpec((tc, D_MODEL), lambda i, *_: (i, 0))
    return _combine_call(
        _combine_kernel,
        [_mod_spec(layer, 5, tc), _layer_spec((1, D_MODEL), layer + 1),
         _mod_spec(layer + 1, 0, tc), _mod_spec(layer + 1, 1, tc)],
        (jax.ShapeDtypeStruct((T, D_MODEL), F32), jax.ShapeDtypeStruct((T, D_MODEL), BF16)),
        (row_tile, row_tile), "combine",
        (pos1, pos2, x, p1, p2, mod, norm1_g, mod, mod, y))


def _combine_final(pos1, pos2, x, p1, p2, mod, layer, final_g, y):
    tc = TC_COMB
    nct = TP // tc
    return _combine_call(
        functools.partial(_combine_final_kernel, n_ctx_tiles=nct),
        [_mod_spec(layer, 5, tc), pl.BlockSpec((1, D_MODEL), lambda i, *_: (0, 0))],
        (jax.ShapeDtypeStruct((TP, D_MODEL), F32), jax.ShapeDtypeStruct((TS, D_MODEL), F32)),
        (pl.BlockSpec((tc, D_MODEL), lambda i, *_: (jnp.minimum(i, nct - 1), 0)),
         pl.BlockSpec((tc, D_MODEL), lambda i, *_: (jnp.maximum(i - nct, 0), 0))),
        "combine_final",
        (pos1, pos2, x, p1, p2, mod, final_g, y),
        extra_scratch=[pltpu.VMEM((tc, D_MODEL), F32)])


def _moe(x, norm_g, mod, layer, j, rw, rb, w1, w3, w2, norm1_g, final_g):
    h, info, probs, counts = _router(x, norm_g, mod, layer, j, rw[0], rw[1], rb)
    counts = counts[:, 0]
    cap = ((counts + TR_EXP - 1) // TR_EXP) * TR_EXP
    ends = jnp.cumsum(cap)
    starts = ends - cap
    e1 = info[:, 0, :].reshape(T)
    e2 = info[:, 1, :].reshape(T)
    pos1 = starts[e1] + info[:, 2, :].reshape(T)
    pos2 = starts[e2] + info[:, 3, :].reshape(T)
    tile_end = ends // TR_EXP
    n_active = tile_end[-1:]
    tiles = jnp.minimum(jnp.arange(NT_EXP, dtype=I32), n_active[0] - 1)
    tile_expert = jnp.sum((tiles[:, None] >= tile_end[None, :]).astype(I32), axis=1)
    zstart = jnp.concatenate([starts + counts, ends[-1:]])
    zcount = jnp.concatenate([cap - counts, ROWS_CAP - ends[-1:]])
    hs = _dispatch(pos1, pos2, zstart, zcount, h)
    y = _experts(tile_expert, n_active, hs, j, w1, w3, w2)
    p1 = probs[:, 0, :].reshape(T, 1)
    p2 = probs[:, 1, :].reshape(T, 1)
    if layer == DEPTH - 1:
        return _combine_final(pos1, pos2, x, p1, p2, mod, layer, final_g, y)
    return _combine(pos1, pos2, x, p1, p2, mod, layer, norm1_g, y)


def _grid_pos_embed():
    rows = DEC_SEQ // GRID_W
    r, col = jnp.meshgrid(jnp.arange(rows, dtype=F32), jnp.arange(GRID_W, dtype=F32), indexing='ij')
    r = r.reshape(-1)
    col = col.reshape(-1)
    quarter = D_MODEL // 4
    freq = jnp.exp(-math.log(10000.0) * jnp.arange(quarter, dtype=F32) / quarter)
    ar = r[:, None] * freq
    ac = col[:, None] * freq
    return jnp.concatenate([jnp.sin(ar), jnp.cos(ar), jnp.sin(ac), jnp.cos(ac)], axis=-1)


def kernel(x_prompt, x_sample, state_gla, c, c_ctx, norm1_g, norm2_g, w_mod, b_mod, w_in,
           sgu_ln_g, sgu_ln_b, w_spatial, b_spatial, gla_a2, gla_ab, gla_norm_g,
           w_branch_a, w_branch_b, w_out, ffn_w1, ffn_w3, ffn_w2,
           moe_router, moe_router_b, moe_w1, moe_w3, moe_w2, final_g):
    cond = jnp.zeros((COND_PAD, D_MODEL), F32).at[0].set(c_ctx).at[1:N_COND].set(c)
    mod = _mod_table(cond, w_mod, b_mod).reshape(DEPTH, COND_PAD, 6, 1, D_MODEL)

    lr_end = W_IN_LR + 2 * GLA_RANK
    w_main = jnp.concatenate([w_in[:, :, lr_end:], w_in[:, :, :W_IN_LR]], axis=2).astype(BF16)
    w_lr = jnp.pad(w_in[:, :, W_IN_LR:lr_end], ((0, 0), (0, 0), (0, LR_PAD - 2 * GLA_RANK))).astype(BF16)
    norm1 = norm1_g.reshape(DEPTH, 1, D_MODEL)
    norm2 = norm2_g.reshape(DEPTH, 1, D_MODEL)
    ln_g = sgu_ln_g.reshape(DEPTH, 1, D_A)
    ln_b = sgu_ln_b.reshape(DEPTH, 1, D_A)
    w_s = w_spatial.astype(BF16)
    bs_full = jnp.repeat(jnp.swapaxes(b_spatial, 1, 2), A_GROUP, axis=2)
    a2_pad = jnp.zeros((DEPTH, LR_PAD, GLA_DK), F32)
    a2f = a2_pad.at[:, :GLA_RANK].set(gla_a2[:, 0]).astype(BF16)
    a2b = a2_pad.at[:, GLA_RANK:2 * GLA_RANK].set(gla_a2[:, 1]).astype(BF16)
    ab = gla_ab.reshape(DEPTH, 2, 1, GLA_DK)
    gla_g = gla_norm_g.reshape(DEPTH, 1, GLA_DV)
    w_a = w_branch_a.astype(BF16)
    w_b = w_branch_b.astype(BF16)
    w_o = w_out.astype(BF16)
    f_w1, f_w3, f_w2 = ffn_w1.astype(BF16), ffn_w3.astype(BF16), ffn_w2.astype(BF16)
    m_w1, m_w3, m_w2 = moe_w1.astype(BF16), moe_w3.astype(BF16), moe_w2.astype(BF16)
    rw_t = jnp.swapaxes(moe_router, 1, 2)
    rw_hi = rw_t.astype(BF16)
    rw_lo = (rw_t - rw_hi.astype(F32)).astype(BF16)
    rw = (jnp.concatenate([rw_hi, rw_lo], axis=1),
          jnp.concatenate([rw_hi, jnp.zeros_like(rw_hi)], axis=1))
    rb = jnp.broadcast_to(moe_router_b[:, :, None], moe_router_b.shape + (128,))

    x, h = _embed(x_prompt.reshape(TP, D_MODEL), x_sample.reshape(TS, D_MODEL), _grid_pos_embed(),
                  norm1, mod)
    new_state = jnp.zeros((BATCH, DEPTH, 2, N_GLA_HEADS, GLA_DK_HEAD, GLA_DV_HEAD), F32)
    for l in range(DEPTH):
        proj, lr = _proj(h, l, w_main, w_lr)
        a = _sgu(proj, l, ln_g, ln_b, w_s, bs_full)
        o_f, o_b, new_state = _gla(proj, lr, a2f, a2b, ab, state_gla, new_state, l)
        x = _mixout(a, o_f, o_b, proj, gla_g, x, mod, l, w_a, w_b, w_o)
        j = l // 2
        if l % 2 == 0:
            x, h = _ffn(x, norm2, norm1, mod, l, j, f_w1, f_w3, f_w2)
        elif l < DEPTH - 1:
            x, h = _moe(x, norm2, mod, l, j, rw, rb, m_w1, m_w3, m_w2, norm1, final_g[None])
        else:
            y_prompt, y_sample = _moe(x, norm2, mod, l, j, rw, rb, m_w1, m_w3, m_w2, norm1,
                                      final_g[None])

    return (y_prompt.reshape(BATCH, SEQ, D_MODEL), y_sample.reshape(DEC_BATCH, DEC_SEQ, D_MODEL),
            new_state)
```

```python
import functools
import math

import numpy as np
import jax
import jax.numpy as jnp
from jax import lax
from jax.experimental import pallas as pl
from jax.experimental.pallas import tpu as pltpu

F32 = jnp.float32
BF16 = jnp.bfloat16
I32 = jnp.int32
U32 = jnp.uint32

D_MODEL = 2048
BATCH = 32
SEQ = 256
DEPTH = 4
DEC_BATCH = 8
DEC_SEQ = 2048
GRID_W = 64
CHUNK_A = 128
D_A = 1024
N_A_GROUPS = 8
A_GROUP = D_A // N_A_GROUPS
N_GLA_HEADS = 4
GLA_DK = D_MODEL // 4
GLA_DV = D_MODEL // 2
GLA_DK_HEAD = GLA_DK // N_GLA_HEADS
GLA_DV_HEAD = GLA_DV // N_GLA_HEADS
GLA_RANK = 16
GLA_TAU = 16.0
GLA_CHUNK = 64
D_FF = 5632
N_EXPERTS = 8
TOP_K = 2
EPS = 1e-6

TP = BATCH * SEQ
TS = DEC_BATCH * DEC_SEQ
T = TP + TS
N_COND = 1 + DEC_BATCH
COND_PAD = 16

W_IN_LR = 2 * D_A + 2 * GLA_DK + 2 * GLA_DV
C_GA, C_GB, C_U, C_V, C_Q, C_K, C_VG, C_R = 0, 2048, 4096, 5120, 6144, 6656, 7168, 8192
N_MAIN = 9216
LR_PAD = 128
D_PACK = D_MODEL // 2

VMEM_LIMIT = 56 * 1024 * 1024

TM_PROJ, TN_PROJ = 1024, 2304
TM_EMBED = 512
TM_SGU = 512
TB_GLA = 256
TM_MIX = 256
TM_FFN, TF_FFN = 512, 512
TM_ROUTE = 512
TD_DISP = 256
TR_EXP, TF_EXP = 512, 512
TC_COMB = 256
SLAB = 32
SLAB_UNROLL = 4
RS_SWIGLU = 512
COMB_SLAB = 16

NT_EXP = (TOP_K * T) // TR_EXP + N_EXPERTS
ROWS_CAP = NT_EXP * TR_EXP


def _cparams(sem):
    return pltpu.CompilerParams(dimension_semantics=sem, vmem_limit_bytes=VMEM_LIMIT)


def _mod_row(i, tm):
    return jnp.maximum((i * tm) // DEC_SEQ - (TP // DEC_SEQ - 1), 0)


def _mod_spec(layer, j, tm, width=D_MODEL):
    return pl.BlockSpec((None, None, None, 1, width),
                        lambda m, *_: (layer, _mod_row(m, tm), j, 0, 0))


def _layer_spec(shape, layer):
    zeros = (0,) * len(shape)
    return pl.BlockSpec((None,) + tuple(shape), lambda *_: (layer,) + zeros)


def _modnorm(x, gain, sh):
    return x * lax.rsqrt(jnp.mean(x * x, axis=-1, keepdims=True) + EPS) * gain + sh


def _fill_modnorm(x_ref, g, sc, sh, h_ref, rows):
    gain = g * (1.0 + sc)

    def body(i, carry):
        r = pl.multiple_of(i * SLAB, SLAB)
        h_ref[pl.ds(r, SLAB), :] = _modnorm(x_ref[pl.ds(r, SLAB), :], gain, sh).astype(h_ref.dtype)
        return carry
    lax.fori_loop(0, rows // SLAB, body, 0, unroll=SLAB_UNROLL)


def _silu(x):
    return x * jax.nn.sigmoid(x)


def _gelu_tanh(x):
    return jax.nn.gelu(x, approximate=True)


def _log_sigmoid(x):
    return jnp.minimum(x, 0.0) - jnp.log(1.0 + jnp.exp(-jnp.abs(x)))


def _swiglu_accumulate(h_ref, w1_ref, w3_ref, w2_ref, acc_ref, rows):
    for r in range(rows // RS_SWIGLU):
        sl = slice(r * RS_SWIGLU, (r + 1) * RS_SWIGLU)
        h = h_ref[sl, :]
        u = _silu(jnp.dot(h, w1_ref[...], preferred_element_type=F32)) * jnp.dot(
            h, w3_ref[...], preferred_element_type=F32)
        acc_ref[sl, :] += jnp.dot(u.astype(BF16), w2_ref[...], preferred_element_type=F32)


def _embed_kernel(xp_ref, xs_ref, emb_ref, g_ref, sh_ref, sc_ref, o_ref, h_ref, *, n_ctx_tiles):
    i = pl.program_id(0)

    @pl.when(i < n_ctx_tiles)
    def _():
        o_ref[...] = xp_ref[...]

    @pl.when(i >= n_ctx_tiles)
    def _():
        o_ref[...] = xs_ref[...] + emb_ref[...]

    _fill_modnorm(o_ref, g_ref[...], sc_ref[...], sh_ref[...], h_ref, TM_EMBED)


def _embed(xp, xs, emb, norm_g, mod):
    tm = TM_EMBED
    nct = TP // tm
    per_seq = DEC_SEQ // tm
    return pl.pallas_call(
        functools.partial(_embed_kernel, n_ctx_tiles=nct),
        out_shape=(jax.ShapeDtypeStruct((T, D_MODEL), F32),
                   jax.ShapeDtypeStruct((T, D_MODEL), BF16)),
        grid=(T // tm,),
        in_specs=[
            pl.BlockSpec((tm, D_MODEL), lambda i: (jnp.minimum(i, nct - 1), 0)),
            pl.BlockSpec((tm, D_MODEL), lambda i: (jnp.maximum(i - nct, 0), 0)),
            pl.BlockSpec((tm, D_MODEL), lambda i: (jnp.maximum(i - nct, 0) % per_seq, 0)),
            _layer_spec((1, D_MODEL), 0),
            _mod_spec(0, 0, tm),
            _mod_spec(0, 1, tm),
        ],
        out_specs=(pl.BlockSpec((tm, D_MODEL), lambda i: (i, 0)),
                   pl.BlockSpec((tm, D_MODEL), lambda i: (i, 0))),
        compiler_params=_cparams(("parallel",)),
        name="embed",
    )(xp, xs, emb, norm_g, mod, mod)


def _mod_kernel(c_ref, w_ref, b_ref, o_ref):
    c = c_ref[...]
    s = _silu(c).astype(BF16)
    o_ref[...] = jnp.dot(s, w_ref[...].astype(BF16), preferred_element_type=F32) + b_ref[...]


def _mod_table(cond, w_mod, b_mod):
    tn = 1024
    n6 = 6 * D_MODEL
    return pl.pallas_call(
        _mod_kernel,
        out_shape=jax.ShapeDtypeStruct((DEPTH, COND_PAD, n6), F32),
        grid=(DEPTH, n6 // tn),
        in_specs=[
            pl.BlockSpec((COND_PAD, D_MODEL), lambda l, n: (0, 0)),
            pl.BlockSpec((None, D_MODEL, tn), lambda l, n: (l, 0, n)),
            pl.BlockSpec((None, 1, tn), lambda l, n: (l, 0, n)),
        ],
        out_specs=pl.BlockSpec((None, COND_PAD, tn), lambda l, n: (l, 0, n)),
        compiler_params=_cparams(("parallel", "parallel")),
        name="mod_table",
    )(cond, w_mod, b_mod.reshape(DEPTH, 1, n6))


def _proj_kernel(h_ref, w_ref, wlr_ref, o_ref, lr_ref):
    @pl.when(pl.program_id(1) == 0)
    def _():
        lr_ref[...] = jnp.dot(h_ref[...], wlr_ref[...], preferred_element_type=F32)

    o_ref[...] = jnp.dot(h_ref[...], w_ref[...], preferred_element_type=F32).astype(o_ref.dtype)


def _proj(h, layer, w_main, w_lr):
    tm, tn = TM_PROJ, TN_PROJ
    return pl.pallas_call(
        _proj_kernel,
        out_shape=(jax.ShapeDtypeStruct((T, N_MAIN), BF16),
                   jax.ShapeDtypeStruct((T, LR_PAD), F32)),
        grid=(T // tm, N_MAIN // tn),
        in_specs=[
            pl.BlockSpec((tm, D_MODEL), lambda m, n: (m, 0)),
            pl.BlockSpec((None, D_MODEL, tn), lambda m, n: (layer, 0, n)),
            _layer_spec((D_MODEL, LR_PAD), layer),
        ],
        out_specs=(pl.BlockSpec((tm, tn), lambda m, n: (m, n)),
                   pl.BlockSpec((tm, LR_PAD), lambda m, n: (m, 0))),
        compiler_params=_cparams(("parallel", "arbitrary")),
        name="proj",
    )(h, w_main, w_lr)


def _sgu_kernel(pu_ref, pv_ref, lng_ref, lnb_ref, ws_ref, bs_ref, a_ref, vn_ref):
    gv = _gelu_tanh(pv_ref[...].astype(F32))
    mu = jnp.mean(gv, axis=-1, keepdims=True)
    d = gv - mu
    var = jnp.mean(d * d, axis=-1, keepdims=True)
    vn_ref[...] = (d * lax.rsqrt(var + EPS) * lng_ref[...] + lnb_ref[...]).astype(BF16)
    for c in range(TM_SGU // CHUNK_A):
        rows = slice(c * CHUNK_A, (c + 1) * CHUNK_A)
        for g in range(N_A_GROUPS):
            cols = slice(g * A_GROUP, (g + 1) * A_GROUP)
            f = jnp.dot(ws_ref[g], vn_ref[rows, cols], preferred_element_type=F32) + bs_ref[:, cols]
            gu = _gelu_tanh(pu_ref[rows, cols].astype(F32))
            a_ref[rows, cols] = (gu * f).astype(BF16)


def _sgu(proj, layer, ln_g, ln_b, w_s, bs_full):
    tm = TM_SGU
    return pl.pallas_call(
        _sgu_kernel,
        out_shape=jax.ShapeDtypeStruct((T, D_A), BF16),
        grid=(T // tm,),
        in_specs=[
            pl.BlockSpec((tm, D_A), lambda i: (i, C_U // D_A)),
            pl.BlockSpec((tm, D_A), lambda i: (i, C_V // D_A)),
            _layer_spec((1, D_A), layer),
            _layer_spec((1, D_A), layer),
            _layer_spec((N_A_GROUPS, CHUNK_A, CHUNK_A), layer),
            _layer_spec((CHUNK_A, D_A), layer),
        ],
        out_specs=pl.BlockSpec((tm, D_A), lambda i: (i, 0)),
        scratch_shapes=[pltpu.VMEM((tm, D_A), BF16)],
        compiler_params=_cparams(("parallel",)),
        name="sgu",
    )(proj, proj, ln_g, ln_b, w_s, bs_full)


def _split3(x):
    hi = x.astype(BF16)
    r = x - hi.astype(F32)
    mid = r.astype(BF16)
    lo = (r - mid.astype(F32)).astype(BF16)
    return hi, mid, lo


def _tri_cumsum(tri, x):
    hi, mid, lo = _split3(x)
    return (jnp.dot(tri, hi, preferred_element_type=F32)
            + jnp.dot(tri, mid, preferred_element_type=F32)
            + jnp.dot(tri, lo, preferred_element_type=F32))


_NT = (((1,), (1,)), ((), ()))
_TN = (((0,), (0,)), ((), ()))


def _gla_decay(lr_ref, a2_ref, ab_ref, tb, backward):
    z = jnp.dot(lr_ref[...].astype(BF16), a2_ref[...], preferred_element_type=F32) + ab_ref[...]
    la = _log_sigmoid(z) / GLA_TAU
    r = lax.broadcasted_iota(I32, (tb, tb), 0)
    c = lax.broadcasted_iota(I32, (tb, tb), 1)
    same = (r // GLA_CHUNK) == (c // GLA_CHUNK)
    tri = (same & ((c >= r) if backward else (c <= r))).astype(BF16)
    return _tri_cumsum(tri, la)


def _gla_block(dirs, tb):
    nchunk = tb // GLA_CHUNK
    scale = GLA_DK_HEAD ** -0.5
    rr = lax.broadcasted_iota(I32, (GLA_CHUNK, GLA_CHUNK), 0)
    cc = lax.broadcasted_iota(I32, (GLA_CHUNK, GLA_CHUNK), 1)
    decays = [_gla_decay(d[0], d[1], d[2], tb, d[8]) for d in dirs]
    per_dir = []
    for (lr_ref, a2_ref, ab_ref, q_ref, k_ref, v_ref, o_ref, st_ref, backward), b in zip(dirs, decays):
        units = []
        per_dir.append(units)
        eb = jnp.exp(b)
        enb = jnp.exp(-b)
        mask = (cc >= rr) if backward else (cc <= rr)
        order = range(nchunk - 1, -1, -1) if backward else range(nchunk)
        for ci in order:
            rows = slice(ci * GLA_CHUNK, (ci + 1) * GLA_CHUNK)
            last = ci * GLA_CHUNK if backward else (ci + 1) * GLA_CHUNK - 1
            btot = b[last:last + 1, :]
            eke = jnp.exp(btot - b[rows, :])
            gdec = jnp.exp(btot)
            for h in range(N_GLA_HEADS):
                kc = slice(h * GLA_DK_HEAD, (h + 1) * GLA_DK_HEAD)
                vc = slice(h * GLA_DV_HEAD, (h + 1) * GLA_DV_HEAD)
                q = q_ref[rows, kc].astype(F32) * scale
                k = k_ref[rows, kc].astype(F32)
                qd = (q * eb[rows, kc]).astype(BF16)
                kd = (k * enb[rows, kc]).astype(BF16)
                ke = (k * eke[:, kc]).astype(BF16)
                vv = v_ref[rows, vc]
                att = lax.dot_general(qd, kd, _NT, preferred_element_type=F32)
                att = jnp.where(mask, att, 0.0).astype(BF16)
                dst = lax.dot_general(vv, ke, _TN, preferred_element_type=F32)
                units.append(dict(o_ref=o_ref, st_ref=st_ref, rows=rows, vc=vc, h=h, qd=qd, vv=vv,
                                  att=att, dst=dst, gdec=gdec[:, kc]))
    units = [u for group in zip(*per_dir) for u in group]
    for u in units:
        u["o_intra"] = jnp.dot(u["att"], u["vv"], preferred_element_type=F32)
    states = {}
    for u in units:
        key = (id(u["st_ref"]), u["h"])
        st = states[key] if key in states else u["st_ref"][u["h"]]
        u["o_ref"][u["rows"], u["vc"]] = u["o_intra"] + lax.dot_general(
            u["qd"], st.astype(BF16), _NT, preferred_element_type=F32)
        states[key] = st * u["gdec"] + u["dst"]
    for (_, _, _, _, _, _, _, st_ref, _) in dirs:
        for h in range(N_GLA_HEADS):
            st_ref[h] = states[(id(st_ref), h)]


def _gla_kernel(fwd_ref, bwd_ref, flag_ref, seq_ref,
                lrf_ref, lrb_ref, a2f_ref, a2b_ref, abf_ref, abb_ref,
                qf_ref, kf_ref, vf_ref, qb_ref, kb_ref, vb_ref, s0_ref, sprev_ref,
                of_ref, ob_ref, sfin_ref, stf_ref, stb_ref):
    del sprev_ref
    i = pl.program_id(0)
    first = (flag_ref[i] & 1) == 1
    last = (flag_ref[i] & 2) == 2
    is_ctx = seq_ref[i] < BATCH

    @pl.when(first & is_ctx)
    def _():
        stf_ref[...] = jnp.zeros_like(stf_ref)
        stb_ref[...] = jnp.zeros_like(stb_ref)

    @pl.when(first & jnp.logical_not(is_ctx))
    def _():
        for h in range(N_GLA_HEADS):
            stf_ref[h] = s0_ref[0, h].T
            stb_ref[h] = s0_ref[1, h].T

    _gla_block([(lrf_ref, a2f_ref, abf_ref, qf_ref, kf_ref, vf_ref, of_ref, stf_ref, False),
                (lrb_ref, a2b_ref, abb_ref, qb_ref, kb_ref, vb_ref, ob_ref, stb_ref, True)], TB_GLA)

    @pl.when(last & is_ctx)
    def _():
        for h in range(N_GLA_HEADS):
            sfin_ref[0, h] = stf_ref[h].T
            sfin_ref[1, h] = stb_ref[h].T


def _gla_tables():
    tb = TB_GLA
    fwd, bwd, flag, seq = [], [], [], []
    for s, (row0, seqlen) in enumerate([(b * SEQ, SEQ) for b in range(BATCH)]
                                       + [(TP + b * DEC_SEQ, DEC_SEQ) for b in range(DEC_BATCH)]):
        nb = seqlen // tb
        for j in range(nb):
            fwd.append(row0 // tb + j)
            bwd.append(row0 // tb + nb - 1 - j)
            flag.append((1 if j == 0 else 0) | (2 if j == nb - 1 else 0))
            seq.append(s)
    return tuple(np.asarray(t, np.int32) for t in (fwd, bwd, flag, seq))


def _gla(proj, lr, a2f, a2b, ab, s0, s_all, layer):
    tb = TB_GLA
    fwd_t, bwd_t, flag_t, seq_t = _gla_tables()
    nsteps = fwd_t.shape[0]
    n_prefetch = 4

    def fwd(i, fw, bw, fl, sq):
        return fw[i]

    def bwd(i, fw, bw, fl, sq):
        return bw[i]

    def col_spec(rowfn, width, col):
        return pl.BlockSpec((tb, width), lambda i, *t: (rowfn(i, *t), col // width))

    state_block = (2, N_GLA_HEADS, GLA_DK_HEAD, GLA_DV_HEAD)
    in_specs = [
        pl.BlockSpec((tb, LR_PAD), lambda i, *t: (fwd(i, *t), 0)),
        pl.BlockSpec((tb, LR_PAD), lambda i, *t: (bwd(i, *t), 0)),
        _layer_spec((LR_PAD, GLA_DK), layer),
        _layer_spec((LR_PAD, GLA_DK), layer),
        pl.BlockSpec((None, None, 1, GLA_DK), lambda i, *t: (layer, 0, 0, 0)),
        pl.BlockSpec((None, None, 1, GLA_DK), lambda i, *t: (layer, 1, 0, 0)),
        col_spec(fwd, GLA_DK, C_Q), col_spec(fwd, GLA_DK, C_K), col_spec(fwd, GLA_DV, C_VG),
        col_spec(bwd, GLA_DK, C_Q), col_spec(bwd, GLA_DK, C_K), col_spec(bwd, GLA_DV, C_VG),
        pl.BlockSpec((None, None) + state_block,
                     lambda i, fw, bw, fl, sq: (jnp.maximum(sq[i] - BATCH, 0), layer, 0, 0, 0, 0)),
        pl.BlockSpec(memory_space=pl.ANY),
    ]
    out_shape = (jax.ShapeDtypeStruct((T, GLA_DV), F32),
                 jax.ShapeDtypeStruct((T, GLA_DV), F32),
                 jax.ShapeDtypeStruct((BATCH, DEPTH) + state_block, F32))
    out_specs = (pl.BlockSpec((tb, GLA_DV), lambda i, *t: (fwd(i, *t), 0)),
                 pl.BlockSpec((tb, GLA_DV), lambda i, *t: (bwd(i, *t), 0)),
                 pl.BlockSpec((None, None) + state_block,
                              lambda i, fw, bw, fl, sq: (jnp.minimum(sq[i], BATCH - 1), layer, 0, 0, 0, 0)))
    return pl.pallas_call(
        _gla_kernel,
        out_shape=out_shape,
        grid_spec=pltpu.PrefetchScalarGridSpec(
            num_scalar_prefetch=n_prefetch,
            grid=(nsteps,),
            in_specs=in_specs,
            out_specs=out_specs,
            scratch_shapes=[pltpu.VMEM((N_GLA_HEADS, GLA_DV_HEAD, GLA_DK_HEAD), F32),
                            pltpu.VMEM((N_GLA_HEADS, GLA_DV_HEAD, GLA_DK_HEAD), F32)],
        ),
        input_output_aliases={n_prefetch + len(in_specs) - 1: 2},
        compiler_params=_cparams(("arbitrary",)),
        name="gla",
    )(fwd_t, bwd_t, flag_t, seq_t, lr, lr, a2f, a2b, ab, ab,
      proj, proj, proj, proj, proj, proj, s0, s_all)


def _mixout_kernel(a_ref, of_ref, ob_ref, pr_ref, gn_ref, ga_ref, gb_ref, x_ref, g1_ref,
                   wa_ref, wb_ref, wo_ref, out_ref, o_ref):
    for h in range(N_GLA_HEADS):
        vc = slice(h * GLA_DV_HEAD, (h + 1) * GLA_DV_HEAD)
        o = of_ref[:, vc] + ob_ref[:, vc]
        o = o * lax.rsqrt(jnp.mean(o * o, axis=-1, keepdims=True) + EPS)
        o_ref[:, vc] = (o * gn_ref[:, vc] * _silu(pr_ref[:, vc].astype(F32))).astype(BF16)
    ya = jnp.dot(a_ref[...], wa_ref[...], preferred_element_type=F32)
    yb = jnp.dot(o_ref[...], wb_ref[...], preferred_element_type=F32)
    merged = (jax.nn.sigmoid(ga_ref[...].astype(F32)) * ya
              + jax.nn.sigmoid(gb_ref[...].astype(F32)) * yb).astype(BF16)
    y = jnp.dot(merged, wo_ref[...], preferred_element_type=F32)
    out_ref[...] = x_ref[...] + g1_ref[...] * y


def _resident_spec(shape, layer):
    zeros = (0,) * len(shape)
    return pl.BlockSpec((None,) + tuple(shape), lambda *_: (layer,) + zeros,
                        pipeline_mode=pl.Buffered(1))


def _mixout(a, o_f, o_b, proj, gla_g, x, mod, layer, w_a, w_b, w_o):
    tm = TM_MIX
    return pl.pallas_call(
        _mixout_kernel,
        out_shape=jax.ShapeDtypeStruct((T, D_MODEL), F32),
        grid=(T // tm,),
        in_specs=[
            pl.BlockSpec((tm, D_A), lambda m: (m, 0)),
            pl.BlockSpec((tm, GLA_DV), lambda m: (m, 0)),
            pl.BlockSpec((tm, GLA_DV), lambda m: (m, 0)),
            pl.BlockSpec((tm, GLA_DV), lambda m: (m, C_R // GLA_DV)),
            _layer_spec((1, GLA_DV), layer),
            pl.BlockSpec((tm, D_MODEL), lambda m: (m, C_GA // D_MODEL)),
            pl.BlockSpec((tm, D_MODEL), lambda m: (m, C_GB // D_MODEL)),
            pl.BlockSpec((tm, D_MODEL), lambda m: (m, 0)),
            _mod_spec(layer, 2, tm),
            _resident_spec((D_A, D_MODEL), layer),
            _resident_spec((GLA_DV, D_MODEL), layer),
            _resident_spec((D_MODEL, D_MODEL), layer),
        ],
        out_specs=pl.BlockSpec((tm, D_MODEL), lambda m: (m, 0)),
        scratch_shapes=[pltpu.VMEM((tm, GLA_DV), BF16)],
        compiler_params=_cparams(("parallel",)),
        name="mixout",
    )(a, o_f, o_b, proj, gla_g, proj, proj, x, mod, w_a, w_b, w_o)


def _ffn_kernel(x_ref, g_ref, sh_ref, sc_ref, g2_ref, gn_ref, shn_ref, scn_ref, w1_ref, w3_ref, w2_ref,
                o_ref, hn_ref, h_ref, *, nf):
    f = pl.program_id(1)

    @pl.when(f == 0)
    def _():
        _fill_modnorm(x_ref, g_ref[...], sc_ref[...], sh_ref[...], h_ref, TM_FFN)
        o_ref[...] = jnp.zeros_like(o_ref)

    _swiglu_accumulate(h_ref, w1_ref, w3_ref, w2_ref, o_ref, TM_FFN)

    @pl.when(f == nf - 1)
    def _():
        g2 = g2_ref[...]
        gain_n, shn = gn_ref[...] * (1.0 + scn_ref[...]), shn_ref[...]

        def body(i, carry):
            r = pl.multiple_of(i * SLAB, SLAB)
            xn = x_ref[pl.ds(r, SLAB), :] + g2 * o_ref[pl.ds(r, SLAB), :]
            o_ref[pl.ds(r, SLAB), :] = xn
            hn_ref[pl.ds(r, SLAB), :] = _modnorm(xn, gain_n, shn).astype(BF16)
            return carry
        lax.fori_loop(0, TM_FFN // SLAB, body, 0, unroll=SLAB_UNROLL)


def _ffn(x, norm2_g, norm1_g, mod, layer, j, w1, w3, w2):
    tm, tf = TM_FFN, TF_FFN
    nf = D_FF // tf
    return pl.pallas_call(
        functools.partial(_ffn_kernel, nf=nf),
        out_shape=(jax.ShapeDtypeStruct((T, D_MODEL), F32),
                   jax.ShapeDtypeStruct((T, D_MODEL), BF16)),
        grid=(T // tm, nf),
        in_specs=[
            pl.BlockSpec((tm, D_MODEL), lambda m, f: (m, 0)),
            _layer_spec((1, D_MODEL), layer),
            _mod_spec(layer, 3, tm),
            _mod_spec(layer, 4, tm),
            _mod_spec(layer, 5, tm),
            _layer_spec((1, D_MODEL), layer + 1),
            _mod_spec(layer + 1, 0, tm),
            _mod_spec(layer + 1, 1, tm),
            pl.BlockSpec((None, D_MODEL, tf), lambda m, f: (j, 0, f)),
            pl.BlockSpec((None, D_MODEL, tf), lambda m, f: (j, 0, f)),
            pl.BlockSpec((None, tf, D_MODEL), lambda m, f: (j, f, 0)),
        ],
        out_specs=(pl.BlockSpec((tm, D_MODEL), lambda m, f: (m, 0)),
                   pl.BlockSpec((tm, D_MODEL), lambda m, f: (m, 0))),
        scratch_shapes=[pltpu.VMEM((tm, D_MODEL), BF16)],
        compiler_params=_cparams(("parallel", "arbitrary")),
        name="ffn",
    )(x, norm2_g, mod, mod, mod, norm1_g, mod, mod, w1, w3, w2)


def _router_kernel(x_ref, g_ref, sh_ref, sc_ref, rwa_ref, rwb_ref, rb_ref, h_ref, info_ref, p_ref, cnt_ref,
                   carry_ref, hhi_ref, hlo_ref, upper_ref):
    i = pl.program_id(0)
    tm = TM_ROUTE

    @pl.when(i == 0)
    def _():
        carry_ref[...] = jnp.zeros_like(carry_ref)
        r = lax.broadcasted_iota(I32, (tm, tm), 0)
        c = lax.broadcasted_iota(I32, (tm, tm), 1)
        upper_ref[...] = (r <= c).astype(BF16)

    gain = g_ref[...] * (1.0 + sc_ref[...])
    sh = sh_ref[...]
    half = D_MODEL // 2

    def fill(k, carry):
        rows = pl.ds(pl.multiple_of(k * SLAB, SLAB), SLAB)
        y = _modnorm(x_ref[rows, :], gain, sh)
        hi = y.astype(BF16)
        hi32 = hi.astype(F32)
        bits = lax.bitcast_convert_type(hi32, jnp.uint32)
        h_ref[rows, :] = (bits[:, half:] & jnp.uint32(0xFFFF0000)) | (bits[:, :half] >> 16)
        hhi_ref[rows, :] = hi
        hlo_ref[rows, :] = (y - hi32).astype(BF16)
        return carry
    lax.fori_loop(0, tm // SLAB, fill, 0, unroll=SLAB_UNROLL)

    t = lax.dot_general(rwa_ref[...], hhi_ref[...], _NT, preferred_element_type=F32)
    u = lax.dot_general(rwb_ref[...], hlo_ref[...], _NT, preferred_element_type=F32)
    logits = t[0:N_EXPERTS] + t[N_EXPERTS:2 * N_EXPERTS] + u[0:N_EXPERTS] + rb_ref[:, 0:1]
    eidx = lax.broadcasted_iota(I32, (N_EXPERTS, tm), 0).astype(F32)
    none = float(N_EXPERTS)
    m1 = jnp.max(logits, axis=0, keepdims=True)
    i1 = jnp.min(jnp.where(logits == m1, eidx, none), axis=0, keepdims=True)
    rest = jnp.where(eidx == i1, -jnp.inf, logits)
    m2 = jnp.max(rest, axis=0, keepdims=True)
    i2 = jnp.min(jnp.where(rest == m2, eidx, none), axis=0, keepdims=True)
    e = jnp.exp(m2 - m1)
    den = 1.0 + e
    sel1 = eidx == i1
    sel2 = eidx == i2
    onehot = (sel1 | sel2).astype(BF16)
    cnt = jnp.dot(onehot, upper_ref[...], preferred_element_type=F32) + carry_ref[:, 0:1]
    r1 = jnp.sum(jnp.where(sel1, cnt, 0.0), axis=0, keepdims=True) - 1.0
    r2 = jnp.sum(jnp.where(sel2, cnt, 0.0), axis=0, keepdims=True) - 1.0
    info_ref[...] = jnp.zeros_like(info_ref)
    info_ref[0:1, :] = i1.astype(I32)
    info_ref[1:2, :] = i2.astype(I32)
    info_ref[2:3, :] = r1.astype(I32)
    info_ref[3:4, :] = r2.astype(I32)
    p_ref[...] = jnp.zeros_like(p_ref)
    p_ref[0:1, :] = 1.0 / den
    p_ref[1:2, :] = e / den
    total = cnt[:, tm - 1:tm]
    carry_ref[...] = jnp.broadcast_to(total, carry_ref.shape)
    cnt_ref[...] = jnp.broadcast_to(total, cnt_ref.shape).astype(I32)


def _router(x, norm_g, mod, layer, j, rw_a, rw_b, rb):
    tm = TM_ROUTE
    nt = T // tm
    return pl.pallas_call(
        _router_kernel,
        out_shape=(jax.ShapeDtypeStruct((T, D_PACK), U32),
                   jax.ShapeDtypeStruct((nt, 8, tm), I32),
                   jax.ShapeDtypeStruct((nt, 8, tm), F32),
                   jax.ShapeDtypeStruct((N_EXPERTS, 128), I32)),
        grid=(nt,),
        in_specs=[
            pl.BlockSpec((tm, D_MODEL), lambda i: (i, 0)),
            _layer_spec((1, D_MODEL), layer),
            _mod_spec(layer, 3, tm),
            _mod_spec(layer, 4, tm),
            _layer_spec((2 * N_EXPERTS, D_MODEL), j),
            _layer_spec((2 * N_EXPERTS, D_MODEL), j),
            _layer_spec((N_EXPERTS, 128), j),
        ],
        out_specs=(pl.BlockSpec((tm, D_PACK), lambda i: (i, 0)),
                   pl.BlockSpec((None, 8, tm), lambda i: (i, 0, 0)),
                   pl.BlockSpec((None, 8, tm), lambda i: (i, 0, 0)),
                   pl.BlockSpec((N_EXPERTS, 128), lambda i: (0, 0))),
        scratch_shapes=[pltpu.VMEM((N_EXPERTS, 128), F32), pltpu.VMEM((tm, D_MODEL), BF16),
                        pltpu.VMEM((tm, D_MODEL), BF16), pltpu.VMEM((tm, tm), BF16)],
        compiler_params=_cparams(("arbitrary",)),
        name="router",
    )(x, norm_g, mod, mod, rw_a, rw_b, rb)


def _dispatch_kernel(pos1_ref, pos2_ref, zstart_ref, zcount_ref, h_ref, o_hbm, zrow_ref, sem):
    i = pl.program_id(0)
    td = TD_DISP
    base = i * td

    def row_copy(src_row, dst_row):
        return pltpu.make_async_copy(h_ref.at[pl.ds(src_row, 1)], o_hbm.at[pl.ds(dst_row, 1)], sem)

    def zero_copy(dst_row):
        return pltpu.make_async_copy(zrow_ref, o_hbm.at[pl.ds(dst_row, 1)], sem)

    @pl.when(i == 0)
    def _():
        zrow_ref[...] = jnp.zeros_like(zrow_ref)
        for e in range(N_EXPERTS + 1):
            def zbody(r, carry):
                zero_copy(zstart_ref[e] + r).start()
                return carry
            lax.fori_loop(0, zcount_ref[e], zbody, 0)

            def zwait(r, carry):
                zero_copy(0).wait()
                return carry
            lax.fori_loop(0, zcount_ref[e], zwait, 0)

    for r in range(td):
        row_copy(r, pos1_ref[base + r]).start()
        row_copy(r, pos2_ref[base + r]).start()

    tile_copy = pltpu.make_async_copy(h_ref, o_hbm.at[pl.ds(0, td)], sem)
    tile_copy.wait()
    tile_copy.wait()


def _dispatch(pos1, pos2, zstart, zcount, h):
    td = TD_DISP
    return pl.pallas_call(
        _dispatch_kernel,
        out_shape=jax.ShapeDtypeStruct((ROWS_CAP, D_PACK), U32),
        grid_spec=pltpu.PrefetchScalarGridSpec(
            num_scalar_prefetch=4,
            grid=(T // td,),
            in_specs=[pl.BlockSpec((td, D_PACK), lambda i, *_: (i, 0))],
            out_specs=pl.BlockSpec(memory_space=pl.ANY),
            scratch_shapes=[pltpu.VMEM((1, D_PACK), U32), pltpu.SemaphoreType.DMA],
        ),
        compiler_params=_cparams(("arbitrary",)),
        name="dispatch",
    )(pos1, pos2, zstart, zcount, h)


def _expert_kernel(te_ref, nact_ref, hs_ref, w1_ref, w3_ref, w2_ref, y_ref, hb_ref):
    i = pl.program_id(0)
    f = pl.program_id(1)
    active = i < nact_ref[0]

    @pl.when(f == 0)
    def _():
        y_ref[...] = jnp.zeros_like(y_ref)

    @pl.when(active & (f == 0))
    def _():
        def body(s, carry):
            rows = pl.ds(pl.multiple_of(s * SLAB, SLAB), SLAB)
            w = hs_ref[rows, :]
            lo = lax.bitcast_convert_type(w << 16, F32)
            hi = lax.bitcast_convert_type(w & jnp.uint32(0xFFFF0000), F32)
            hb_ref[rows, 0:D_PACK] = lo.astype(BF16)
            hb_ref[rows, D_PACK:D_MODEL] = hi.astype(BF16)
            return carry
        lax.fori_loop(0, TR_EXP // SLAB, body, 0, unroll=SLAB_UNROLL)

    @pl.when(active)
    def _():
        _swiglu_accumulate(hb_ref, w1_ref, w3_ref, w2_ref, y_ref, TR_EXP)


def _experts(tile_expert, n_active, hs, j, w1, w3, w2):
    tr, tf = TR_EXP, TF_EXP
    nf = D_FF // tf

    def row_map(i, f, te, na):
        return (jnp.minimum(i, na[0] - 1), 0)

    def out_map(i, f, te, na):
        return (i, 0)

    def w13_map(i, f, te, na):
        return (j, te[i], 0, jnp.where(i < na[0], f, nf - 1))

    def w2_map(i, f, te, na):
        return (j, te[i], jnp.where(i < na[0], f, nf - 1), 0)

    return pl.pallas_call(
        _expert_kernel,
        out_shape=jax.ShapeDtypeStruct((ROWS_CAP, D_MODEL), F32),
        grid_spec=pltpu.PrefetchScalarGridSpec(
            num_scalar_prefetch=2,
            grid=(NT_EXP, nf),
            in_specs=[
                pl.BlockSpec((tr, D_PACK), row_map),
                pl.BlockSpec((None, None, D_MODEL, tf), w13_map),
                pl.BlockSpec((None, None, D_MODEL, tf), w13_map),
                pl.BlockSpec((None, None, tf, D_MODEL), w2_map),
            ],
            out_specs=pl.BlockSpec((tr, D_MODEL), out_map),
            scratch_shapes=[pltpu.VMEM((tr, D_MODEL), BF16)],
        ),
        compiler_params=_cparams(("arbitrary", "arbitrary")),
        name="experts",
    )(tile_expert, n_active, hs, w1, w3, w2)


def _gather_residual(pos1_ref, pos2_ref, x_ref, p1_ref, p2_ref, g2_ref, y_hbm, buf_ref, sem, emit):
    i = pl.program_id(0)
    n = pl.num_programs(0)
    tc = TC_COMB
    slot = i % 2

    def start_rows(tile, s, r0):
        base = tile * tc
        for d in range(COMB_SLAB):
            r = r0 + d
            pltpu.make_async_copy(y_hbm.at[pl.ds(pos1_ref[base + r], 1)],
                                  buf_ref.at[s, 0, pl.ds(r, 1)], sem.at[s]).start()
            pltpu.make_async_copy(y_hbm.at[pl.ds(pos2_ref[base + r], 1)],
                                  buf_ref.at[s, 1, pl.ds(r, 1)], sem.at[s]).start()

    @pl.when(i == 0)
    def _():
        def body(k, carry):
            start_rows(0, 0, k * COMB_SLAB)
            return carry
        lax.fori_loop(0, tc // COMB_SLAB, body, 0)

    pltpu.make_async_copy(y_hbm.at[pl.ds(0, tc)], buf_ref.at[slot, 0], sem.at[slot]).wait()
    pltpu.make_async_copy(y_hbm.at[pl.ds(0, tc)], buf_ref.at[slot, 1], sem.at[slot]).wait()
    g2 = g2_ref[...]

    def tile_pass(prefetch):
        for k in range(tc // COMB_SLAB):
            r0 = k * COMB_SLAB
            if prefetch:
                start_rows(i + 1, 1 - slot, r0)
            rows = slice(r0, r0 + COMB_SLAB)
            mix = (p1_ref[rows, :] * buf_ref[slot, 0, rows, :]
                   + p2_ref[rows, :] * buf_ref[slot, 1, rows, :])
            emit(rows, x_ref[rows, :] + g2 * mix)

    @pl.when(i + 1 < n)
    def _():
        tile_pass(True)

    @pl.when(i + 1 >= n)
    def _():
        tile_pass(False)


def _combine_kernel(pos1_ref, pos2_ref, x_ref, p1_ref, p2_ref, g2_ref, gn_ref, shn_ref, scn_ref, y_hbm,
                    o_ref, hn_ref, buf_ref, sem):
    gain_n, shn = gn_ref[...] * (1.0 + scn_ref[...]), shn_ref[...]

    def emit(rows, xn):
        o_ref[rows, :] = xn
        hn_ref[rows, :] = _modnorm(xn, gain_n, shn).astype(BF16)

    _gather_residual(pos1_ref, pos2_ref, x_ref, p1_ref, p2_ref, g2_ref, y_hbm, buf_ref, sem, emit)


def _combine_final_kernel(pos1_ref, pos2_ref, x_ref, p1_ref, p2_ref, g2_ref, gf_ref, y_hbm,
                          yp_ref, ys_ref, buf_ref, sem, yt_ref, *, n_ctx_tiles):
    gf = gf_ref[...]

    def emit(rows, xn):
        yt_ref[rows, :] = xn * lax.rsqrt(jnp.mean(xn * xn, axis=-1, keepdims=True) + EPS) * gf

    _gather_residual(pos1_ref, pos2_ref, x_ref, p1_ref, p2_ref, g2_ref, y_hbm, buf_ref, sem, emit)
    i = pl.program_id(0)

    @pl.when(i < n_ctx_tiles)
    def _():
        yp_ref[...] = yt_ref[...]

    @pl.when(i >= n_ctx_tiles)
    def _():
        ys_ref[...] = yt_ref[...]


def _combine_call(body, extra_specs, out_shape, out_specs, name, args, extra_scratch=()):
    tc = TC_COMB
    row_tile = pl.BlockSpec((tc, D_MODEL), lambda i, *_: (i, 0))
    col = pl.BlockSpec((tc, 1), lambda i, *_: (i, 0))
    return pl.pallas_call(
        body,
        out_shape=out_shape,
        grid_spec=pltpu.PrefetchScalarGridSpec(
            num_scalar_prefetch=2,
            grid=(T // tc,),
            in_specs=[row_tile, col, col] + extra_specs + [pl.BlockSpec(memory_space=pl.ANY)],
            out_specs=out_specs,
            scratch_shapes=[pltpu.VMEM((2, 2, tc, D_MODEL), F32), pltpu.SemaphoreType.DMA((2,))]
            + list(extra_scratch),
        ),
        compiler_params=_cparams(("arbitrary",)),
        name=name,
    )(*args)


def _combine(pos1, pos2, x, p1, p2, mod, layer, norm1_g, y):
    tc = TC_COMB
    row_tile = pl.BlockSpec((tc, D_MODEL), lambda i, *_: (i, 0))
    return _combine_call(
        _combine_kernel,
        [_mod_spec(layer, 5, tc), _layer_spec((1, D_MODEL), layer + 1),
         _mod_spec(layer + 1, 0, tc), _mod_spec(layer + 1, 1, tc)],
        (jax.ShapeDtypeStruct((T, D_MODEL), F32), jax.ShapeDtypeStruct((T, D_MODEL), BF16)),
        (row_tile, row_tile), "combine",
        (pos1, pos2, x, p1, p2, mod, norm1_g, mod, mod, y))


def _combine_final(pos1, pos2, x, p1, p2, mod, layer, final_g, y):
    tc = TC_COMB
    nct = TP // tc
    return _combine_call(
        functools.partial(_combine_final_kernel, n_ctx_tiles=nct),
        [_mod_spec(layer, 5, tc), pl.BlockSpec((1, D_MODEL), lambda i, *_: (0, 0))],
        (jax.ShapeDtypeStruct((TP, D_MODEL), F32), jax.ShapeDtypeStruct((TS, D_MODEL), F32)),
        (pl.BlockSpec((tc, D_MODEL), lambda i, *_: (jnp.minimum(i, nct - 1), 0)),
         pl.BlockSpec((tc, D_MODEL), lambda i, *_: (jnp.maximum(i - nct, 0), 0))),
        "combine_final",
        (pos1, pos2, x, p1, p2, mod, final_g, y),
        extra_scratch=[pltpu.VMEM((tc, D_MODEL), F32)])


def _moe(x, norm_g, mod, layer, j, rw, rb, w1, w3, w2, norm1_g, final_g):
    h, info, probs, counts = _router(x, norm_g, mod, layer, j, rw[0], rw[1], rb)
    counts = counts[:, 0]
    cap = ((counts + TR_EXP - 1) // TR_EXP) * TR_EXP
    ends = jnp.cumsum(cap)
    starts = ends - cap
    e1 = info[:, 0, :].reshape(T)
    e2 = info[:, 1, :].reshape(T)
    pos1 = starts[e1] + info[:, 2, :].reshape(T)
    pos2 = starts[e2] + info[:, 3, :].reshape(T)
    tile_end = ends // TR_EXP
    n_active = tile_end[-1:]
    tiles = jnp.minimum(jnp.arange(NT_EXP, dtype=I32), n_active[0] - 1)
    tile_expert = jnp.sum((tiles[:, None] >= tile_end[None, :]).astype(I32), axis=1)
    zstart = jnp.concatenate([starts + counts, ends[-1:]])
    zcount = jnp.concatenate([cap - counts, ROWS_CAP - ends[-1:]])
    hs = _dispatch(pos1, pos2, zstart, zcount, h)
    y = _experts(tile_expert, n_active, hs, j, w1, w3, w2)
    p1 = probs[:, 0, :].reshape(T, 1)
    p2 = probs[:, 1, :].reshape(T, 1)
    if layer == DEPTH - 1:
        return _combine_final(pos1, pos2, x, p1, p2, mod, layer, final_g, y)
    return _combine(pos1, pos2, x, p1, p2, mod, layer, norm1_g, y)


def _grid_pos_embed():
    rows = DEC_SEQ // GRID_W
    r, col = jnp.meshgrid(jnp.arange(rows, dtype=F32), jnp.arange(GRID_W, dtype=F32), indexing='ij')
    r = r.reshape(-1)
    col = col.reshape(-1)
    quarter = D_MODEL // 4
    freq = jnp.exp(-math.log(10000.0) * jnp.arange(quarter, dtype=F32) / quarter)
    ar = r[:, None] * freq
    ac = col[:, None] * freq
    return jnp.concatenate([jnp.sin(ar), jnp.cos(ar), jnp.sin(ac), jnp.cos(ac)], axis=-1)


def kernel(x_prompt, x_sample, state_gla, c, c_ctx, norm1_g, norm2_g, w_mod, b_mod, w_in,
           sgu_ln_g, sgu_ln_b, w_spatial, b_spatial, gla_a2, gla_ab, gla_norm_g,
           w_branch_a, w_branch_b, w_out, ffn_w1, ffn_w3, ffn_w2,
           moe_router, moe_router_b, moe_w1, moe_w3, moe_w2, final_g):
    cond = jnp.zeros((COND_PAD, D_MODEL), F32).at[0].set(c_ctx).at[1:N_COND].set(c)
    mod = _mod_table(cond, w_mod, b_mod).reshape(DEPTH, COND_PAD, 6, 1, D_MODEL)

    lr_end = W_IN_LR + 2 * GLA_RANK
    w_main = jnp.concatenate([w_in[:, :, lr_end:], w_in[:, :, :W_IN_LR]], axis=2).astype(BF16)
    w_lr = jnp.pad(w_in[:, :, W_IN_LR:lr_end], ((0, 0), (0, 0), (0, LR_PAD - 2 * GLA_RANK))).astype(BF16)
    norm1 = norm1_g.reshape(DEPTH, 1, D_MODEL)
    norm2 = norm2_g.reshape(DEPTH, 1, D_MODEL)
    ln_g = sgu_ln_g.reshape(DEPTH, 1, D_A)
    ln_b = sgu_ln_b.reshape(DEPTH, 1, D_A)
    w_s = w_spatial.astype(BF16)
    bs_full = jnp.repeat(jnp.swapaxes(b_spatial, 1, 2), A_GROUP, axis=2)
    a2_pad = jnp.zeros((DEPTH, LR_PAD, GLA_DK), F32)
    a2f = a2_pad.at[:, :GLA_RANK].set(gla_a2[:, 0]).astype(BF16)
    a2b = a2_pad.at[:, GLA_RANK:2 * GLA_RANK].set(gla_a2[:, 1]).astype(BF16)
    ab = gla_ab.reshape(DEPTH, 2, 1, GLA_DK)
    gla_g = gla_norm_g.reshape(DEPTH, 1, GLA_DV)
    w_a = w_branch_a.astype(BF16)
    w_b = w_branch_b.astype(BF16)
    w_o = w_out.astype(BF16)
    f_w1, f_w3, f_w2 = ffn_w1.astype(BF16), ffn_w3.astype(BF16), ffn_w2.astype(BF16)
    m_w1, m_w3, m_w2 = moe_w1.astype(BF16), moe_w3.astype(BF16), moe_w2.astype(BF16)
    rw_t = jnp.swapaxes(moe_router, 1, 2)
    rw_hi = rw_t.astype(BF16)
    rw_lo = (rw_t - rw_hi.astype(F32)).astype(BF16)
    rw = (jnp.concatenate([rw_hi, rw_lo], axis=1),
          jnp.concatenate([rw_hi, jnp.zeros_like(rw_hi)], axis=1))
    rb = jnp.broadcast_to(moe_router_b[:, :, None], moe_router_b.shape + (128,))

    x, h = _embed(x_prompt.reshape(TP, D_MODEL), x_sample.reshape(TS, D_MODEL), _grid_pos_embed(),
                  norm1, mod)
    new_state = jnp.zeros((BATCH, DEPTH, 2, N_GLA_HEADS, GLA_DK_HEAD, GLA_DV_HEAD), F32)
    for l in range(DEPTH):
        proj, lr = _proj(h, l, w_main, w_lr)
        a = _sgu(proj, l, ln_g, ln_b, w_s, bs_full)
        o_f, o_b, new_state = _gla(proj, lr, a2f, a2b, ab, state_gla, new_state, l)
        x = _mixout(a, o_f, o_b, proj, gla_g, x, mod, l, w_a, w_b, w_o)
        j = l // 2
        if l % 2 == 0:
            x, h = _ffn(x, norm2, norm1, mod, l, j, f_w1, f_w3, f_w2)
        elif l < DEPTH - 1:
            x, h = _moe(x, norm2, mod, l, j, rw, rb, m_w1, m_w3, m_w2, norm1, final_g[None])
        else:
            y_prompt, y_sample = _moe(x, norm2, mod, l, j, rw, rb, m_w1, m_w3, m_w2, norm1,
                                      final_g[None])

    return (y_prompt.reshape(BATCH, SEQ, D_MODEL), y_sample.reshape(DEC_BATCH, DEC_SEQ, D_MODEL),
            new_state)
```

```python
import functools
import math

import numpy as np
import jax
import jax.numpy as jnp
from jax import lax
from jax.experimental import pallas as pl
from jax.experimental.pallas import tpu as pltpu

F32 = jnp.float32
BF16 = jnp.bfloat16
I32 = jnp.int32

D_MODEL = 2048
BATCH = 32
SEQ = 256
DEPTH = 4
DEC_BATCH = 8
DEC_SEQ = 2048
GRID_W = 64
CHUNK_A = 128
D_A = 1024
N_A_GROUPS = 8
A_GROUP = D_A // N_A_GROUPS
N_GLA_HEADS = 4
GLA_DK = D_MODEL // 4
GLA_DV = D_MODEL // 2
GLA_DK_HEAD = GLA_DK // N_GLA_HEADS
GLA_DV_HEAD = GLA_DV // N_GLA_HEADS
GLA_RANK = 16
GLA_TAU = 16.0
GLA_CHUNK = 64
D_FF = 5632
N_EXPERTS = 8
TOP_K = 2
EPS = 1e-6

TP = BATCH * SEQ
TS = DEC_BATCH * DEC_SEQ
T = TP + TS
N_COND = 1 + DEC_BATCH
COND_PAD = 16

W_IN_LR = 2 * D_A + 2 * GLA_DK + 2 * GLA_DV
C_GA = 0
C_GB = C_GA + D_MODEL
C_U = C_GB + D_MODEL
C_V = C_U + D_A
C_Q = C_V + D_A
C_K = C_Q + GLA_DK
C_VG = C_K + GLA_DK
C_R = C_VG + GLA_DV
N_MAIN = C_R + GLA_DV
LR_PAD = 128

VMEM_LIMIT = 56 * 1024 * 1024

TM_PROJ, TN_PROJ = 1024, 2304
TM_EMBED = 512
TM_SGU = 512
TB_GLA = 256
TM_MIX = 256
TM_FFN, TF_FFN = 512, 512
TM_ROUTE = 512
TD_DISP = 256
TR_EXP, TF_EXP = 1024, 256
TC_COMB = 256
SLAB = 32
SLAB_UNROLL = 4
RS_SWIGLU = 512
COMB_SLAB = 16

NT_EXP = (TOP_K * T) // TR_EXP + N_EXPERTS
ROWS_CAP = NT_EXP * TR_EXP


def _cparams(sem):
    return pltpu.CompilerParams(dimension_semantics=sem, vmem_limit_bytes=VMEM_LIMIT)


def _mod_row(i, tm):
    return jnp.maximum((i * tm) // DEC_SEQ - (TP // DEC_SEQ - 1), 0)


def _mod_spec(layer, j, tm, width=D_MODEL):
    return pl.BlockSpec((None, None, None, 1, width),
                        lambda m, *_: (layer, _mod_row(m, tm), j, 0, 0))


def _layer_spec(shape, layer):
    zeros = (0,) * len(shape)
    return pl.BlockSpec((None,) + tuple(shape), lambda *_: (layer,) + zeros)


def _modnorm(x, gain, sh):
    return x * lax.rsqrt(jnp.mean(x * x, axis=-1, keepdims=True) + EPS) * gain + sh


def _fill_modnorm(x_ref, g, sc, sh, h_ref, rows):
    gain = g * (1.0 + sc)

    def body(i, carry):
        r = pl.multiple_of(i * SLAB, SLAB)
        h_ref[pl.ds(r, SLAB), :] = _modnorm(x_ref[pl.ds(r, SLAB), :], gain, sh).astype(h_ref.dtype)
        return carry
    lax.fori_loop(0, rows // SLAB, body, 0, unroll=SLAB_UNROLL)


def _silu(x):
    return x * jax.nn.sigmoid(x)


def _gelu_tanh(x):
    return jax.nn.gelu(x, approximate=True)


def _log_sigmoid(x):
    return jnp.minimum(x, 0.0) - jnp.log(1.0 + jnp.exp(-jnp.abs(x)))


def _swiglu_accumulate(h_ref, w1_ref, w3_ref, w2_ref, acc_ref, rows):
    for r in range(rows // RS_SWIGLU):
        sl = slice(r * RS_SWIGLU, (r + 1) * RS_SWIGLU)
        h = h_ref[sl, :]
        u = _silu(jnp.dot(h, w1_ref[...], preferred_element_type=F32)) * jnp.dot(
            h, w3_ref[...], preferred_element_type=F32)
        acc_ref[sl, :] += jnp.dot(u.astype(BF16), w2_ref[...], preferred_element_type=F32)


def _embed_kernel(xp_ref, xs_ref, emb_ref, g_ref, sh_ref, sc_ref, o_ref, h_ref, *, n_ctx_tiles):
    i = pl.program_id(0)

    @pl.when(i < n_ctx_tiles)
    def _():
        o_ref[...] = xp_ref[...]

    @pl.when(i >= n_ctx_tiles)
    def _():
        o_ref[...] = xs_ref[...] + emb_ref[...]

    _fill_modnorm(o_ref, g_ref[...], sc_ref[...], sh_ref[...], h_ref, TM_EMBED)


def _embed(xp, xs, emb, norm_g, mod):
    tm = TM_EMBED
    nct = TP // tm
    per_seq = DEC_SEQ // tm
    return pl.pallas_call(
        functools.partial(_embed_kernel, n_ctx_tiles=nct),
        out_shape=(jax.ShapeDtypeStruct((T, D_MODEL), F32),
                   jax.ShapeDtypeStruct((T, D_MODEL), BF16)),
        grid=(T // tm,),
        in_specs=[
            pl.BlockSpec((tm, D_MODEL), lambda i: (jnp.minimum(i, nct - 1), 0)),
            pl.BlockSpec((tm, D_MODEL), lambda i: (jnp.maximum(i - nct, 0), 0)),
            pl.BlockSpec((tm, D_MODEL), lambda i: (jnp.maximum(i - nct, 0) % per_seq, 0)),
            _layer_spec((1, D_MODEL), 0),
            _mod_spec(0, 0, tm),
            _mod_spec(0, 1, tm),
        ],
        out_specs=(pl.BlockSpec((tm, D_MODEL), lambda i: (i, 0)),
                   pl.BlockSpec((tm, D_MODEL), lambda i: (i, 0))),
        compiler_params=_cparams(("parallel",)),
        name="embed",
    )(xp, xs, emb, norm_g, mod, mod)


def _mod_kernel(c_ref, w_ref, b_ref, o_ref):
    c = c_ref[...]
    s = _silu(c).astype(BF16)
    o_ref[...] = jnp.dot(s, w_ref[...].astype(BF16), preferred_element_type=F32) + b_ref[...]


def _mod_table(cond, w_mod, b_mod):
    tn = 1024
    n6 = 6 * D_MODEL
    return pl.pallas_call(
        _mod_kernel,
        out_shape=jax.ShapeDtypeStruct((DEPTH, COND_PAD, n6), F32),
        grid=(DEPTH, n6 // tn),
        in_specs=[
            pl.BlockSpec((COND_PAD, D_MODEL), lambda l, n: (0, 0)),
            pl.BlockSpec((None, D_MODEL, tn), lambda l, n: (l, 0, n)),
            pl.BlockSpec((None, 1, tn), lambda l, n: (l, 0, n)),
        ],
        out_specs=pl.BlockSpec((None, COND_PAD, tn), lambda l, n: (l, 0, n)),
        compiler_params=_cparams(("parallel", "parallel")),
        name="mod_table",
    )(cond, w_mod, b_mod.reshape(DEPTH, 1, n6))


def _proj_kernel(h_ref, w_ref, wlr_ref, o_ref, lr_ref):
    @pl.when(pl.program_id(1) == 0)
    def _():
        lr_ref[...] = jnp.dot(h_ref[...], wlr_ref[...], preferred_element_type=F32)

    o_ref[...] = jnp.dot(h_ref[...], w_ref[...], preferred_element_type=F32).astype(o_ref.dtype)


def _proj(h, layer, w_main, w_lr):
    tm, tn = TM_PROJ, TN_PROJ
    return pl.pallas_call(
        _proj_kernel,
        out_shape=(jax.ShapeDtypeStruct((T, N_MAIN), BF16),
                   jax.ShapeDtypeStruct((T, LR_PAD), F32)),
        grid=(T // tm, N_MAIN // tn),
        in_specs=[
            pl.BlockSpec((tm, D_MODEL), lambda m, n: (m, 0)),
            pl.BlockSpec((None, D_MODEL, tn), lambda m, n: (layer, 0, n)),
            _layer_spec((D_MODEL, LR_PAD), layer),
        ],
        out_specs=(pl.BlockSpec((tm, tn), lambda m, n: (m, n)),
                   pl.BlockSpec((tm, LR_PAD), lambda m, n: (m, 0))),
        compiler_params=_cparams(("parallel", "arbitrary")),
        name="proj",
    )(h, w_main, w_lr)


def _sgu_kernel(pu_ref, pv_ref, lng_ref, lnb_ref, ws_ref, bs_ref, a_ref, vn_ref):
    gv = _gelu_tanh(pv_ref[...].astype(F32))
    mu = jnp.mean(gv, axis=-1, keepdims=True)
    d = gv - mu
    var = jnp.mean(d * d, axis=-1, keepdims=True)
    vn_ref[...] = (d * lax.rsqrt(var + EPS) * lng_ref[...] + lnb_ref[...]).astype(BF16)
    for c in range(TM_SGU // CHUNK_A):
        rows = slice(c * CHUNK_A, (c + 1) * CHUNK_A)
        for g in range(N_A_GROUPS):
            cols = slice(g * A_GROUP, (g + 1) * A_GROUP)
            f = jnp.dot(ws_ref[g], vn_ref[rows, cols], preferred_element_type=F32) + bs_ref[:, cols]
            gu = _gelu_tanh(pu_ref[rows, cols].astype(F32))
            a_ref[rows, cols] = (gu * f).astype(BF16)


def _sgu(proj, layer, ln_g, ln_b, w_s, bs_full):
    tm = TM_SGU
    return pl.pallas_call(
        _sgu_kernel,
        out_shape=jax.ShapeDtypeStruct((T, D_A), BF16),
        grid=(T // tm,),
        in_specs=[
            pl.BlockSpec((tm, D_A), lambda i: (i, C_U // D_A)),
            pl.BlockSpec((tm, D_A), lambda i: (i, C_V // D_A)),
            _layer_spec((1, D_A), layer),
            _layer_spec((1, D_A), layer),
            _layer_spec((N_A_GROUPS, CHUNK_A, CHUNK_A), layer),
            _layer_spec((CHUNK_A, D_A), layer),
        ],
        out_specs=pl.BlockSpec((tm, D_A), lambda i: (i, 0)),
        scratch_shapes=[pltpu.VMEM((tm, D_A), BF16)],
        compiler_params=_cparams(("parallel",)),
        name="sgu",
    )(proj, proj, ln_g, ln_b, w_s, bs_full)


def _split3(x):
    hi = x.astype(BF16)
    r = x - hi.astype(F32)
    mid = r.astype(BF16)
    lo = (r - mid.astype(F32)).astype(BF16)
    return hi, mid, lo


def _tri_cumsum(tri, x):
    hi, mid, lo = _split3(x)
    return (jnp.dot(tri, hi, preferred_element_type=F32)
            + jnp.dot(tri, mid, preferred_element_type=F32)
            + jnp.dot(tri, lo, preferred_element_type=F32))


_NT = (((1,), (1,)), ((), ()))
_TN = (((0,), (0,)), ((), ()))


def _gla_decay(lr_ref, a2_ref, ab_ref, tb, backward):
    z = jnp.dot(lr_ref[...].astype(BF16), a2_ref[...], preferred_element_type=F32) + ab_ref[...]
    la = _log_sigmoid(z) / GLA_TAU
    r = lax.broadcasted_iota(I32, (tb, tb), 0)
    c = lax.broadcasted_iota(I32, (tb, tb), 1)
    same = (r // GLA_CHUNK) == (c // GLA_CHUNK)
    tri = (same & ((c >= r) if backward else (c <= r))).astype(BF16)
    return _tri_cumsum(tri, la)


def _gla_block(dirs, tb):
    nchunk = tb // GLA_CHUNK
    scale = GLA_DK_HEAD ** -0.5
    rr = lax.broadcasted_iota(I32, (GLA_CHUNK, GLA_CHUNK), 0)
    cc = lax.broadcasted_iota(I32, (GLA_CHUNK, GLA_CHUNK), 1)
    decays = [_gla_decay(d[0], d[1], d[2], tb, d[8]) for d in dirs]
    per_dir = []
    for (lr_ref, a2_ref, ab_ref, q_ref, k_ref, v_ref, o_ref, st_ref, backward), b in zip(dirs, decays):
        units = []
        per_dir.append(units)
        eb = jnp.exp(b)
        enb = jnp.exp(-b)
        mask = (cc >= rr) if backward else (cc <= rr)
        order = range(nchunk - 1, -1, -1) if backward else range(nchunk)
        for ci in order:
            rows = slice(ci * GLA_CHUNK, (ci + 1) * GLA_CHUNK)
            last = ci * GLA_CHUNK if backward else (ci + 1) * GLA_CHUNK - 1
            btot = b[last:last + 1, :]
            eke = jnp.exp(btot - b[rows, :])
            gdec = jnp.exp(btot)
            for h in range(N_GLA_HEADS):
                kc = slice(h * GLA_DK_HEAD, (h + 1) * GLA_DK_HEAD)
                vc = slice(h * GLA_DV_HEAD, (h + 1) * GLA_DV_HEAD)
                q = q_ref[rows, kc].astype(F32) * scale
                k = k_ref[rows, kc].astype(F32)
                qd = (q * eb[rows, kc]).astype(BF16)
                kd = (k * enb[rows, kc]).astype(BF16)
                ke = (k * eke[:, kc]).astype(BF16)
                vv = v_ref[rows, vc]
                att = lax.dot_general(qd, kd, _NT, preferred_element_type=F32)
                att = jnp.where(mask, att, 0.0).astype(BF16)
                dst = lax.dot_general(vv, ke, _TN, preferred_element_type=F32)
                units.append(dict(o_ref=o_ref, st_ref=st_ref, rows=rows, vc=vc, h=h, qd=qd, vv=vv,
                                  att=att, dst=dst, gdec=gdec[:, kc]))
    units = [u for group in zip(*per_dir) for u in group]
    for u in units:
        u["o_intra"] = jnp.dot(u["att"], u["vv"], preferred_element_type=F32)
    states = {}
    for u in units:
        key = (id(u["st_ref"]), u["h"])
        st = states[key] if key in states else u["st_ref"][u["h"]]
        u["o_ref"][u["rows"], u["vc"]] = u["o_intra"] + lax.dot_general(
            u["qd"], st.astype(BF16), _NT, preferred_element_type=F32)
        states[key] = st * u["gdec"] + u["dst"]
    for (_, _, _, _, _, _, _, st_ref, _) in dirs:
        for h in range(N_GLA_HEADS):
            st_ref[h] = states[(id(st_ref), h)]


def _gla_kernel(fwd_ref, bwd_ref, flag_ref, seq_ref,
                lrf_ref, lrb_ref, a2f_ref, a2b_ref, abf_ref, abb_ref,
                qf_ref, kf_ref, vf_ref, qb_ref, kb_ref, vb_ref, s0_ref, sprev_ref,
                of_ref, ob_ref, sfin_ref, stf_ref, stb_ref):
    del sprev_ref
    i = pl.program_id(0)
    first = (flag_ref[i] & 1) == 1
    last = (flag_ref[i] & 2) == 2
    is_ctx = seq_ref[i] < BATCH

    @pl.when(first & is_ctx)
    def _():
        stf_ref[...] = jnp.zeros_like(stf_ref)
        stb_ref[...] = jnp.zeros_like(stb_ref)

    @pl.when(first & jnp.logical_not(is_ctx))
    def _():
        for h in range(N_GLA_HEADS):
            stf_ref[h] = s0_ref[0, h].T
            stb_ref[h] = s0_ref[1, h].T

    _gla_block([(lrf_ref, a2f_ref, abf_ref, qf_ref, kf_ref, vf_ref, of_ref, stf_ref, False),
                (lrb_ref, a2b_ref, abb_ref, qb_ref, kb_ref, vb_ref, ob_ref, stb_ref, True)], TB_GLA)

    @pl.when(last & is_ctx)
    def _():
        for h in range(N_GLA_HEADS):
            sfin_ref[0, h] = stf_ref[h].T
            sfin_ref[1, h] = stb_ref[h].T


def _gla_tables():
    tb = TB_GLA
    fwd, bwd, flag, seq = [], [], [], []
    for s, (row0, seqlen) in enumerate([(b * SEQ, SEQ) for b in range(BATCH)]
                                       + [(TP + b * DEC_SEQ, DEC_SEQ) for b in range(DEC_BATCH)]):
        nb = seqlen // tb
        for j in range(nb):
            fwd.append(row0 // tb + j)
            bwd.append(row0 // tb + nb - 1 - j)
            flag.append((1 if j == 0 else 0) | (2 if j == nb - 1 else 0))
            seq.append(s)
    return tuple(np.asarray(t, np.int32) for t in (fwd, bwd, flag, seq))


def _gla(proj, lr, a2f, a2b, ab, s0, s_all, layer):
    tb = TB_GLA
    fwd_t, bwd_t, flag_t, seq_t = _gla_tables()
    nsteps = fwd_t.shape[0]
    n_prefetch = 4

    def fwd(i, fw, bw, fl, sq):
        return fw[i]

    def bwd(i, fw, bw, fl, sq):
        return bw[i]

    def col_spec(rowfn, width, col):
        return pl.BlockSpec((tb, width), lambda i, *t: (rowfn(i, *t), col // width))

    state_block = (2, N_GLA_HEADS, GLA_DK_HEAD, GLA_DV_HEAD)
    in_specs = [
        pl.BlockSpec((tb, LR_PAD), lambda i, *t: (fwd(i, *t), 0)),
        pl.BlockSpec((tb, LR_PAD), lambda i, *t: (bwd(i, *t), 0)),
        _layer_spec((LR_PAD, GLA_DK), layer),
        _layer_spec((LR_PAD, GLA_DK), layer),
        pl.BlockSpec((None, None, 1, GLA_DK), lambda i, *t: (layer, 0, 0, 0)),
        pl.BlockSpec((None, None, 1, GLA_DK), lambda i, *t: (layer, 1, 0, 0)),
        col_spec(fwd, GLA_DK, C_Q), col_spec(fwd, GLA_DK, C_K), col_spec(fwd, GLA_DV, C_VG),
        col_spec(bwd, GLA_DK, C_Q), col_spec(bwd, GLA_DK, C_K), col_spec(bwd, GLA_DV, C_VG),
        pl.BlockSpec((None, None) + state_block,
                     lambda i, fw, bw, fl, sq: (jnp.maximum(sq[i] - BATCH, 0), layer, 0, 0, 0, 0)),
        pl.BlockSpec(memory_space=pl.ANY),
    ]
    out_shape = (jax.ShapeDtypeStruct((T, GLA_DV), F32),
                 jax.ShapeDtypeStruct((T, GLA_DV), F32),
                 jax.ShapeDtypeStruct((BATCH, DEPTH) + state_block, F32))
    out_specs = (pl.BlockSpec((tb, GLA_DV), lambda i, *t: (fwd(i, *t), 0)),
                 pl.BlockSpec((tb, GLA_DV), lambda i, *t: (bwd(i, *t), 0)),
                 pl.BlockSpec((None, None) + state_block,
                              lambda i, fw, bw, fl, sq: (jnp.minimum(sq[i], BATCH - 1), layer, 0, 0, 0, 0)))
    return pl.pallas_call(
        _gla_kernel,
        out_shape=out_shape,
        grid_spec=pltpu.PrefetchScalarGridSpec(
            num_scalar_prefetch=n_prefetch,
            grid=(nsteps,),
            in_specs=in_specs,
            out_specs=out_specs,
            scratch_shapes=[pltpu.VMEM((N_GLA_HEADS, GLA_DV_HEAD, GLA_DK_HEAD), F32),
                            pltpu.VMEM((N_GLA_HEADS, GLA_DV_HEAD, GLA_DK_HEAD), F32)],
        ),
        input_output_aliases={n_prefetch + len(in_specs) - 1: 2},
        compiler_params=_cparams(("arbitrary",)),
        name="gla",
    )(fwd_t, bwd_t, flag_t, seq_t, lr, lr, a2f, a2b, ab, ab,
      proj, proj, proj, proj, proj, proj, s0, s_all)


def _mixout_kernel(a_ref, of_ref, ob_ref, pr_ref, gn_ref, ga_ref, gb_ref, x_ref, g1_ref,
                   wa_ref, wb_ref, wo_ref, out_ref, o_ref):
    for h in range(N_GLA_HEADS):
        vc = slice(h * GLA_DV_HEAD, (h + 1) * GLA_DV_HEAD)
        o = of_ref[:, vc] + ob_ref[:, vc]
        o = o * lax.rsqrt(jnp.mean(o * o, axis=-1, keepdims=True) + EPS)
        o_ref[:, vc] = (o * gn_ref[:, vc] * _silu(pr_ref[:, vc].astype(F32))).astype(BF16)
    ya = jnp.dot(a_ref[...], wa_ref[...], preferred_element_type=F32)
    yb = jnp.dot(o_ref[...], wb_ref[...], preferred_element_type=F32)
    merged = (jax.nn.sigmoid(ga_ref[...].astype(F32)) * ya
              + jax.nn.sigmoid(gb_ref[...].astype(F32)) * yb).astype(BF16)
    y = jnp.dot(merged, wo_ref[...], preferred_element_type=F32)
    out_ref[...] = x_ref[...] + g1_ref[...] * y


def _resident_spec(shape, layer):
    zeros = (0,) * len(shape)
    return pl.BlockSpec((None,) + tuple(shape), lambda *_: (layer,) + zeros,
                        pipeline_mode=pl.Buffered(1))


def _mixout(a, o_f, o_b, proj, gla_g, x, mod, layer, w_a, w_b, w_o):
    tm = TM_MIX
    return pl.pallas_call(
        _mixout_kernel,
        out_shape=jax.ShapeDtypeStruct((T, D_MODEL), F32),
        grid=(T // tm,),
        in_specs=[
            pl.BlockSpec((tm, D_A), lambda m: (m, 0)),
            pl.BlockSpec((tm, GLA_DV), lambda m: (m, 0)),
            pl.BlockSpec((tm, GLA_DV), lambda m: (m, 0)),
            pl.BlockSpec((tm, GLA_DV), lambda m: (m, C_R // GLA_DV)),
            _layer_spec((1, GLA_DV), layer),
            pl.BlockSpec((tm, D_MODEL), lambda m: (m, C_GA // D_MODEL)),
            pl.BlockSpec((tm, D_MODEL), lambda m: (m, C_GB // D_MODEL)),
            pl.BlockSpec((tm, D_MODEL), lambda m: (m, 0)),
            _mod_spec(layer, 2, tm),
            _resident_spec((D_A, D_MODEL), layer),
            _resident_spec((GLA_DV, D_MODEL), layer),
            _resident_spec((D_MODEL, D_MODEL), layer),
        ],
        out_specs=pl.BlockSpec((tm, D_MODEL), lambda m: (m, 0)),
        scratch_shapes=[pltpu.VMEM((tm, GLA_DV), BF16)],
        compiler_params=_cparams(("parallel",)),
        name="mixout",
    )(a, o_f, o_b, proj, gla_g, proj, proj, x, mod, w_a, w_b, w_o)


def _ffn_kernel(x_ref, g_ref, sh_ref, sc_ref, g2_ref, gn_ref, shn_ref, scn_ref, w1_ref, w3_ref, w2_ref,
                o_ref, hn_ref, h_ref, *, nf):
    f = pl.program_id(1)

    @pl.when(f == 0)
    def _():
        _fill_modnorm(x_ref, g_ref[...], sc_ref[...], sh_ref[...], h_ref, TM_FFN)
        o_ref[...] = jnp.zeros_like(o_ref)

    _swiglu_accumulate(h_ref, w1_ref, w3_ref, w2_ref, o_ref, TM_FFN)

    @pl.when(f == nf - 1)
    def _():
        g2 = g2_ref[...]
        gain_n, shn = gn_ref[...] * (1.0 + scn_ref[...]), shn_ref[...]

        def body(i, carry):
            r = pl.multiple_of(i * SLAB, SLAB)
            xn = x_ref[pl.ds(r, SLAB), :] + g2 * o_ref[pl.ds(r, SLAB), :]
            o_ref[pl.ds(r, SLAB), :] = xn
            hn_ref[pl.ds(r, SLAB), :] = _modnorm(xn, gain_n, shn).astype(BF16)
            return carry
        lax.fori_loop(0, TM_FFN // SLAB, body, 0, unroll=SLAB_UNROLL)


def _ffn(x, norm2_g, norm1_g, mod, layer, j, w1, w3, w2):
    tm, tf = TM_FFN, TF_FFN
    nf = D_FF // tf
    return pl.pallas_call(
        functools.partial(_ffn_kernel, nf=nf),
        out_shape=(jax.ShapeDtypeStruct((T, D_MODEL), F32),
                   jax.ShapeDtypeStruct((T, D_MODEL), BF16)),
        grid=(T // tm, nf),
        in_specs=[
            pl.BlockSpec((tm, D_MODEL), lambda m, f: (m, 0)),
            _layer_spec((1, D_MODEL), layer),
            _mod_spec(layer, 3, tm),
            _mod_spec(layer, 4, tm),
            _mod_spec(layer, 5, tm),
            _layer_spec((1, D_MODEL), layer + 1),
            _mod_spec(layer + 1, 0, tm),
            _mod_spec(layer + 1, 1, tm),
            pl.BlockSpec((None, D_MODEL, tf), lambda m, f: (j, 0, f)),
            pl.BlockSpec((None, D_MODEL, tf), lambda m, f: (j, 0, f)),
            pl.BlockSpec((None, tf, D_MODEL), lambda m, f: (j, f, 0)),
        ],
        out_specs=(pl.BlockSpec((tm, D_MODEL), lambda m, f: (m, 0)),
                   pl.BlockSpec((tm, D_MODEL), lambda m, f: (m, 0))),
        scratch_shapes=[pltpu.VMEM((tm, D_MODEL), BF16)],
        compiler_params=_cparams(("parallel", "arbitrary")),
        name="ffn",
    )(x, norm2_g, mod, mod, mod, norm1_g, mod, mod, w1, w3, w2)


def _router_kernel(x_ref, g_ref, sh_ref, sc_ref, rwa_ref, rwb_ref, rb_ref, h_ref, info_ref, p_ref, cnt_ref,
                   carry_ref, hhi_ref, hlo_ref, upper_ref):
    i = pl.program_id(0)
    tm = TM_ROUTE

    @pl.when(i == 0)
    def _():
        carry_ref[...] = jnp.zeros_like(carry_ref)
        r = lax.broadcasted_iota(I32, (tm, tm), 0)
        c = lax.broadcasted_iota(I32, (tm, tm), 1)
        upper_ref[...] = (r <= c).astype(BF16)

    gain = g_ref[...] * (1.0 + sc_ref[...])
    sh = sh_ref[...]

    def fill(k, carry):
        rows = pl.ds(pl.multiple_of(k * SLAB, SLAB), SLAB)
        y = _modnorm(x_ref[rows, :], gain, sh)
        hi = y.astype(BF16)
        h_ref[rows, :] = y
        hhi_ref[rows, :] = hi
        hlo_ref[rows, :] = (y - hi.astype(F32)).astype(BF16)
        return carry
    lax.fori_loop(0, tm // SLAB, fill, 0, unroll=SLAB_UNROLL)

    t = lax.dot_general(rwa_ref[...], hhi_ref[...], _NT, preferred_element_type=F32)
    u = lax.dot_general(rwb_ref[...], hlo_ref[...], _NT, preferred_element_type=F32)
    logits = t[0:N_EXPERTS] + t[N_EXPERTS:2 * N_EXPERTS] + u[0:N_EXPERTS] + rb_ref[:, 0:1]
    eidx = lax.broadcasted_iota(I32, (N_EXPERTS, tm), 0).astype(F32)
    none = float(N_EXPERTS)
    m1 = jnp.max(logits, axis=0, keepdims=True)
    i1 = jnp.min(jnp.where(logits == m1, eidx, none), axis=0, keepdims=True)
    rest = jnp.where(eidx == i1, -jnp.inf, logits)
    m2 = jnp.max(rest, axis=0, keepdims=True)
    i2 = jnp.min(jnp.where(rest == m2, eidx, none), axis=0, keepdims=True)
    e = jnp.exp(m2 - m1)
    den = 1.0 + e
    sel1 = eidx == i1
    sel2 = eidx == i2
    onehot = (sel1 | sel2).astype(BF16)
    cnt = jnp.dot(onehot, upper_ref[...], preferred_element_type=F32) + carry_ref[:, 0:1]
    r1 = jnp.sum(jnp.where(sel1, cnt, 0.0), axis=0, keepdims=True) - 1.0
    r2 = jnp.sum(jnp.where(sel2, cnt, 0.0), axis=0, keepdims=True) - 1.0
    info_ref[...] = jnp.zeros_like(info_ref)
    info_ref[0:1, :] = i1.astype(I32)
    info_ref[1:2, :] = i2.astype(I32)
    info_ref[2:3, :] = r1.astype(I32)
    info_ref[3:4, :] = r2.astype(I32)
    p_ref[...] = jnp.zeros_like(p_ref)
    p_ref[0:1, :] = 1.0 / den
    p_ref[1:2, :] = e / den
    total = cnt[:, tm - 1:tm]
    carry_ref[...] = jnp.broadcast_to(total, carry_ref.shape)
    cnt_ref[...] = jnp.broadcast_to(total, cnt_ref.shape).astype(I32)


def _router(x, norm_g, mod, layer, j, rw_a, rw_b, rb):
    tm = TM_ROUTE
    nt = T // tm
    return pl.pallas_call(
        _router_kernel,
        out_shape=(jax.ShapeDtypeStruct((T, D_MODEL), F32),
                   jax.ShapeDtypeStruct((nt, 8, tm), I32),
                   jax.ShapeDtypeStruct((nt, 8, tm), F32),
                   jax.ShapeDtypeStruct((N_EXPERTS, 128), I32)),
        grid=(nt,),
        in_specs=[
            pl.BlockSpec((tm, D_MODEL), lambda i: (i, 0)),
            _layer_spec((1, D_MODEL), layer),
            _mod_spec(layer, 3, tm),
            _mod_spec(layer, 4, tm),
            _layer_spec((2 * N_EXPERTS, D_MODEL), j),
            _layer_spec((2 * N_EXPERTS, D_MODEL), j),
            _layer_spec((N_EXPERTS, 128), j),
        ],
        out_specs=(pl.BlockSpec((tm, D_MODEL), lambda i: (i, 0)),
                   pl.BlockSpec((None, 8, tm), lambda i: (i, 0, 0)),
                   pl.BlockSpec((None, 8, tm), lambda i: (i, 0, 0)),
                   pl.BlockSpec((N_EXPERTS, 128), lambda i: (0, 0))),
        scratch_shapes=[pltpu.VMEM((N_EXPERTS, 128), F32), pltpu.VMEM((tm, D_MODEL), BF16),
                        pltpu.VMEM((tm, D_MODEL), BF16), pltpu.VMEM((tm, tm), BF16)],
        compiler_params=_cparams(("arbitrary",)),
        name="router",
    )(x, norm_g, mod, mod, rw_a, rw_b, rb)


def _dispatch_kernel(pos1_ref, pos2_ref, zstart_ref, zcount_ref, h_ref, o_hbm, zrow_ref, sem):
    i = pl.program_id(0)
    td = TD_DISP
    base = i * td

    def row_copy(src_row, dst_row):
        return pltpu.make_async_copy(h_ref.at[pl.ds(src_row, 1)], o_hbm.at[pl.ds(dst_row, 1)], sem)

    def zero_copy(dst_row):
        return pltpu.make_async_copy(zrow_ref, o_hbm.at[pl.ds(dst_row, 1)], sem)

    @pl.when(i == 0)
    def _():
        zrow_ref[...] = jnp.zeros_like(zrow_ref)
        for e in range(N_EXPERTS + 1):
            def zbody(r, carry):
                zero_copy(zstart_ref[e] + r).start()
                return carry
            lax.fori_loop(0, zcount_ref[e], zbody, 0)

            def zwait(r, carry):
                zero_copy(0).wait()
                return carry
            lax.fori_loop(0, zcount_ref[e], zwait, 0)

    for r in range(td):
        row_copy(r, pos1_ref[base + r]).start()
        row_copy(r, pos2_ref[base + r]).start()

    tile_copy = pltpu.make_async_copy(h_ref, o_hbm.at[pl.ds(0, td)], sem)
    tile_copy.wait()
    tile_copy.wait()


def _dispatch(pos1, pos2, zstart, zcount, h):
    td = TD_DISP
    return pl.pallas_call(
        _dispatch_kernel,
        out_shape=jax.ShapeDtypeStruct((ROWS_CAP, D_MODEL), F32),
        grid_spec=pltpu.PrefetchScalarGridSpec(
            num_scalar_prefetch=4,
            grid=(T // td,),
            in_specs=[pl.BlockSpec((td, D_MODEL), lambda i, *_: (i, 0))],
            out_specs=pl.BlockSpec(memory_space=pl.ANY),
            scratch_shapes=[pltpu.VMEM((1, D_MODEL), F32), pltpu.SemaphoreType.DMA],
        ),
        compiler_params=_cparams(("arbitrary",)),
        name="dispatch",
    )(pos1, pos2, zstart, zcount, h)


def _expert_kernel(te_ref, nact_ref, hs_ref, w1_ref, w3_ref, w2_ref, y_ref, hb_ref):
    i = pl.program_id(0)
    f = pl.program_id(1)
    active = i < nact_ref[0]

    @pl.when(f == 0)
    def _():
        y_ref[...] = jnp.zeros_like(y_ref)

    @pl.when(active & (f == 0))
    def _():
        def body(s, carry):
            r = pl.multiple_of(s * SLAB, SLAB)
            hb_ref[pl.ds(r, SLAB), :] = hs_ref[pl.ds(r, SLAB), :].astype(BF16)
            return carry
        lax.fori_loop(0, TR_EXP // SLAB, body, 0, unroll=SLAB_UNROLL)

    @pl.when(active)
    def _():
        _swiglu_accumulate(hb_ref, w1_ref, w3_ref, w2_ref, y_ref, TR_EXP)


def _experts(tile_expert, n_active, hs, j, w1, w3, w2):
    tr, tf = TR_EXP, TF_EXP
    nf = D_FF // tf

    def row_map(i, f, te, na):
        return (jnp.minimum(i, na[0] - 1), 0)

    def out_map(i, f, te, na):
        return (i, 0)

    def w13_map(i, f, te, na):
        return (j, te[i], 0, jnp.where(i < na[0], f, nf - 1))

    def w2_map(i, f, te, na):
        return (j, te[i], jnp.where(i < na[0], f, nf - 1), 0)

    return pl.pallas_call(
        _expert_kernel,
        out_shape=jax.ShapeDtypeStruct((ROWS_CAP, D_MODEL), F32),
        grid_spec=pltpu.PrefetchScalarGridSpec(
            num_scalar_prefetch=2,
            grid=(NT_EXP, nf),
            in_specs=[
                pl.BlockSpec((tr, D_MODEL), row_map),
                pl.BlockSpec((None, None, D_MODEL, tf), w13_map),
                pl.BlockSpec((None, None, D_MODEL, tf), w13_map),
                pl.BlockSpec((None, None, tf, D_MODEL), w2_map),
            ],
            out_specs=pl.BlockSpec((tr, D_MODEL), out_map),
            scratch_shapes=[pltpu.VMEM((tr, D_MODEL), BF16)],
        ),
        compiler_params=_cparams(("arbitrary", "arbitrary")),
        name="experts",
    )(tile_expert, n_active, hs, w1, w3, w2)


def _gather_residual(pos1_ref, pos2_ref, x_ref, p1_ref, p2_ref, g2_ref, y_hbm, buf_ref, sem, emit):
    i = pl.program_id(0)
    n = pl.num_programs(0)
    tc = TC_COMB
    slot = i % 2

    def start_rows(tile, s, r0):
        base = tile * tc
        for d in range(COMB_SLAB):
            r = r0 + d
            pltpu.make_async_copy(y_hbm.at[pl.ds(pos1_ref[base + r], 1)],
                                  buf_ref.at[s, 0, pl.ds(r, 1)], sem.at[s]).start()
            pltpu.make_async_copy(y_hbm.at[pl.ds(pos2_ref[base + r], 1)],
                                  buf_ref.at[s, 1, pl.ds(r, 1)], sem.at[s]).start()

    @pl.when(i == 0)
    def _():
        def body(k, carry):
            start_rows(0, 0, k * COMB_SLAB)
            return carry
        lax.fori_loop(0, tc // COMB_SLAB, body, 0)

    pltpu.make_async_copy(y_hbm.at[pl.ds(0, tc)], buf_ref.at[slot, 0], sem.at[slot]).wait()
    pltpu.make_async_copy(y_hbm.at[pl.ds(0, tc)], buf_ref.at[slot, 1], sem.at[slot]).wait()
    g2 = g2_ref[...]

    def tile_pass(prefetch):
        for k in range(tc // COMB_SLAB):
            r0 = k * COMB_SLAB
            if prefetch:
                start_rows(i + 1, 1 - slot, r0)
            rows = slice(r0, r0 + COMB_SLAB)
            mix = (p1_ref[rows, :] * buf_ref[slot, 0, rows, :]
                   + p2_ref[rows, :] * buf_ref[slot, 1, rows, :])
            emit(rows, x_ref[rows, :] + g2 * mix)

    @pl.when(i + 1 < n)
    def _():
        tile_pass(True)

    @pl.when(i + 1 >= n)
    def _():
        tile_pass(False)


def _combine_kernel(pos1_ref, pos2_ref, x_ref, p1_ref, p2_ref, g2_ref, gn_ref, shn_ref, scn_ref, y_hbm,
                    o_ref, hn_ref, buf_ref, sem):
    gain_n, shn = gn_ref[...] * (1.0 + scn_ref[...]), shn_ref[...]

    def emit(rows, xn):
        o_ref[rows, :] = xn
        hn_ref[rows, :] = _modnorm(xn, gain_n, shn).astype(BF16)

    _gather_residual(pos1_ref, pos2_ref, x_ref, p1_ref, p2_ref, g2_ref, y_hbm, buf_ref, sem, emit)


def _combine_final_kernel(pos1_ref, pos2_ref, x_ref, p1_ref, p2_ref, g2_ref, gf_ref, y_hbm,
                          yp_ref, ys_ref, buf_ref, sem, yt_ref, *, n_ctx_tiles):
    gf = gf_ref[...]

    def emit(rows, xn):
        yt_ref[rows, :] = xn * lax.rsqrt(jnp.mean(xn * xn, axis=-1, keepdims=True) + EPS) * gf

    _gather_residual(pos1_ref, pos2_ref, x_ref, p1_ref, p2_ref, g2_ref, y_hbm, buf_ref, sem, emit)
    i = pl.program_id(0)

    @pl.when(i < n_ctx_tiles)
    def _():
        yp_ref[...] = yt_ref[...]

    @pl.when(i >= n_ctx_tiles)
    def _():
        ys_ref[...] = yt_ref[...]


def _combine_call(body, extra_specs, out_shape, out_specs, name, args, extra_scratch=()):
    tc = TC_COMB
    row_tile = pl.BlockSpec((tc, D_MODEL), lambda i, *_: (i, 0))
    col = pl.BlockSpec((tc, 1), lambda i, *_: (i, 0))
    return pl.pallas_call(
        body,
        out_shape=out_shape,
        grid_spec=pltpu.PrefetchScalarGridSpec(
            num_scalar_prefetch=2,
            grid=(T // tc,),
            in_specs=[row_tile, col, col] + extra_specs + [pl.BlockSpec(memory_space=pl.ANY)],
            out_specs=out_specs,
            scratch_shapes=[pltpu.VMEM((2, 2, tc, D_MODEL), F32), pltpu.SemaphoreType.DMA((2,))]
            + list(extra_scratch),
        ),
        compiler_params=_cparams(("arbitrary",)),
        name=name,
    )(*args)


def _combine(pos1, pos2, x, p1, p2, mod, layer, norm1_g, y):
    tc = TC_COMB
    row_tile = pl.BlockSpec((tc, D_MODEL), lambda i, *_: (i, 0))
    return _combine_call(
        _combine_kernel,
        [_mod_spec(layer, 5, tc), _layer_spec((1, D_MODEL), layer + 1),
         _mod_spec(layer + 1, 0, tc), _mod_spec(layer + 1, 1, tc)],
        (jax.ShapeDtypeStruct((T, D_MODEL), F32), jax.ShapeDtypeStruct((T, D_MODEL), BF16)),
        (row_tile, row_tile), "combine",
        (pos1, pos2, x, p1, p2, mod, norm1_g, mod, mod, y))


def _combine_final(pos1, pos2, x, p1, p2, mod, layer, final_g, y):
    tc = TC_COMB
    nct = TP // tc
    return _combine_call(
        functools.partial(_combine_final_kernel, n_ctx_tiles=nct),
        [_mod_spec(layer, 5, tc), pl.BlockSpec((1, D_MODEL), lambda i, *_: (0, 0))],
        (jax.ShapeDtypeStruct((TP, D_MODEL), F32), jax.ShapeDtypeStruct((TS, D_MODEL), F32)),
        (pl.BlockSpec((tc, D_MODEL), lambda i, *_: (jnp.minimum(i, nct - 1), 0)),
         pl.BlockSpec((tc, D_MODEL), lambda i, *_: (jnp.maximum(i - nct, 0), 0))),
        "combine_final",
        (pos1, pos2, x, p1, p2, mod, final_g, y),
        extra_scratch=[pltpu.VMEM((tc, D_MODEL), F32)])


def _moe(x, norm_g, mod, layer, j, rw, rb, w1, w3, w2, norm1_g, final_g):
    h, info, probs, counts = _router(x, norm_g, mod, layer, j, rw[0], rw[1], rb)
    counts = counts[:, 0]
    cap = ((counts + TR_EXP - 1) // TR_EXP) * TR_EXP
    ends = jnp.cumsum(cap)
    starts = ends - cap
    e1 = info[:, 0, :].reshape(T)
    e2 = info[:, 1, :].reshape(T)
    pos1 = starts[e1] + info[:, 2, :].reshape(T)
    pos2 = starts[e2] + info[:, 3, :].reshape(T)
    tile_end = ends // TR_EXP
    n_active = tile_end[-1:]
    tiles = jnp.minimum(jnp.arange(NT_EXP, dtype=I32), n_active[0] - 1)
    tile_expert = jnp.sum((tiles[:, None] >= tile_end[None, :]).astype(I32), axis=1)
    zstart = jnp.concatenate([starts + counts, ends[-1:]])
    zcount = jnp.concatenate([cap - counts, ROWS_CAP - ends[-1:]])
    hs = _dispatch(pos1, pos2, zstart, zcount, h)
    y = _experts(tile_expert, n_active, hs, j, w1, w3, w2)
    p1 = probs[:, 0, :].reshape(T, 1)
    p2 = probs[:, 1, :].reshape(T, 1)
    if layer == DEPTH - 1:
        return _combine_final(pos1, pos2, x, p1, p2, mod, layer, final_g, y)
    return _combine(pos1, pos2, x, p1, p2, mod, layer, norm1_g, y)


def _grid_pos_embed():
    rows = DEC_SEQ // GRID_W
    r, col = jnp.meshgrid(jnp.arange(rows, dtype=F32), jnp.arange(GRID_W, dtype=F32), indexing='ij')
    r = r.reshape(-1)
    col = col.reshape(-1)
    quarter = D_MODEL // 4
    freq = jnp.exp(-math.log(10000.0) * jnp.arange(quarter, dtype=F32) / quarter)
    ar = r[:, None] * freq
    ac = col[:, None] * freq
    return jnp.concatenate([jnp.sin(ar), jnp.cos(ar), jnp.sin(ac), jnp.cos(ac)], axis=-1)


def kernel(x_prompt, x_sample, state_gla, c, c_ctx, norm1_g, norm2_g, w_mod, b_mod, w_in,
           sgu_ln_g, sgu_ln_b, w_spatial, b_spatial, gla_a2, gla_ab, gla_norm_g,
           w_branch_a, w_branch_b, w_out, ffn_w1, ffn_w3, ffn_w2,
           moe_router, moe_router_b, moe_w1, moe_w3, moe_w2, final_g):
    cond = jnp.zeros((COND_PAD, D_MODEL), F32).at[0].set(c_ctx).at[1:N_COND].set(c)
    mod = _mod_table(cond, w_mod, b_mod).reshape(DEPTH, COND_PAD, 6, 1, D_MODEL)

    lr_end = W_IN_LR + 2 * GLA_RANK
    w_main = jnp.concatenate([w_in[:, :, lr_end:], w_in[:, :, :W_IN_LR]], axis=2).astype(BF16)
    w_lr = jnp.pad(w_in[:, :, W_IN_LR:lr_end], ((0, 0), (0, 0), (0, LR_PAD - 2 * GLA_RANK))).astype(BF16)
    norm1 = norm1_g.reshape(DEPTH, 1, D_MODEL)
    norm2 = norm2_g.reshape(DEPTH, 1, D_MODEL)
    ln_g = sgu_ln_g.reshape(DEPTH, 1, D_A)
    ln_b = sgu_ln_b.reshape(DEPTH, 1, D_A)
    w_s = w_spatial.astype(BF16)
    bs_full = jnp.repeat(jnp.swapaxes(b_spatial, 1, 2), A_GROUP, axis=2)
    a2_pad = jnp.zeros((DEPTH, LR_PAD, GLA_DK), F32)
    a2f = a2_pad.at[:, :GLA_RANK].set(gla_a2[:, 0]).astype(BF16)
    a2b = a2_pad.at[:, GLA_RANK:2 * GLA_RANK].set(gla_a2[:, 1]).astype(BF16)
    ab = gla_ab.reshape(DEPTH, 2, 1, GLA_DK)
    gla_g = gla_norm_g.reshape(DEPTH, 1, GLA_DV)
    w_a = w_branch_a.astype(BF16)
    w_b = w_branch_b.astype(BF16)
    w_o = w_out.astype(BF16)
    f_w1, f_w3, f_w2 = ffn_w1.astype(BF16), ffn_w3.astype(BF16), ffn_w2.astype(BF16)
    m_w1, m_w3, m_w2 = moe_w1.astype(BF16), moe_w3.astype(BF16), moe_w2.astype(BF16)
    rw_t = jnp.swapaxes(moe_router, 1, 2)
    rw_hi = rw_t.astype(BF16)
    rw_lo = (rw_t - rw_hi.astype(F32)).astype(BF16)
    rw = (jnp.concatenate([rw_hi, rw_lo], axis=1),
          jnp.concatenate([rw_hi, jnp.zeros_like(rw_hi)], axis=1))
    rb = jnp.broadcast_to(moe_router_b[:, :, None], moe_router_b.shape + (128,))

    x, h = _embed(x_prompt.reshape(TP, D_MODEL), x_sample.reshape(TS, D_MODEL), _grid_pos_embed(),
                  norm1, mod)
    new_state = jnp.zeros((BATCH, DEPTH, 2, N_GLA_HEADS, GLA_DK_HEAD, GLA_DV_HEAD), F32)
    for l in range(DEPTH):
        proj, lr = _proj(h, l, w_main, w_lr)
        a = _sgu(proj, l, ln_g, ln_b, w_s, bs_full)
        o_f, o_b, new_state = _gla(proj, lr, a2f, a2b, ab, state_gla, new_state, l)
        x = _mixout(a, o_f, o_b, proj, gla_g, x, mod, l, w_a, w_b, w_o)
        j = l // 2
        if l % 2 == 0:
            x, h = _ffn(x, norm2, norm1, mod, l, j, f_w1, f_w3, f_w2)
        elif l < DEPTH - 1:
            x, h = _moe(x, norm2, mod, l, j, rw, rb, m_w1, m_w3, m_w2, norm1, final_g[None])
        else:
            y_prompt, y_sample = _moe(x, norm2, mod, l, j, rw, rb, m_w1, m_w3, m_w2, norm1,
                                      final_g[None])

    return (y_prompt.reshape(BATCH, SEQ, D_MODEL), y_sample.reshape(DEC_BATCH, DEC_SEQ, D_MODEL),
            new_state)
```

```python
import functools
import math

import numpy as np
import jax
import jax.numpy as jnp
from jax import lax
from jax.experimental import pallas as pl
from jax.experimental.pallas import tpu as pltpu

F32 = jnp.float32
BF16 = jnp.bfloat16
I32 = jnp.int32

D_MODEL = 2048
BATCH = 32
SEQ = 256
DEPTH = 4
DEC_BATCH = 8
DEC_SEQ = 2048
GRID_W = 64
CHUNK_A = 128
D_A = 1024
N_A_GROUPS = 8
A_GROUP = D_A // N_A_GROUPS
N_GLA_HEADS = 4
GLA_DK = D_MODEL // 4
GLA_DV = D_MODEL // 2
GLA_DK_HEAD = GLA_DK // N_GLA_HEADS
GLA_DV_HEAD = GLA_DV // N_GLA_HEADS
GLA_RANK = 16
GLA_TAU = 16.0
GLA_CHUNK = 64
D_FF = 5632
N_EXPERTS = 8
TOP_K = 2
EPS = 1e-6

TP = BATCH * SEQ
TS = DEC_BATCH * DEC_SEQ
T = TP + TS
N_COND = 1 + DEC_BATCH
COND_PAD = 16

W_IN_LR = 2 * D_A + 2 * GLA_DK + 2 * GLA_DV
C_GA, C_GB, C_U, C_V, C_Q, C_K, C_VG, C_R = 0, 2048, 4096, 5120, 6144, 6656, 7168, 8192
N_MAIN = 9216
LR_PAD = 128

VMEM_LIMIT = 56 * 1024 * 1024

TM_PROJ, TN_PROJ = 1024, 2304
TM_EMBED = 512
TM_SGU = 512
TB_GLA = 256
TM_MIX = 256
TM_FFN, TF_FFN = 512, 512
TM_ROUTE = 512
TD_DISP = 256
TR_EXP, TF_EXP = 512, 512
TC_COMB = 256
SLAB = 32
SLAB_UNROLL = 4
RS_SWIGLU = 512
CAST_BLOCKS = N_EXPERTS * 2 * (D_FF // TF_FFN)
COMB_SLAB = 16

NT_EXP = (TOP_K * T) // TR_EXP + N_EXPERTS
ROWS_CAP = NT_EXP * TR_EXP


def _cparams(sem):
    return pltpu.CompilerParams(dimension_semantics=sem, vmem_limit_bytes=VMEM_LIMIT)


def _mod_row(i, tm):
    return jnp.maximum((i * tm) // DEC_SEQ - (TP // DEC_SEQ - 1), 0)


def _mod_spec(layer, j, tm, width=D_MODEL):
    return pl.BlockSpec((None, None, None, 1, width),
                        lambda m, *_: (layer, _mod_row(m, tm), j, 0, 0))


def _layer_spec(shape, layer):
    zeros = (0,) * len(shape)
    return pl.BlockSpec((None,) + tuple(shape), lambda *_: (layer,) + zeros)


def _modnorm(x, gain, sh):
    return x * lax.rsqrt(jnp.mean(x * x, axis=-1, keepdims=True) + EPS) * gain + sh


def _fill_modnorm(x_ref, g, sc, sh, h_ref, rows):
    gain = g * (1.0 + sc)

    def body(i, carry):
        r = pl.multiple_of(i * SLAB, SLAB)
        h_ref[pl.ds(r, SLAB), :] = _modnorm(x_ref[pl.ds(r, SLAB), :], gain, sh).astype(h_ref.dtype)
        return carry
    lax.fori_loop(0, rows // SLAB, body, 0, unroll=SLAB_UNROLL)


def _silu(x):
    return x * jax.nn.sigmoid(x)


def _gelu_tanh(x):
    return jax.nn.gelu(x, approximate=True)


def _log_sigmoid(x):
    return jnp.minimum(x, 0.0) - jnp.log(1.0 + jnp.exp(-jnp.abs(x)))


def _swiglu_accumulate(h_ref, w1_ref, w3_ref, w2_ref, acc_ref, rows):
    for r in range(rows // RS_SWIGLU):
        sl = slice(r * RS_SWIGLU, (r + 1) * RS_SWIGLU)
        h = h_ref[sl, :]
        u = _silu(jnp.dot(h, w1_ref[...], preferred_element_type=F32)) * jnp.dot(
            h, w3_ref[...], preferred_element_type=F32)
        acc_ref[sl, :] += jnp.dot(u.astype(BF16), w2_ref[...], preferred_element_type=F32)


def _embed_kernel(xp_ref, xs_ref, emb_ref, g_ref, sh_ref, sc_ref, o_ref, h_ref, *, n_ctx_tiles):
    i = pl.program_id(0)

    @pl.when(i < n_ctx_tiles)
    def _():
        o_ref[...] = xp_ref[...]

    @pl.when(i >= n_ctx_tiles)
    def _():
        o_ref[...] = xs_ref[...] + emb_ref[...]

    _fill_modnorm(o_ref, g_ref[...], sc_ref[...], sh_ref[...], h_ref, TM_EMBED)


def _embed(xp, xs, emb, norm_g, mod):
    tm = TM_EMBED
    nct = TP // tm
    per_seq = DEC_SEQ // tm
    return pl.pallas_call(
        functools.partial(_embed_kernel, n_ctx_tiles=nct),
        out_shape=(jax.ShapeDtypeStruct((T, D_MODEL), F32),
                   jax.ShapeDtypeStruct((T, D_MODEL), BF16)),
        grid=(T // tm,),
        in_specs=[
            pl.BlockSpec((tm, D_MODEL), lambda i: (jnp.minimum(i, nct - 1), 0)),
            pl.BlockSpec((tm, D_MODEL), lambda i: (jnp.maximum(i - nct, 0), 0)),
            pl.BlockSpec((tm, D_MODEL), lambda i: (jnp.maximum(i - nct, 0) % per_seq, 0)),
            _layer_spec((1, D_MODEL), 0),
            _mod_spec(0, 0, tm),
            _mod_spec(0, 1, tm),
        ],
        out_specs=(pl.BlockSpec((tm, D_MODEL), lambda i: (i, 0)),
                   pl.BlockSpec((tm, D_MODEL), lambda i: (i, 0))),
        compiler_params=_cparams(("parallel",)),
        name="embed",
    )(xp, xs, emb, norm_g, mod, mod)


def _mod_kernel(c_ref, w_ref, b_ref, o_ref):
    c = c_ref[...]
    s = _silu(c).astype(BF16)
    o_ref[...] = jnp.dot(s, w_ref[...].astype(BF16), preferred_element_type=F32) + b_ref[...]


def _mod_table(cond, w_mod, b_mod):
    tn = 1024
    n6 = 6 * D_MODEL
    return pl.pallas_call(
        _mod_kernel,
        out_shape=jax.ShapeDtypeStruct((DEPTH, COND_PAD, n6), F32),
        grid=(DEPTH, n6 // tn),
        in_specs=[
            pl.BlockSpec((COND_PAD, D_MODEL), lambda l, n: (0, 0)),
            pl.BlockSpec((None, D_MODEL, tn), lambda l, n: (l, 0, n)),
            pl.BlockSpec((None, 1, tn), lambda l, n: (l, 0, n)),
        ],
        out_specs=pl.BlockSpec((None, COND_PAD, tn), lambda l, n: (l, 0, n)),
        compiler_params=_cparams(("parallel", "parallel")),
        name="mod_table",
    )(cond, w_mod, b_mod.reshape(DEPTH, 1, n6))


def _proj_kernel(h_ref, w_ref, wlr_ref, o_ref, lr_ref):
    @pl.when(pl.program_id(1) == 0)
    def _():
        lr_ref[...] = jnp.dot(h_ref[...], wlr_ref[...], preferred_element_type=F32)

    o_ref[...] = jnp.dot(h_ref[...], w_ref[...], preferred_element_type=F32).astype(o_ref.dtype)


def _proj(h, layer, w_main, w_lr):
    tm, tn = TM_PROJ, TN_PROJ
    return pl.pallas_call(
        _proj_kernel,
        out_shape=(jax.ShapeDtypeStruct((T, N_MAIN), BF16),
                   jax.ShapeDtypeStruct((T, LR_PAD), F32)),
        grid=(T // tm, N_MAIN // tn),
        in_specs=[
            pl.BlockSpec((tm, D_MODEL), lambda m, n: (m, 0)),
            pl.BlockSpec((None, D_MODEL, tn), lambda m, n: (layer, 0, n)),
            _layer_spec((D_MODEL, LR_PAD), layer),
        ],
        out_specs=(pl.BlockSpec((tm, tn), lambda m, n: (m, n)),
                   pl.BlockSpec((tm, LR_PAD), lambda m, n: (m, 0))),
        compiler_params=_cparams(("parallel", "arbitrary")),
        name="proj",
    )(h, w_main, w_lr)


def _sgu_kernel(pu_ref, pv_ref, lng_ref, lnb_ref, ws_ref, bs_ref, a_ref, vn_ref):
    gv = _gelu_tanh(pv_ref[...].astype(F32))
    mu = jnp.mean(gv, axis=-1, keepdims=True)
    d = gv - mu
    var = jnp.mean(d * d, axis=-1, keepdims=True)
    vn_ref[...] = (d * lax.rsqrt(var + EPS) * lng_ref[...] + lnb_ref[...]).astype(BF16)
    for c in range(TM_SGU // CHUNK_A):
        rows = slice(c * CHUNK_A, (c + 1) * CHUNK_A)
        for g in range(N_A_GROUPS):
            cols = slice(g * A_GROUP, (g + 1) * A_GROUP)
            f = jnp.dot(ws_ref[g], vn_ref[rows, cols], preferred_element_type=F32) + bs_ref[:, cols]
            gu = _gelu_tanh(pu_ref[rows, cols].astype(F32))
            a_ref[rows, cols] = (gu * f).astype(BF16)


def _sgu(proj, layer, ln_g, ln_b, w_s, bs_full):
    tm = TM_SGU
    return pl.pallas_call(
        _sgu_kernel,
        out_shape=jax.ShapeDtypeStruct((T, D_A), BF16),
        grid=(T // tm,),
        in_specs=[
            pl.BlockSpec((tm, D_A), lambda i: (i, C_U // D_A)),
            pl.BlockSpec((tm, D_A), lambda i: (i, C_V // D_A)),
            _layer_spec((1, D_A), layer),
            _layer_spec((1, D_A), layer),
            _layer_spec((N_A_GROUPS, CHUNK_A, CHUNK_A), layer),
            _layer_spec((CHUNK_A, D_A), layer),
        ],
        out_specs=pl.BlockSpec((tm, D_A), lambda i: (i, 0)),
        scratch_shapes=[pltpu.VMEM((tm, D_A), BF16)],
        compiler_params=_cparams(("parallel",)),
        name="sgu",
    )(proj, proj, ln_g, ln_b, w_s, bs_full)


def _split3(x):
    hi = x.astype(BF16)
    r = x - hi.astype(F32)
    mid = r.astype(BF16)
    lo = (r - mid.astype(F32)).astype(BF16)
    return hi, mid, lo


def _tri_cumsum(tri, x):
    hi, mid, lo = _split3(x)
    return (jnp.dot(tri, hi, preferred_element_type=F32)
            + jnp.dot(tri, mid, preferred_element_type=F32)
            + jnp.dot(tri, lo, preferred_element_type=F32))


_NT = (((1,), (1,)), ((), ()))
_TN = (((0,), (0,)), ((), ()))


def _gla_decay(lr_ref, a2_ref, ab_ref, tb, backward):
    z = jnp.dot(lr_ref[...].astype(BF16), a2_ref[...], preferred_element_type=F32) + ab_ref[...]
    la = _log_sigmoid(z) / GLA_TAU
    r = lax.broadcasted_iota(I32, (tb, tb), 0)
    c = lax.broadcasted_iota(I32, (tb, tb), 1)
    same = (r // GLA_CHUNK) == (c // GLA_CHUNK)
    tri = (same & ((c >= r) if backward else (c <= r))).astype(BF16)
    return _tri_cumsum(tri, la)


def _gla_block(dirs, tb):
    nchunk = tb // GLA_CHUNK
    scale = GLA_DK_HEAD ** -0.5
    rr = lax.broadcasted_iota(I32, (GLA_CHUNK, GLA_CHUNK), 0)
    cc = lax.broadcasted_iota(I32, (GLA_CHUNK, GLA_CHUNK), 1)
    decays = [_gla_decay(d[0], d[1], d[2], tb, d[8]) for d in dirs]
    per_dir = []
    for (lr_ref, a2_ref, ab_ref, q_ref, k_ref, v_ref, o_ref, st_ref, backward), b in zip(dirs, decays):
        units = []
        per_dir.append(units)
        eb = jnp.exp(b)
        enb = jnp.exp(-b)
        mask = (cc >= rr) if backward else (cc <= rr)
        order = range(nchunk - 1, -1, -1) if backward else range(nchunk)
        for ci in order:
            rows = slice(ci * GLA_CHUNK, (ci + 1) * GLA_CHUNK)
            last = ci * GLA_CHUNK if backward else (ci + 1) * GLA_CHUNK - 1
            btot = b[last:last + 1, :]
            eke = jnp.exp(btot - b[rows, :])
            gdec = jnp.exp(btot)
            for h in range(N_GLA_HEADS):
                kc = slice(h * GLA_DK_HEAD, (h + 1) * GLA_DK_HEAD)
                vc = slice(h * GLA_DV_HEAD, (h + 1) * GLA_DV_HEAD)
                q = q_ref[rows, kc].astype(F32) * scale
                k = k_ref[rows, kc].astype(F32)
                qd = (q * eb[rows, kc]).astype(BF16)
                kd = (k * enb[rows, kc]).astype(BF16)
                ke = (k * eke[:, kc]).astype(BF16)
                vv = v_ref[rows, vc]
                att = lax.dot_general(qd, kd, _NT, preferred_element_type=F32)
                att = jnp.where(mask, att, 0.0).astype(BF16)
                dst = lax.dot_general(vv, ke, _TN, preferred_element_type=F32)
                units.append(dict(o_ref=o_ref, st_ref=st_ref, rows=rows, vc=vc, h=h, qd=qd, vv=vv,
                                  att=att, dst=dst, gdec=gdec[:, kc]))
    units = [u for group in zip(*per_dir) for u in group]
    for u in units:
        u["o_intra"] = jnp.dot(u["att"], u["vv"], preferred_element_type=F32)
    states = {}
    for u in units:
        key = (id(u["st_ref"]), u["h"])
        st = states[key] if key in states else u["st_ref"][u["h"]]
        u["o_ref"][u["rows"], u["vc"]] = u["o_intra"] + lax.dot_general(
            u["qd"], st.astype(BF16), _NT, preferred_element_type=F32)
        states[key] = st * u["gdec"] + u["dst"]
    for (_, _, _, _, _, _, _, st_ref, _) in dirs:
        for h in range(N_GLA_HEADS):
            st_ref[h] = states[(id(st_ref), h)]


def _gla_kernel(fwd_ref, bwd_ref, flag_ref, seq_ref,
                lrf_ref, lrb_ref, a2f_ref, a2b_ref, abf_ref, abb_ref,
                qf_ref, kf_ref, vf_ref, qb_ref, kb_ref, vb_ref, s0_ref, sprev_ref,
                of_ref, ob_ref, sfin_ref, stf_ref, stb_ref):
    del sprev_ref
    i = pl.program_id(0)
    first = (flag_ref[i] & 1) == 1
    last = (flag_ref[i] & 2) == 2
    is_ctx = seq_ref[i] < BATCH

    @pl.when(first & is_ctx)
    def _():
        stf_ref[...] = jnp.zeros_like(stf_ref)
        stb_ref[...] = jnp.zeros_like(stb_ref)

    @pl.when(first & jnp.logical_not(is_ctx))
    def _():
        for h in range(N_GLA_HEADS):
            stf_ref[h] = s0_ref[0, h].T
            stb_ref[h] = s0_ref[1, h].T

    _gla_block([(lrf_ref, a2f_ref, abf_ref, qf_ref, kf_ref, vf_ref, of_ref, stf_ref, False),
                (lrb_ref, a2b_ref, abb_ref, qb_ref, kb_ref, vb_ref, ob_ref, stb_ref, True)], TB_GLA)

    @pl.when(last & is_ctx)
    def _():
        for h in range(N_GLA_HEADS):
            sfin_ref[0, h] = stf_ref[h].T
            sfin_ref[1, h] = stb_ref[h].T


def _gla_tables():
    tb = TB_GLA
    fwd, bwd, flag, seq = [], [], [], []
    for s, (row0, seqlen) in enumerate([(b * SEQ, SEQ) for b in range(BATCH)]
                                       + [(TP + b * DEC_SEQ, DEC_SEQ) for b in range(DEC_BATCH)]):
        nb = seqlen // tb
        for j in range(nb):
            fwd.append(row0 // tb + j)
            bwd.append(row0 // tb + nb - 1 - j)
            flag.append((1 if j == 0 else 0) | (2 if j == nb - 1 else 0))
            seq.append(s)
    return tuple(np.asarray(t, np.int32) for t in (fwd, bwd, flag, seq))


def _gla(proj, lr, a2f, a2b, ab, s0, s_all, layer):
    tb = TB_GLA
    fwd_t, bwd_t, flag_t, seq_t = _gla_tables()
    nsteps = fwd_t.shape[0]
    n_prefetch = 4

    def fwd(i, fw, bw, fl, sq):
        return fw[i]

    def bwd(i, fw, bw, fl, sq):
        return bw[i]

    def col_spec(rowfn, width, col):
        return pl.BlockSpec((tb, width), lambda i, *t: (rowfn(i, *t), col // width))

    state_block = (2, N_GLA_HEADS, GLA_DK_HEAD, GLA_DV_HEAD)
    in_specs = [
        pl.BlockSpec((tb, LR_PAD), lambda i, *t: (fwd(i, *t), 0)),
        pl.BlockSpec((tb, LR_PAD), lambda i, *t: (bwd(i, *t), 0)),
        _layer_spec((LR_PAD, GLA_DK), layer),
        _layer_spec((LR_PAD, GLA_DK), layer),
        pl.BlockSpec((None, None, 1, GLA_DK), lambda i, *t: (layer, 0, 0, 0)),
        pl.BlockSpec((None, None, 1, GLA_DK), lambda i, *t: (layer, 1, 0, 0)),
        col_spec(fwd, GLA_DK, C_Q), col_spec(fwd, GLA_DK, C_K), col_spec(fwd, GLA_DV, C_VG),
        col_spec(bwd, GLA_DK, C_Q), col_spec(bwd, GLA_DK, C_K), col_spec(bwd, GLA_DV, C_VG),
        pl.BlockSpec((None, None) + state_block,
                     lambda i, fw, bw, fl, sq: (jnp.maximum(sq[i] - BATCH, 0), layer, 0, 0, 0, 0)),
        pl.BlockSpec(memory_space=pl.ANY),
    ]
    out_shape = (jax.ShapeDtypeStruct((T, GLA_DV), F32),
                 jax.ShapeDtypeStruct((T, GLA_DV), F32),
                 jax.ShapeDtypeStruct((BATCH, DEPTH) + state_block, F32))
    out_specs = (pl.BlockSpec((tb, GLA_DV), lambda i, *t: (fwd(i, *t), 0)),
                 pl.BlockSpec((tb, GLA_DV), lambda i, *t: (bwd(i, *t), 0)),
                 pl.BlockSpec((None, None) + state_block,
                              lambda i, fw, bw, fl, sq: (jnp.minimum(sq[i], BATCH - 1), layer, 0, 0, 0, 0)))
    return pl.pallas_call(
        _gla_kernel,
        out_shape=out_shape,
        grid_spec=pltpu.PrefetchScalarGridSpec(
            num_scalar_prefetch=n_prefetch,
            grid=(nsteps,),
            in_specs=in_specs,
            out_specs=out_specs,
            scratch_shapes=[pltpu.VMEM((N_GLA_HEADS, GLA_DV_HEAD, GLA_DK_HEAD), F32),
                            pltpu.VMEM((N_GLA_HEADS, GLA_DV_HEAD, GLA_DK_HEAD), F32)],
        ),
        input_output_aliases={n_prefetch + len(in_specs) - 1: 2},
        compiler_params=_cparams(("arbitrary",)),
        name="gla",
    )(fwd_t, bwd_t, flag_t, seq_t, lr, lr, a2f, a2b, ab, ab,
      proj, proj, proj, proj, proj, proj, s0, s_all)


def _mixout_kernel(a_ref, of_ref, ob_ref, pr_ref, gn_ref, ga_ref, gb_ref, x_ref, g1_ref,
                   wa_ref, wb_ref, wo_ref, out_ref, o_ref):
    for h in range(N_GLA_HEADS):
        vc = slice(h * GLA_DV_HEAD, (h + 1) * GLA_DV_HEAD)
        o = of_ref[:, vc] + ob_ref[:, vc]
        o = o * lax.rsqrt(jnp.mean(o * o, axis=-1, keepdims=True) + EPS)
        o_ref[:, vc] = (o * gn_ref[:, vc] * _silu(pr_ref[:, vc].astype(F32))).astype(BF16)
    ya = jnp.dot(a_ref[...], wa_ref[...], preferred_element_type=F32)
    yb = jnp.dot(o_ref[...], wb_ref[...], preferred_element_type=F32)
    merged = (jax.nn.sigmoid(ga_ref[...].astype(F32)) * ya
              + jax.nn.sigmoid(gb_ref[...].astype(F32)) * yb).astype(BF16)
    y = jnp.dot(merged, wo_ref[...], preferred_element_type=F32)
    out_ref[...] = x_ref[...] + g1_ref[...] * y


def _resident_spec(shape, layer):
    zeros = (0,) * len(shape)
    return pl.BlockSpec((None,) + tuple(shape), lambda *_: (layer,) + zeros,
                        pipeline_mode=pl.Buffered(1))


def _mixout(a, o_f, o_b, proj, gla_g, x, mod, layer, w_a, w_b, w_o):
    tm = TM_MIX
    return pl.pallas_call(
        _mixout_kernel,
        out_shape=jax.ShapeDtypeStruct((T, D_MODEL), F32),
        grid=(T // tm,),
        in_specs=[
            pl.BlockSpec((tm, D_A), lambda m: (m, 0)),
            pl.BlockSpec((tm, GLA_DV), lambda m: (m, 0)),
            pl.BlockSpec((tm, GLA_DV), lambda m: (m, 0)),
            pl.BlockSpec((tm, GLA_DV), lambda m: (m, C_R // GLA_DV)),
            _layer_spec((1, GLA_DV), layer),
            pl.BlockSpec((tm, D_MODEL), lambda m: (m, C_GA // D_MODEL)),
            pl.BlockSpec((tm, D_MODEL), lambda m: (m, C_GB // D_MODEL)),
            pl.BlockSpec((tm, D_MODEL), lambda m: (m, 0)),
            _mod_spec(layer, 2, tm),
            _resident_spec((D_A, D_MODEL), layer),
            _resident_spec((GLA_DV, D_MODEL), layer),
            _resident_spec((D_MODEL, D_MODEL), layer),
        ],
        out_specs=pl.BlockSpec((tm, D_MODEL), lambda m: (m, 0)),
        scratch_shapes=[pltpu.VMEM((tm, GLA_DV), BF16)],
        compiler_params=_cparams(("parallel",)),
        name="mixout",
    )(a, o_f, o_b, proj, gla_g, proj, proj, x, mod, w_a, w_b, w_o)


def _ffn_kernel(x_ref, g_ref, sh_ref, sc_ref, g2_ref, gn_ref, shn_ref, scn_ref, w1_ref, w3_ref, w2_ref,
                e1_ref, e3_ref, e2_ref, o_ref, hn_ref, c1_ref, c3_ref, c2_ref, h_ref, *, nf):
    f = pl.program_id(1)

    s = pl.program_id(0) * nf + f

    @pl.when(s < CAST_BLOCKS)
    def _():
        c1_ref[...] = e1_ref[...].astype(BF16)

    @pl.when((s >= CAST_BLOCKS) & (s < 2 * CAST_BLOCKS))
    def _():
        c3_ref[...] = e3_ref[...].astype(BF16)

    @pl.when(s >= 2 * CAST_BLOCKS)
    def _():
        c2_ref[...] = e2_ref[...].astype(BF16)

    @pl.when(f == 0)
    def _():
        _fill_modnorm(x_ref, g_ref[...], sc_ref[...], sh_ref[...], h_ref, TM_FFN)
        o_ref[...] = jnp.zeros_like(o_ref)

    _swiglu_accumulate(h_ref, w1_ref, w3_ref, w2_ref, o_ref, TM_FFN)

    @pl.when(f == nf - 1)
    def _():
        g2 = g2_ref[...]
        gain_n, shn = gn_ref[...] * (1.0 + scn_ref[...]), shn_ref[...]

        def body(i, carry):
            r = pl.multiple_of(i * SLAB, SLAB)
            xn = x_ref[pl.ds(r, SLAB), :] + g2 * o_ref[pl.ds(r, SLAB), :]
            o_ref[pl.ds(r, SLAB), :] = xn
            hn_ref[pl.ds(r, SLAB), :] = _modnorm(xn, gain_n, shn).astype(BF16)
            return carry
        lax.fori_loop(0, TM_FFN // SLAB, body, 0, unroll=SLAB_UNROLL)


def _ffn(x, norm2_g, norm1_g, mod, layer, j, w1, w3, w2, e_w1, e_w3, e_w2):
    tm, tf = TM_FFN, TF_FFN
    nf = D_FF // tf
    assert (T // tm) * nf == 3 * CAST_BLOCKS
    per_e = CAST_BLOCKS // N_EXPERTS

    def block_no(m, f, which):
        return jnp.clip(m * nf + f - which * CAST_BLOCKS, 0, CAST_BLOCKS - 1)

    def w13_idx(which, lead):
        def idx(m, f):
            k = block_no(m, f, which)
            rem = k % per_e
            return lead + (k // per_e, rem // nf, rem % nf)
        return idx

    def w2_idx(lead):
        def idx(m, f):
            k = block_no(m, f, 2)
            return lead + (k // per_e, k % per_e, 0)
        return idx

    b13 = (D_MODEL // 2, tf)
    b2 = (D_FF // per_e, D_MODEL)
    return pl.pallas_call(
        functools.partial(_ffn_kernel, nf=nf),
        out_shape=(jax.ShapeDtypeStruct((T, D_MODEL), F32),
                   jax.ShapeDtypeStruct((T, D_MODEL), BF16),
                   jax.ShapeDtypeStruct((N_EXPERTS, D_MODEL, D_FF), BF16),
                   jax.ShapeDtypeStruct((N_EXPERTS, D_MODEL, D_FF), BF16),
                   jax.ShapeDtypeStruct((N_EXPERTS, D_FF, D_MODEL), BF16)),
        grid=(T // tm, nf),
        in_specs=[
            pl.BlockSpec((tm, D_MODEL), lambda m, f: (m, 0)),
            _layer_spec((1, D_MODEL), layer),
            _mod_spec(layer, 3, tm),
            _mod_spec(layer, 4, tm),
            _mod_spec(layer, 5, tm),
            _layer_spec((1, D_MODEL), layer + 1),
            _mod_spec(layer + 1, 0, tm),
            _mod_spec(layer + 1, 1, tm),
            pl.BlockSpec((None, D_MODEL, tf), lambda m, f: (j, 0, f)),
            pl.BlockSpec((None, D_MODEL, tf), lambda m, f: (j, 0, f)),
            pl.BlockSpec((None, tf, D_MODEL), lambda m, f: (j, f, 0)),
            pl.BlockSpec((None, None) + b13, w13_idx(0, (j,))),
            pl.BlockSpec((None, None) + b13, w13_idx(1, (j,))),
            pl.BlockSpec((None, None) + b2, w2_idx((j,))),
        ],
        out_specs=(pl.BlockSpec((tm, D_MODEL), lambda m, f: (m, 0)),
                   pl.BlockSpec((tm, D_MODEL), lambda m, f: (m, 0)),
                   pl.BlockSpec((None,) + b13, w13_idx(0, ())),
                   pl.BlockSpec((None,) + b13, w13_idx(1, ())),
                   pl.BlockSpec((None,) + b2, w2_idx(()))),
        scratch_shapes=[pltpu.VMEM((tm, D_MODEL), BF16)],
        compiler_params=_cparams(("arbitrary", "arbitrary")),
        name="ffn",
    )(x, norm2_g, mod, mod, mod, norm1_g, mod, mod, w1, w3, w2, e_w1, e_w3, e_w2)


def _router_kernel(x_ref, g_ref, sh_ref, sc_ref, rwa_ref, rwb_ref, rb_ref, h_ref, info_ref, p_ref, cnt_ref,
                   carry_ref, hhi_ref, hlo_ref, upper_ref):
    i = pl.program_id(0)
    tm = TM_ROUTE

    @pl.when(i == 0)
    def _():
        carry_ref[...] = jnp.zeros_like(carry_ref)
        r = lax.broadcasted_iota(I32, (tm, tm), 0)
        c = lax.broadcasted_iota(I32, (tm, tm), 1)
        upper_ref[...] = (r <= c).astype(BF16)

    gain = g_ref[...] * (1.0 + sc_ref[...])
    sh = sh_ref[...]

    def fill(k, carry):
        rows = pl.ds(pl.multiple_of(k * SLAB, SLAB), SLAB)
        y = _modnorm(x_ref[rows, :], gain, sh)
        hi = y.astype(BF16)
        h_ref[rows, :] = y
        hhi_ref[rows, :] = hi
        hlo_ref[rows, :] = (y - hi.astype(F32)).astype(BF16)
        return carry
    lax.fori_loop(0, tm // SLAB, fill, 0, unroll=SLAB_UNROLL)

    t = lax.dot_general(rwa_ref[...], hhi_ref[...], _NT, preferred_element_type=F32)
    u = lax.dot_general(rwb_ref[...], hlo_ref[...], _NT, preferred_element_type=F32)
    logits = t[0:N_EXPERTS] + t[N_EXPERTS:2 * N_EXPERTS] + u[0:N_EXPERTS] + rb_ref[:, 0:1]
    eidx = lax.broadcasted_iota(I32, (N_EXPERTS, tm), 0).astype(F32)
    none = float(N_EXPERTS)
    m1 = jnp.max(logits, axis=0, keepdims=True)
    i1 = jnp.min(jnp.where(logits == m1, eidx, none), axis=0, keepdims=True)
    rest = jnp.where(eidx == i1, -jnp.inf, logits)
    m2 = jnp.max(rest, axis=0, keepdims=True)
    i2 = jnp.min(jnp.where(rest == m2, eidx, none), axis=0, keepdims=True)
    e = jnp.exp(m2 - m1)
    den = 1.0 + e
    sel1 = eidx == i1
    sel2 = eidx == i2
    onehot = (sel1 | sel2).astype(BF16)
    cnt = jnp.dot(onehot, upper_ref[...], preferred_element_type=F32) + carry_ref[:, 0:1]
    r1 = jnp.sum(jnp.where(sel1, cnt, 0.0), axis=0, keepdims=True) - 1.0
    r2 = jnp.sum(jnp.where(sel2, cnt, 0.0), axis=0, keepdims=True) - 1.0
    info_ref[...] = jnp.zeros_like(info_ref)
    info_ref[0:1, :] = i1.astype(I32)
    info_ref[1:2, :] = i2.astype(I32)
    info_ref[2:3, :] = r1.astype(I32)
    info_ref[3:4, :] = r2.astype(I32)
    p_ref[...] = jnp.zeros_like(p_ref)
    p_ref[0:1, :] = 1.0 / den
    p_ref[1:2, :] = e / den
    total = cnt[:, tm - 1:tm]
    carry_ref[...] = jnp.broadcast_to(total, carry_ref.shape)
    cnt_ref[...] = jnp.broadcast_to(total, cnt_ref.shape).astype(I32)


def _router(x, norm_g, mod, layer, j, rw_a, rw_b, rb):
    tm = TM_ROUTE
    nt = T // tm
    return pl.pallas_call(
        _router_kernel,
        out_shape=(jax.ShapeDtypeStruct((T, D_MODEL), F32),
                   jax.ShapeDtypeStruct((nt, 8, tm), I32),
                   jax.ShapeDtypeStruct((nt, 8, tm), F32),
                   jax.ShapeDtypeStruct((N_EXPERTS, 128), I32)),
        grid=(nt,),
        in_specs=[
            pl.BlockSpec((tm, D_MODEL), lambda i: (i, 0)),
            _layer_spec((1, D_MODEL), layer),
            _mod_spec(layer, 3, tm),
            _mod_spec(layer, 4, tm),
            _layer_spec((2 * N_EXPERTS, D_MODEL), j),
            _layer_spec((2 * N_EXPERTS, D_MODEL), j),
            _layer_spec((N_EXPERTS, 128), j),
        ],
        out_specs=(pl.BlockSpec((tm, D_MODEL), lambda i: (i, 0)),
                   pl.BlockSpec((None, 8, tm), lambda i: (i, 0, 0)),
                   pl.BlockSpec((None, 8, tm), lambda i: (i, 0, 0)),
                   pl.BlockSpec((N_EXPERTS, 128), lambda i: (0, 0))),
        scratch_shapes=[pltpu.VMEM((N_EXPERTS, 128), F32), pltpu.VMEM((tm, D_MODEL), BF16),
                        pltpu.VMEM((tm, D_MODEL), BF16), pltpu.VMEM((tm, tm), BF16)],
        compiler_params=_cparams(("arbitrary",)),
        name="router",
    )(x, norm_g, mod, mod, rw_a, rw_b, rb)


def _dispatch_kernel(pos1_ref, pos2_ref, zstart_ref, zcount_ref, h_ref, o_hbm, zrow_ref, sem):
    i = pl.program_id(0)
    td = TD_DISP
    base = i * td

    def row_copy(src_row, dst_row):
        return pltpu.make_async_copy(h_ref.at[pl.ds(src_row, 1)], o_hbm.at[pl.ds(dst_row, 1)], sem)

    def zero_copy(dst_row):
        return pltpu.make_async_copy(zrow_ref, o_hbm.at[pl.ds(dst_row, 1)], sem)

    @pl.when(i == 0)
    def _():
        zrow_ref[...] = jnp.zeros_like(zrow_ref)
        for e in range(N_EXPERTS + 1):
            def zbody(r, carry):
                zero_copy(zstart_ref[e] + r).start()
                return carry
            lax.fori_loop(0, zcount_ref[e], zbody, 0)

            def zwait(r, carry):
                zero_copy(0).wait()
                return carry
            lax.fori_loop(0, zcount_ref[e], zwait, 0)

    for r in range(td):
        row_copy(r, pos1_ref[base + r]).start()
        row_copy(r, pos2_ref[base + r]).start()

    tile_copy = pltpu.make_async_copy(h_ref, o_hbm.at[pl.ds(0, td)], sem)
    tile_copy.wait()
    tile_copy.wait()


def _dispatch(pos1, pos2, zstart, zcount, h):
    td = TD_DISP
    return pl.pallas_call(
        _dispatch_kernel,
        out_shape=jax.ShapeDtypeStruct((ROWS_CAP, D_MODEL), F32),
        grid_spec=pltpu.PrefetchScalarGridSpec(
            num_scalar_prefetch=4,
            grid=(T // td,),
            in_specs=[pl.BlockSpec((td, D_MODEL), lambda i, *_: (i, 0))],
            out_specs=pl.BlockSpec(memory_space=pl.ANY),
            scratch_shapes=[pltpu.VMEM((1, D_MODEL), F32), pltpu.SemaphoreType.DMA],
        ),
        compiler_params=_cparams(("arbitrary",)),
        name="dispatch",
    )(pos1, pos2, zstart, zcount, h)


def _expert_kernel(te_ref, nact_ref, hs_ref, w1_ref, w3_ref, w2_ref, y_ref, hb_ref):
    i = pl.program_id(0)
    f = pl.program_id(1)
    active = i < nact_ref[0]

    @pl.when(f == 0)
    def _():
        y_ref[...] = jnp.zeros_like(y_ref)

    @pl.when(active & (f == 0))
    def _():
        def body(s, carry):
            r = pl.multiple_of(s * SLAB, SLAB)
            hb_ref[pl.ds(r, SLAB), :] = hs_ref[pl.ds(r, SLAB), :].astype(BF16)
            return carry
        lax.fori_loop(0, TR_EXP // SLAB, body, 0, unroll=SLAB_UNROLL)

    @pl.when(active)
    def _():
        _swiglu_accumulate(hb_ref, w1_ref, w3_ref, w2_ref, y_ref, TR_EXP)


def _experts(tile_expert, n_active, hs, j, w1, w3, w2):
    tr, tf = TR_EXP, TF_EXP
    nf = D_FF // tf

    def row_map(i, f, te, na):
        return (jnp.minimum(i, na[0] - 1), 0)

    def out_map(i, f, te, na):
        return (i, 0)

    def w13_map(i, f, te, na):
        return (j, te[i], 0, jnp.where(i < na[0], f, nf - 1))

    def w2_map(i, f, te, na):
        return (j, te[i], jnp.where(i < na[0], f, nf - 1), 0)

    return pl.pallas_call(
        _expert_kernel,
        out_shape=jax.ShapeDtypeStruct((ROWS_CAP, D_MODEL), F32),
        grid_spec=pltpu.PrefetchScalarGridSpec(
            num_scalar_prefetch=2,
            grid=(NT_EXP, nf),
            in_specs=[
                pl.BlockSpec((tr, D_MODEL), row_map),
                pl.BlockSpec((None, None, D_MODEL, tf), w13_map),
                pl.BlockSpec((None, None, D_MODEL, tf), w13_map),
                pl.BlockSpec((None, None, tf, D_MODEL), w2_map),
            ],
            out_specs=pl.BlockSpec((tr, D_MODEL), out_map),
            scratch_shapes=[pltpu.VMEM((tr, D_MODEL), BF16)],
        ),
        compiler_params=_cparams(("arbitrary", "arbitrary")),
        name="experts",
    )(tile_expert, n_active, hs, w1, w3, w2)


def _gather_residual(pos1_ref, pos2_ref, x_ref, p1_ref, p2_ref, g2_ref, y_hbm, buf_ref, sem, emit):
    i = pl.program_id(0)
    n = pl.num_programs(0)
    tc = TC_COMB
    slot = i % 2

    def start_rows(tile, s, r0):
        base = tile * tc
        for d in range(COMB_SLAB):
            r = r0 + d
            pltpu.make_async_copy(y_hbm.at[pl.ds(pos1_ref[base + r], 1)],
                                  buf_ref.at[s, 0, pl.ds(r, 1)], sem.at[s]).start()
            pltpu.make_async_copy(y_hbm.at[pl.ds(pos2_ref[base + r], 1)],
                                  buf_ref.at[s, 1, pl.ds(r, 1)], sem.at[s]).start()

    @pl.when(i == 0)
    def _():
        def body(k, carry):
            start_rows(0, 0, k * COMB_SLAB)
            return carry
        lax.fori_loop(0, tc // COMB_SLAB, body, 0)

    pltpu.make_async_copy(y_hbm.at[pl.ds(0, tc)], buf_ref.at[slot, 0], sem.at[slot]).wait()
    pltpu.make_async_copy(y_hbm.at[pl.ds(0, tc)], buf_ref.at[slot, 1], sem.at[slot]).wait()
    g2 = g2_ref[...]

    def tile_pass(prefetch):
        for k in range(tc // COMB_SLAB):
            r0 = k * COMB_SLAB
            if prefetch:
                start_rows(i + 1, 1 - slot, r0)
            rows = slice(r0, r0 + COMB_SLAB)
            mix = (p1_ref[rows, :] * buf_ref[slot, 0, rows, :]
                   + p2_ref[rows, :] * buf_ref[slot, 1, rows, :])
            emit(rows, x_ref[rows, :] + g2 * mix)

    @pl.when(i + 1 < n)
    def _():
        tile_pass(True)

    @pl.when(i + 1 >= n)
    def _():
        tile_pass(False)


def _combine_kernel(pos1_ref, pos2_ref, x_ref, p1_ref, p2_ref, g2_ref, gn_ref, shn_ref, scn_ref, y_hbm,
                    o_ref, hn_ref, buf_ref, sem):
    gain_n, shn = gn_ref[...] * (1.0 + scn_ref[...]), shn_ref[...]

    def emit(rows, xn):
        o_ref[rows, :] = xn
        hn_ref[rows, :] = _modnorm(xn, gain_n, shn).astype(BF16)

    _gather_residual(pos1_ref, pos2_ref, x_ref, p1_ref, p2_ref, g2_ref, y_hbm, buf_ref, sem, emit)


def _combine_final_kernel(pos1_ref, pos2_ref, x_ref, p1_ref, p2_ref, g2_ref, gf_ref, y_hbm,
                          yp_ref, ys_ref, buf_ref, sem, yt_ref, *, n_ctx_tiles):
    gf = gf_ref[...]

    def emit(rows, xn):
        yt_ref[rows, :] = xn * lax.rsqrt(jnp.mean(xn * xn, axis=-1, keepdims=True) + EPS) * gf

    _gather_residual(pos1_ref, pos2_ref, x_ref, p1_ref, p2_ref, g2_ref, y_hbm, buf_ref, sem, emit)
    i = pl.program_id(0)

    @pl.when(i < n_ctx_tiles)
    def _():
        yp_ref[...] = yt_ref[...]

    @pl.when(i >= n_ctx_tiles)
    def _():
        ys_ref[...] = yt_ref[...]


def _combine_call(body, extra_specs, out_shape, out_specs, name, args, extra_scratch=()):
    tc = TC_COMB
    row_tile = pl.BlockSpec((tc, D_MODEL), lambda i, *_: (i, 0))
    col = pl.BlockSpec((tc, 1), lambda i, *_: (i, 0))
    return pl.pallas_call(
        body,
        out_shape=out_shape,
        grid_spec=pltpu.PrefetchScalarGridSpec(
            num_scalar_prefetch=2,
            grid=(T // tc,),
            in_specs=[row_tile, col, col] + extra_specs + [pl.BlockSpec(memory_space=pl.ANY)],
            out_specs=out_specs,
            scratch_shapes=[pltpu.VMEM((2, 2, tc, D_MODEL), F32), pltpu.SemaphoreType.DMA((2,))]
            + list(extra_scratch),
        ),
        compiler_params=_cparams(("arbitrary",)),
        name=name,
    )(*args)


def _combine(pos1, pos2, x, p1, p2, mod, layer, norm1_g, y):
    tc = TC_COMB
    row_tile = pl.BlockSpec((tc, D_MODEL), lambda i, *_: (i, 0))
    return _combine_call(
        _combine_kernel,
        [_mod_spec(layer, 5, tc), _layer_spec((1, D_MODEL), layer + 1),
         _mod_spec(layer + 1, 0, tc), _mod_spec(layer + 1, 1, tc)],
        (jax.ShapeDtypeStruct((T, D_MODEL), F32), jax.ShapeDtypeStruct((T, D_MODEL), BF16)),
        (row_tile, row_tile), "combine",
        (pos1, pos2, x, p1, p2, mod, norm1_g, mod, mod, y))


def _combine_final(pos1, pos2, x, p1, p2, mod, layer, final_g, y):
    tc = TC_COMB
    nct = TP // tc
    return _combine_call(
        functools.partial(_combine_final_kernel, n_ctx_tiles=nct),
        [_mod_spec(layer, 5, tc), pl.BlockSpec((1, D_MODEL), lambda i, *_: (0, 0))],
        (jax.ShapeDtypeStruct((TP, D_MODEL), F32), jax.ShapeDtypeStruct((TS, D_MODEL), F32)),
        (pl.BlockSpec((tc, D_MODEL), lambda i, *_: (jnp.minimum(i, nct - 1), 0)),
         pl.BlockSpec((tc, D_MODEL), lambda i, *_: (jnp.maximum(i - nct, 0), 0))),
        "combine_final",
        (pos1, pos2, x, p1, p2, mod, final_g, y),
        extra_scratch=[pltpu.VMEM((tc, D_MODEL), F32)])


def _moe(x, norm_g, mod, layer, j, rw, rb, w1, w3, w2, norm1_g, final_g):
    h, info, probs, counts = _router(x, norm_g, mod, layer, j, rw[0], rw[1], rb)
    counts = counts[:, 0]
    cap = ((counts + TR_EXP - 1) // TR_EXP) * TR_EXP
    ends = jnp.cumsum(cap)
    starts = ends - cap
    e1 = info[:, 0, :].reshape(T)
    e2 = info[:, 1, :].reshape(T)
    pos1 = starts[e1] + info[:, 2, :].reshape(T)
    pos2 = starts[e2] + info[:, 3, :].reshape(T)
    tile_end = ends // TR_EXP
    n_active = tile_end[-1:]
    tiles = jnp.minimum(jnp.arange(NT_EXP, dtype=I32), n_active[0] - 1)
    tile_expert = jnp.sum((tiles[:, None] >= tile_end[None, :]).astype(I32), axis=1)
    zstart = jnp.concatenate([starts + counts, ends[-1:]])
    zcount = jnp.concatenate([cap - counts, ROWS_CAP - ends[-1:]])
    hs = _dispatch(pos1, pos2, zstart, zcount, h)
    y = _experts(tile_expert, n_active, hs, 0, w1[None], w3[None], w2[None])
    p1 = probs[:, 0, :].reshape(T, 1)
    p2 = probs[:, 1, :].reshape(T, 1)
    if layer == DEPTH - 1:
        return _combine_final(pos1, pos2, x, p1, p2, mod, layer, final_g, y)
    return _combine(pos1, pos2, x, p1, p2, mod, layer, norm1_g, y)


def _grid_pos_embed():
    rows = DEC_SEQ // GRID_W
    r, col = jnp.meshgrid(jnp.arange(rows, dtype=F32), jnp.arange(GRID_W, dtype=F32), indexing='ij')
    r = r.reshape(-1)
    col = col.reshape(-1)
    quarter = D_MODEL // 4
    freq = jnp.exp(-math.log(10000.0) * jnp.arange(quarter, dtype=F32) / quarter)
    ar = r[:, None] * freq
    ac = col[:, None] * freq
    return jnp.concatenate([jnp.sin(ar), jnp.cos(ar), jnp.sin(ac), jnp.cos(ac)], axis=-1)


def kernel(x_prompt, x_sample, state_gla, c, c_ctx, norm1_g, norm2_g, w_mod, b_mod, w_in,
           sgu_ln_g, sgu_ln_b, w_spatial, b_spatial, gla_a2, gla_ab, gla_norm_g,
           w_branch_a, w_branch_b, w_out, ffn_w1, ffn_w3, ffn_w2,
           moe_router, moe_router_b, moe_w1, moe_w3, moe_w2, final_g):
    cond = jnp.zeros((COND_PAD, D_MODEL), F32).at[0].set(c_ctx).at[1:N_COND].set(c)
    mod = _mod_table(cond, w_mod, b_mod).reshape(DEPTH, COND_PAD, 6, 1, D_MODEL)

    lr_end = W_IN_LR + 2 * GLA_RANK
    w_main = jnp.concatenate([w_in[:, :, lr_end:], w_in[:, :, :W_IN_LR]], axis=2).astype(BF16)
    w_lr = jnp.pad(w_in[:, :, W_IN_LR:lr_end], ((0, 0), (0, 0), (0, LR_PAD - 2 * GLA_RANK))).astype(BF16)
    norm1 = norm1_g.reshape(DEPTH, 1, D_MODEL)
    norm2 = norm2_g.reshape(DEPTH, 1, D_MODEL)
    ln_g = sgu_ln_g.reshape(DEPTH, 1, D_A)
    ln_b = sgu_ln_b.reshape(DEPTH, 1, D_A)
    w_s = w_spatial.astype(BF16)
    bs_full = jnp.repeat(jnp.swapaxes(b_spatial, 1, 2), A_GROUP, axis=2)
    a2_pad = jnp.zeros((DEPTH, LR_PAD, GLA_DK), F32)
    a2f = a2_pad.at[:, :GLA_RANK].set(gla_a2[:, 0]).astype(BF16)
    a2b = a2_pad.at[:, GLA_RANK:2 * GLA_RANK].set(gla_a2[:, 1]).astype(BF16)
    ab = gla_ab.reshape(DEPTH, 2, 1, GLA_DK)
    gla_g = gla_norm_g.reshape(DEPTH, 1, GLA_DV)
    w_a = w_branch_a.astype(BF16)
    w_b = w_branch_b.astype(BF16)
    w_o = w_out.astype(BF16)
    f_w1, f_w3, f_w2 = ffn_w1.astype(BF16), ffn_w3.astype(BF16), ffn_w2.astype(BF16)
    rw_t = jnp.swapaxes(moe_router, 1, 2)
    rw_hi = rw_t.astype(BF16)
    rw_lo = (rw_t - rw_hi.astype(F32)).astype(BF16)
    rw = (jnp.concatenate([rw_hi, rw_lo], axis=1),
          jnp.concatenate([rw_hi, jnp.zeros_like(rw_hi)], axis=1))
    rb = jnp.broadcast_to(moe_router_b[:, :, None], moe_router_b.shape + (128,))

    x, h = _embed(x_prompt.reshape(TP, D_MODEL), x_sample.reshape(TS, D_MODEL), _grid_pos_embed(),
                  norm1, mod)
    new_state = jnp.zeros((BATCH, DEPTH, 2, N_GLA_HEADS, GLA_DK_HEAD, GLA_DV_HEAD), F32)
    for l in range(DEPTH):
        proj, lr = _proj(h, l, w_main, w_lr)
        a = _sgu(proj, l, ln_g, ln_b, w_s, bs_full)
        o_f, o_b, new_state = _gla(proj, lr, a2f, a2b, ab, state_gla, new_state, l)
        x = _mixout(a, o_f, o_b, proj, gla_g, x, mod, l, w_a, w_b, w_o)
        j = l // 2
        if l % 2 == 0:
            x, h, m_w1, m_w3, m_w2 = _ffn(x, norm2, norm1, mod, l, j, f_w1, f_w3, f_w2,
                                          moe_w1, moe_w3, moe_w2)
        elif l < DEPTH - 1:
            x, h = _moe(x, norm2, mod, l, j, rw, rb, m_w1, m_w3, m_w2, norm1, final_g[None])
        else:
            y_prompt, y_sample = _moe(x, norm2, mod, l, j, rw, rb, m_w1, m_w3, m_w2, norm1,
                                      final_g[None])

    return (y_prompt.reshape(BATCH, SEQ, D_MODEL), y_sample.reshape(DEC_BATCH, DEC_SEQ, D_MODEL),
            new_state)
```

```python
import functools
import math

import numpy as np
import jax
import jax.numpy as jnp
from jax import lax
from jax.experimental import pallas as pl
from jax.experimental.pallas import tpu as pltpu

F32 = jnp.float32
BF16 = jnp.bfloat16
I32 = jnp.int32

D_MODEL = 2048
BATCH = 32
SEQ = 256
DEPTH = 4
DEC_BATCH = 8
DEC_SEQ = 2048
GRID_W = 64
CHUNK_A = 128
D_A = 1024
N_A_GROUPS = 8
A_GROUP = D_A // N_A_GROUPS
N_GLA_HEADS = 4
GLA_DK = D_MODEL // 4
GLA_DV = D_MODEL // 2
GLA_DK_HEAD = GLA_DK // N_GLA_HEADS
GLA_DV_HEAD = GLA_DV // N_GLA_HEADS
GLA_RANK = 16
GLA_TAU = 16.0
GLA_CHUNK = 64
D_FF = 5632
N_EXPERTS = 8
TOP_K = 2
EPS = 1e-6

TP = BATCH * SEQ
TS = DEC_BATCH * DEC_SEQ
T = TP + TS
N_COND = 1 + DEC_BATCH
COND_PAD = 16

W_IN_LR = 2 * D_A + 2 * GLA_DK + 2 * GLA_DV
C_GA, C_GB, C_U, C_V, C_Q, C_K, C_VG, C_R = 0, 2048, 4096, 5120, 6144, 6656, 7168, 8192
N_MAIN = 9216
LR_PAD = 128

VMEM_LIMIT = 56 * 1024 * 1024

TM_PROJ, TN_PROJ = 1024, 2304
TM_EMBED = 512
TM_SGU = 512
TB_GLA = 256
TM_MIX = 256
TM_FFN, TF_FFN = 512, 512
TM_ROUTE = 512
TD_DISP = 256
TR_EXP, TF_EXP = 512, 512
TC_COMB = 256
SLAB = 32
SLAB_UNROLL = 4
RS_SWIGLU = 512
CAST_BLOCKS = N_EXPERTS * 2 * (D_FF // TF_FFN)
COMB_SLAB = 16

NT_EXP = (TOP_K * T) // TR_EXP + N_EXPERTS
ROWS_CAP = NT_EXP * TR_EXP


def _cparams(sem):
    return pltpu.CompilerParams(dimension_semantics=sem, vmem_limit_bytes=VMEM_LIMIT)


def _mod_row(i, tm):
    return jnp.maximum((i * tm) // DEC_SEQ - (TP // DEC_SEQ - 1), 0)


def _mod_spec(layer, j, tm, width=D_MODEL):
    return pl.BlockSpec((None, None, None, 1, width),
                        lambda m, *_: (layer, _mod_row(m, tm), j, 0, 0))


def _layer_spec(shape, layer):
    zeros = (0,) * len(shape)
    return pl.BlockSpec((None,) + tuple(shape), lambda *_: (layer,) + zeros)


def _modnorm(x, gain, sh):
    return x * lax.rsqrt(jnp.mean(x * x, axis=-1, keepdims=True) + EPS) * gain + sh


def _fill_modnorm(x_ref, g, sc, sh, h_ref, rows):
    gain = g * (1.0 + sc)

    def body(i, carry):
        r = pl.multiple_of(i * SLAB, SLAB)
        h_ref[pl.ds(r, SLAB), :] = _modnorm(x_ref[pl.ds(r, SLAB), :], gain, sh).astype(h_ref.dtype)
        return carry
    lax.fori_loop(0, rows // SLAB, body, 0, unroll=SLAB_UNROLL)


def _silu(x):
    return x * jax.nn.sigmoid(x)


def _gelu_tanh(x):
    return jax.nn.gelu(x, approximate=True)


def _log_sigmoid(x):
    return jnp.minimum(x, 0.0) - jnp.log(1.0 + jnp.exp(-jnp.abs(x)))


def _swiglu_accumulate(h_ref, w1_ref, w3_ref, w2_ref, acc_ref, rows):
    for r in range(rows // RS_SWIGLU):
        sl = slice(r * RS_SWIGLU, (r + 1) * RS_SWIGLU)
        h = h_ref[sl, :]
        u = _silu(jnp.dot(h, w1_ref[...], preferred_element_type=F32)) * jnp.dot(
            h, w3_ref[...], preferred_element_type=F32)
        acc_ref[sl, :] += jnp.dot(u.astype(BF16), w2_ref[...], preferred_element_type=F32)


def _embed_kernel(xp_ref, xs_ref, emb_ref, g_ref, sh_ref, sc_ref, o_ref, h_ref, *, n_ctx_tiles):
    i = pl.program_id(0)

    @pl.when(i < n_ctx_tiles)
    def _():
        o_ref[...] = xp_ref[...]

    @pl.when(i >= n_ctx_tiles)
    def _():
        o_ref[...] = xs_ref[...] + emb_ref[...]

    _fill_modnorm(o_ref, g_ref[...], sc_ref[...], sh_ref[...], h_ref, TM_EMBED)


def _embed(xp, xs, emb, norm_g, mod):
    tm = TM_EMBED
    nct = TP // tm
    per_seq = DEC_SEQ // tm
    return pl.pallas_call(
        functools.partial(_embed_kernel, n_ctx_tiles=nct),
        out_shape=(jax.ShapeDtypeStruct((T, D_MODEL), F32),
                   jax.ShapeDtypeStruct((T, D_MODEL), BF16)),
        grid=(T // tm,),
        in_specs=[
            pl.BlockSpec((tm, D_MODEL), lambda i: (jnp.minimum(i, nct - 1), 0)),
            pl.BlockSpec((tm, D_MODEL), lambda i: (jnp.maximum(i - nct, 0), 0)),
            pl.BlockSpec((tm, D_MODEL), lambda i: (jnp.maximum(i - nct, 0) % per_seq, 0)),
            _layer_spec((1, D_MODEL), 0),
            _mod_spec(0, 0, tm),
            _mod_spec(0, 1, tm),
        ],
        out_specs=(pl.BlockSpec((tm, D_MODEL), lambda i: (i, 0)),
                   pl.BlockSpec((tm, D_MODEL), lambda i: (i, 0))),
        compiler_params=_cparams(("parallel",)),
        name="embed",
    )(xp, xs, emb, norm_g, mod, mod)


def _mod_kernel(c_ref, w_ref, b_ref, o_ref):
    c = c_ref[...]
    s = _silu(c).astype(BF16)
    o_ref[...] = jnp.dot(s, w_ref[...].astype(BF16), preferred_element_type=F32) + b_ref[...]


def _mod_table(cond, w_mod, b_mod):
    tn = 1024
    n6 = 6 * D_MODEL
    return pl.pallas_call(
        _mod_kernel,
        out_shape=jax.ShapeDtypeStruct((DEPTH, COND_PAD, n6), F32),
        grid=(DEPTH, n6 // tn),
        in_specs=[
            pl.BlockSpec((COND_PAD, D_MODEL), lambda l, n: (0, 0)),
            pl.BlockSpec((None, D_MODEL, tn), lambda l, n: (l, 0, n)),
            pl.BlockSpec((None, 1, tn), lambda l, n: (l, 0, n)),
        ],
        out_specs=pl.BlockSpec((None, COND_PAD, tn), lambda l, n: (l, 0, n)),
        compiler_params=_cparams(("parallel", "parallel")),
        name="mod_table",
    )(cond, w_mod, b_mod.reshape(DEPTH, 1, n6))


def _proj_kernel(h_ref, w_ref, wlr_ref, o_ref, lr_ref):
    @pl.when(pl.program_id(1) == 0)
    def _():
        lr_ref[...] = jnp.dot(h_ref[...], wlr_ref[...], preferred_element_type=F32)

    o_ref[...] = jnp.dot(h_ref[...], w_ref[...], preferred_element_type=F32).astype(o_ref.dtype)


def _proj(h, layer, w_main, w_lr):
    tm, tn = TM_PROJ, TN_PROJ
    return pl.pallas_call(
        _proj_kernel,
        out_shape=(jax.ShapeDtypeStruct((T, N_MAIN), BF16),
                   jax.ShapeDtypeStruct((T, LR_PAD), F32)),
        grid=(T // tm, N_MAIN // tn),
        in_specs=[
            pl.BlockSpec((tm, D_MODEL), lambda m, n: (m, 0)),
            pl.BlockSpec((None, D_MODEL, tn), lambda m, n: (layer, 0, n)),
            _layer_spec((D_MODEL, LR_PAD), layer),
        ],
        out_specs=(pl.BlockSpec((tm, tn), lambda m, n: (m, n)),
                   pl.BlockSpec((tm, LR_PAD), lambda m, n: (m, 0))),
        compiler_params=_cparams(("parallel", "arbitrary")),
        name="proj",
    )(h, w_main, w_lr)


def _sgu_kernel(pu_ref, pv_ref, lng_ref, lnb_ref, ws_ref, bs_ref, a_ref, vn_ref):
    gv = _gelu_tanh(pv_ref[...].astype(F32))
    mu = jnp.mean(gv, axis=-1, keepdims=True)
    d = gv - mu
    var = jnp.mean(d * d, axis=-1, keepdims=True)
    vn_ref[...] = (d * lax.rsqrt(var + EPS) * lng_ref[...] + lnb_ref[...]).astype(BF16)
    for c in range(TM_SGU // CHUNK_A):
        rows = slice(c * CHUNK_A, (c + 1) * CHUNK_A)
        for g in range(N_A_GROUPS):
            cols = slice(g * A_GROUP, (g + 1) * A_GROUP)
            f = jnp.dot(ws_ref[g], vn_ref[rows, cols], preferred_element_type=F32) + bs_ref[:, cols]
            gu = _gelu_tanh(pu_ref[rows, cols].astype(F32))
            a_ref[rows, cols] = (gu * f).astype(BF16)


def _sgu(proj, layer, ln_g, ln_b, w_s, bs_full):
    tm = TM_SGU
    return pl.pallas_call(
        _sgu_kernel,
        out_shape=jax.ShapeDtypeStruct((T, D_A), BF16),
        grid=(T // tm,),
        in_specs=[
            pl.BlockSpec((tm, D_A), lambda i: (i, C_U // D_A)),
            pl.BlockSpec((tm, D_A), lambda i: (i, C_V // D_A)),
            _layer_spec((1, D_A), layer),
            _layer_spec((1, D_A), layer),
            _layer_spec((N_A_GROUPS, CHUNK_A, CHUNK_A), layer),
            _layer_spec((CHUNK_A, D_A), layer),
        ],
        out_specs=pl.BlockSpec((tm, D_A), lambda i: (i, 0)),
        scratch_shapes=[pltpu.VMEM((tm, D_A), BF16)],
        compiler_params=_cparams(("parallel",)),
        name="sgu",
    )(proj, proj, ln_g, ln_b, w_s, bs_full)


def _split3(x):
    hi = x.astype(BF16)
    r = x - hi.astype(F32)
    mid = r.astype(BF16)
    lo = (r - mid.astype(F32)).astype(BF16)
    return hi, mid, lo


def _tri_cumsum(tri, x):
    hi, mid, lo = _split3(x)
    return (jnp.dot(tri, hi, preferred_element_type=F32)
            + jnp.dot(tri, mid, preferred_element_type=F32)
            + jnp.dot(tri, lo, preferred_element_type=F32))


_NT = (((1,), (1,)), ((), ()))
_TN = (((0,), (0,)), ((), ()))


def _gla_decay(lr_ref, a2_ref, ab_ref, tb, backward):
    z = jnp.dot(lr_ref[...].astype(BF16), a2_ref[...], preferred_element_type=F32) + ab_ref[...]
    la = _log_sigmoid(z) / GLA_TAU
    r = lax.broadcasted_iota(I32, (tb, tb), 0)
    c = lax.broadcasted_iota(I32, (tb, tb), 1)
    same = (r // GLA_CHUNK) == (c // GLA_CHUNK)
    tri = (same & ((c >= r) if backward else (c <= r))).astype(BF16)
    return _tri_cumsum(tri, la)


def _gla_block(dirs, tb):
    nchunk = tb // GLA_CHUNK
    scale = GLA_DK_HEAD ** -0.5
    rr = lax.broadcasted_iota(I32, (GLA_CHUNK, GLA_CHUNK), 0)
    cc = lax.broadcasted_iota(I32, (GLA_CHUNK, GLA_CHUNK), 1)
    decays = [_gla_decay(d[0], d[1], d[2], tb, d[8]) for d in dirs]
    per_dir = []
    for (lr_ref, a2_ref, ab_ref, q_ref, k_ref, v_ref, o_ref, st_ref, backward), b in zip(dirs, decays):
        units = []
        per_dir.append(units)
        eb = jnp.exp(b)
        enb = jnp.exp(-b)
        mask = (cc >= rr) if backward else (cc <= rr)
        order = range(nchunk - 1, -1, -1) if backward else range(nchunk)
        for ci in order:
            rows = slice(ci * GLA_CHUNK, (ci + 1) * GLA_CHUNK)
            last = ci * GLA_CHUNK if backward else (ci + 1) * GLA_CHUNK - 1
            btot = b[last:last + 1, :]
            eke = jnp.exp(btot - b[rows, :])
            gdec = jnp.exp(btot)
            for h in range(N_GLA_HEADS):
                kc = slice(h * GLA_DK_HEAD, (h + 1) * GLA_DK_HEAD)
                vc = slice(h * GLA_DV_HEAD, (h + 1) * GLA_DV_HEAD)
                q = q_ref[rows, kc].astype(F32) * scale
                k = k_ref[rows, kc].astype(F32)
                qd = (q * eb[rows, kc]).astype(BF16)
                kd = (k * enb[rows, kc]).astype(BF16)
                ke = (k * eke[:, kc]).astype(BF16)
                vv = v_ref[rows, vc]
                att = lax.dot_general(qd, kd, _NT, preferred_element_type=F32)
                att = jnp.where(mask, att, 0.0).astype(BF16)
                dst = lax.dot_general(vv, ke, _TN, preferred_element_type=F32)
                units.append(dict(o_ref=o_ref, st_ref=st_ref, rows=rows, vc=vc, h=h, qd=qd, vv=vv,
                                  att=att, dst=dst, gdec=gdec[:, kc]))
    units = [u for group in zip(*per_dir) for u in group]
    for u in units:
        u["o_intra"] = jnp.dot(u["att"], u["vv"], preferred_element_type=F32)
    states = {}
    for u in units:
        key = (id(u["st_ref"]), u["h"])
        st = states[key] if key in states else u["st_ref"][u["h"]]
        u["o_ref"][u["rows"], u["vc"]] = u["o_intra"] + lax.dot_general(
            u["qd"], st.astype(BF16), _NT, preferred_element_type=F32)
        states[key] = st * u["gdec"] + u["dst"]
    for (_, _, _, _, _, _, _, st_ref, _) in dirs:
        for h in range(N_GLA_HEADS):
            st_ref[h] = states[(id(st_ref), h)]


def _gla_kernel(fwd_ref, bwd_ref, flag_ref, seq_ref,
                lrf_ref, lrb_ref, a2f_ref, a2b_ref, abf_ref, abb_ref,
                qf_ref, kf_ref, vf_ref, qb_ref, kb_ref, vb_ref, s0_ref, sprev_ref,
                of_ref, ob_ref, sfin_ref, stf_ref, stb_ref):
    del sprev_ref
    i = pl.program_id(0)
    first = (flag_ref[i] & 1) == 1
    last = (flag_ref[i] & 2) == 2
    is_ctx = seq_ref[i] < BATCH

    @pl.when(first & is_ctx)
    def _():
        stf_ref[...] = jnp.zeros_like(stf_ref)
        stb_ref[...] = jnp.zeros_like(stb_ref)

    @pl.when(first & jnp.logical_not(is_ctx))
    def _():
        for h in range(N_GLA_HEADS):
            stf_ref[h] = s0_ref[0, h].T
            stb_ref[h] = s0_ref[1, h].T

    _gla_block([(lrf_ref, a2f_ref, abf_ref, qf_ref, kf_ref, vf_ref, of_ref, stf_ref, False),
                (lrb_ref, a2b_ref, abb_ref, qb_ref, kb_ref, vb_ref, ob_ref, stb_ref, True)], TB_GLA)

    @pl.when(last & is_ctx)
    def _():
        for h in range(N_GLA_HEADS):
            sfin_ref[0, h] = stf_ref[h].T
            sfin_ref[1, h] = stb_ref[h].T


def _gla_tables():
    tb = TB_GLA
    fwd, bwd, flag, seq = [], [], [], []
    for s, (row0, seqlen) in enumerate([(b * SEQ, SEQ) for b in range(BATCH)]
                                       + [(TP + b * DEC_SEQ, DEC_SEQ) for b in range(DEC_BATCH)]):
        nb = seqlen // tb
        for j in range(nb):
            fwd.append(row0 // tb + j)
            bwd.append(row0 // tb + nb - 1 - j)
            flag.append((1 if j == 0 else 0) | (2 if j == nb - 1 else 0))
            seq.append(s)
    return tuple(np.asarray(t, np.int32) for t in (fwd, bwd, flag, seq))


def _gla(proj, lr, a2f, a2b, ab, s0, s_all, layer):
    tb = TB_GLA
    fwd_t, bwd_t, flag_t, seq_t = _gla_tables()
    nsteps = fwd_t.shape[0]
    n_prefetch = 4

    def fwd(i, fw, bw, fl, sq):
        return fw[i]

    def bwd(i, fw, bw, fl, sq):
        return bw[i]

    def col_spec(rowfn, width, col):
        return pl.BlockSpec((tb, width), lambda i, *t: (rowfn(i, *t), col // width))

    state_block = (2, N_GLA_HEADS, GLA_DK_HEAD, GLA_DV_HEAD)
    in_specs = [
        pl.BlockSpec((tb, LR_PAD), lambda i, *t: (fwd(i, *t), 0)),
        pl.BlockSpec((tb, LR_PAD), lambda i, *t: (bwd(i, *t), 0)),
        _layer_spec((LR_PAD, GLA_DK), layer),
        _layer_spec((LR_PAD, GLA_DK), layer),
        pl.BlockSpec((None, None, 1, GLA_DK), lambda i, *t: (layer, 0, 0, 0)),
        pl.BlockSpec((None, None, 1, GLA_DK), lambda i, *t: (layer, 1, 0, 0)),
        col_spec(fwd, GLA_DK, C_Q), col_spec(fwd, GLA_DK, C_K), col_spec(fwd, GLA_DV, C_VG),
        col_spec(bwd, GLA_DK, C_Q), col_spec(bwd, GLA_DK, C_K), col_spec(bwd, GLA_DV, C_VG),
        pl.BlockSpec((None, None) + state_block,
                     lambda i, fw, bw, fl, sq: (jnp.maximum(sq[i] - BATCH, 0), layer, 0, 0, 0, 0)),
        pl.BlockSpec(memory_space=pl.ANY),
    ]
    out_shape = (jax.ShapeDtypeStruct((T, GLA_DV), F32),
                 jax.ShapeDtypeStruct((T, GLA_DV), F32),
                 jax.ShapeDtypeStruct((BATCH, DEPTH) + state_block, F32))
    out_specs = (pl.BlockSpec((tb, GLA_DV), lambda i, *t: (fwd(i, *t), 0)),
                 pl.BlockSpec((tb, GLA_DV), lambda i, *t: (bwd(i, *t), 0)),
                 pl.BlockSpec((None, None) + state_block,
                              lambda i, fw, bw, fl, sq: (jnp.minimum(sq[i], BATCH - 1), layer, 0, 0, 0, 0)))
    return pl.pallas_call(
        _gla_kernel,
        out_shape=out_shape,
        grid_spec=pltpu.PrefetchScalarGridSpec(
            num_scalar_prefetch=n_prefetch,
            grid=(nsteps,),
            in_specs=in_specs,
            out_specs=out_specs,
            scratch_shapes=[pltpu.VMEM((N_GLA_HEADS, GLA_DV_HEAD, GLA_DK_HEAD), F32),
                            pltpu.VMEM((N_GLA_HEADS, GLA_DV_HEAD, GLA_DK_HEAD), F32)],
        ),
        input_output_aliases={n_prefetch + len(in_specs) - 1: 2},
        compiler_params=_cparams(("arbitrary",)),
        name="gla",
    )(fwd_t, bwd_t, flag_t, seq_t, lr, lr, a2f, a2b, ab, ab,
      proj, proj, proj, proj, proj, proj, s0, s_all)


def _mixout_kernel(a_ref, of_ref, ob_ref, pr_ref, gn_ref, ga_ref, gb_ref, x_ref, g1_ref,
                   wa_ref, wb_ref, wo_ref, out_ref, o_ref):
    for h in range(N_GLA_HEADS):
        vc = slice(h * GLA_DV_HEAD, (h + 1) * GLA_DV_HEAD)
        o = of_ref[:, vc] + ob_ref[:, vc]
        o = o * lax.rsqrt(jnp.mean(o * o, axis=-1, keepdims=True) + EPS)
        o_ref[:, vc] = (o * gn_ref[:, vc] * _silu(pr_ref[:, vc].astype(F32))).astype(BF16)
    ya = jnp.dot(a_ref[...], wa_ref[...], preferred_element_type=F32)
    yb = jnp.dot(o_ref[...], wb_ref[...], preferred_element_type=F32)
    merged = (jax.nn.sigmoid(ga_ref[...].astype(F32)) * ya
              + jax.nn.sigmoid(gb_ref[...].astype(F32)) * yb).astype(BF16)
    y = jnp.dot(merged, wo_ref[...], preferred_element_type=F32)
    out_ref[...] = x_ref[...] + g1_ref[...] * y


def _resident_spec(shape, layer):
    zeros = (0,) * len(shape)
    return pl.BlockSpec((None,) + tuple(shape), lambda *_: (layer,) + zeros,
                        pipeline_mode=pl.Buffered(1))


def _mixout(a, o_f, o_b, proj, gla_g, x, mod, layer, w_a, w_b, w_o):
    tm = TM_MIX
    return pl.pallas_call(
        _mixout_kernel,
        out_shape=jax.ShapeDtypeStruct((T, D_MODEL), F32),
        grid=(T // tm,),
        in_specs=[
            pl.BlockSpec((tm, D_A), lambda m: (m, 0)),
            pl.BlockSpec((tm, GLA_DV), lambda m: (m, 0)),
            pl.BlockSpec((tm, GLA_DV), lambda m: (m, 0)),
            pl.BlockSpec((tm, GLA_DV), lambda m: (m, C_R // GLA_DV)),
            _layer_spec((1, GLA_DV), layer),
            pl.BlockSpec((tm, D_MODEL), lambda m: (m, C_GA // D_MODEL)),
            pl.BlockSpec((tm, D_MODEL), lambda m: (m, C_GB // D_MODEL)),
            pl.BlockSpec((tm, D_MODEL), lambda m: (m, 0)),
            _mod_spec(layer, 2, tm),
            _resident_spec((D_A, D_MODEL), layer),
            _resident_spec((GLA_DV, D_MODEL), layer),
            _resident_spec((D_MODEL, D_MODEL), layer),
        ],
        out_specs=pl.BlockSpec((tm, D_MODEL), lambda m: (m, 0)),
        scratch_shapes=[pltpu.VMEM((tm, GLA_DV), BF16)],
        compiler_params=_cparams(("parallel",)),
        name="mixout",
    )(a, o_f, o_b, proj, gla_g, proj, proj, x, mod, w_a, w_b, w_o)


def _ffn_kernel(x_ref, g_ref, sh_ref, sc_ref, g2_ref, gn_ref, shn_ref, scn_ref, w1_ref, w3_ref, w2_ref,
                e1_ref, e3_ref, e2_ref, o_ref, hn_ref, c1_ref, c3_ref, c2_ref, h_ref, *, nf):
    f = pl.program_id(1)

    s = pl.program_id(0) * nf + f

    @pl.when(s < CAST_BLOCKS)
    def _():
        c1_ref[...] = e1_ref[...].astype(BF16)

    @pl.when((s >= CAST_BLOCKS) & (s < 2 * CAST_BLOCKS))
    def _():
        c3_ref[...] = e3_ref[...].astype(BF16)

    @pl.when(s >= 2 * CAST_BLOCKS)
    def _():
        c2_ref[...] = e2_ref[...].astype(BF16)

    @pl.when(f == 0)
    def _():
        _fill_modnorm(x_ref, g_ref[...], sc_ref[...], sh_ref[...], h_ref, TM_FFN)
        o_ref[...] = jnp.zeros_like(o_ref)

    _swiglu_accumulate(h_ref, w1_ref, w3_ref, w2_ref, o_ref, TM_FFN)

    @pl.when(f == nf - 1)
    def _():
        g2 = g2_ref[...]
        gain_n, shn = gn_ref[...] * (1.0 + scn_ref[...]), shn_ref[...]

        def body(i, carry):
            r = pl.multiple_of(i * SLAB, SLAB)
            xn = x_ref[pl.ds(r, SLAB), :] + g2 * o_ref[pl.ds(r, SLAB), :]
            o_ref[pl.ds(r, SLAB), :] = xn
            hn_ref[pl.ds(r, SLAB), :] = _modnorm(xn, gain_n, shn).astype(BF16)
            return carry
        lax.fori_loop(0, TM_FFN // SLAB, body, 0, unroll=SLAB_UNROLL)


def _ffn(x, norm2_g, norm1_g, mod, layer, j, w1, w3, w2, e_w1, e_w3, e_w2):
    tm, tf = TM_FFN, TF_FFN
    nf = D_FF // tf
    assert (T // tm) * nf == 3 * CAST_BLOCKS
    per_e = CAST_BLOCKS // N_EXPERTS

    def block_no(m, f, which):
        return jnp.clip(m * nf + f - which * CAST_BLOCKS, 0, CAST_BLOCKS - 1)

    def w13_idx(which, lead):
        def idx(m, f):
            k = block_no(m, f, which)
            rem = k % per_e
            return lead + (k // per_e, rem // nf, rem % nf)
        return idx

    def w2_idx(lead):
        def idx(m, f):
            k = block_no(m, f, 2)
            return lead + (k // per_e, k % per_e, 0)
        return idx

    b13 = (D_MODEL // 2, tf)
    b2 = (D_FF // per_e, D_MODEL)
    return pl.pallas_call(
        functools.partial(_ffn_kernel, nf=nf),
        out_shape=(jax.ShapeDtypeStruct((T, D_MODEL), F32),
                   jax.ShapeDtypeStruct((T, D_MODEL), BF16),
                   jax.ShapeDtypeStruct((N_EXPERTS, D_MODEL, D_FF), BF16),
                   jax.ShapeDtypeStruct((N_EXPERTS, D_MODEL, D_FF), BF16),
                   jax.ShapeDtypeStruct((N_EXPERTS, D_FF, D_MODEL), BF16)),
        grid=(T // tm, nf),
        in_specs=[
            pl.BlockSpec((tm, D_MODEL), lambda m, f: (m, 0)),
            _layer_spec((1, D_MODEL), layer),
            _mod_spec(layer, 3, tm),
            _mod_spec(layer, 4, tm),
            _mod_spec(layer, 5, tm),
            _layer_spec((1, D_MODEL), layer + 1),
            _mod_spec(layer + 1, 0, tm),
            _mod_spec(layer + 1, 1, tm),
            pl.BlockSpec((None, D_MODEL, tf), lambda m, f: (j, 0, f)),
            pl.BlockSpec((None, D_MODEL, tf), lambda m, f: (j, 0, f)),
            pl.BlockSpec((None, tf, D_MODEL), lambda m, f: (j, f, 0)),
            pl.BlockSpec((None, None) + b13, w13_idx(0, (j,))),
            pl.BlockSpec((None, None) + b13, w13_idx(1, (j,))),
            pl.BlockSpec((None, None) + b2, w2_idx((j,))),
        ],
        out_specs=(pl.BlockSpec((tm, D_MODEL), lambda m, f: (m, 0)),
                   pl.BlockSpec((tm, D_MODEL), lambda m, f: (m, 0)),
                   pl.BlockSpec((None,) + b13, w13_idx(0, ())),
                   pl.BlockSpec((None,) + b13, w13_idx(1, ())),
                   pl.BlockSpec((None,) + b2, w2_idx(()))),
        scratch_shapes=[pltpu.VMEM((tm, D_MODEL), BF16)],
        compiler_params=_cparams(("arbitrary", "arbitrary")),
        name="ffn",
    )(x, norm2_g, mod, mod, mod, norm1_g, mod, mod, w1, w3, w2, e_w1, e_w3, e_w2)


def _router_kernel(x_ref, g_ref, sh_ref, sc_ref, rwa_ref, rwb_ref, rb_ref, h_ref, info_ref, p_ref, cnt_ref,
                   carry_ref, hhi_ref, hlo_ref, upper_ref):
    i = pl.program_id(0)
    tm = TM_ROUTE

    @pl.when(i == 0)
    def _():
        carry_ref[...] = jnp.zeros_like(carry_ref)
        r = lax.broadcasted_iota(I32, (tm, tm), 0)
        c = lax.broadcasted_iota(I32, (tm, tm), 1)
        upper_ref[...] = (r <= c).astype(BF16)

    gain = g_ref[...] * (1.0 + sc_ref[...])
    sh = sh_ref[...]

    def fill(k, carry):
        rows = pl.ds(pl.multiple_of(k * SLAB, SLAB), SLAB)
        y = _modnorm(x_ref[rows, :], gain, sh)
        hi = y.astype(BF16)
        h_ref[rows, :] = y
        hhi_ref[rows, :] = hi
        hlo_ref[rows, :] = (y - hi.astype(F32)).astype(BF16)
        return carry
    lax.fori_loop(0, tm // SLAB, fill, 0, unroll=SLAB_UNROLL)

    t = lax.dot_general(rwa_ref[...], hhi_ref[...], _NT, preferred_element_type=F32)
    u = lax.dot_general(rwb_ref[...], hlo_ref[...], _NT, preferred_element_type=F32)
    logits = t[0:N_EXPERTS] + t[N_EXPERTS:2 * N_EXPERTS] + u[0:N_EXPERTS] + rb_ref[:, 0:1]
    eidx = lax.broadcasted_iota(I32, (N_EXPERTS, tm), 0).astype(F32)
    none = float(N_EXPERTS)
    m1 = jnp.max(logits, axis=0, keepdims=True)
    i1 = jnp.min(jnp.where(logits == m1, eidx, none), axis=0, keepdims=True)
    rest = jnp.where(eidx == i1, -jnp.inf, logits)
    m2 = jnp.max(rest, axis=0, keepdims=True)
    i2 = jnp.min(jnp.where(rest == m2, eidx, none), axis=0, keepdims=True)
    e = jnp.exp(m2 - m1)
    den = 1.0 + e
    sel1 = eidx == i1
    sel2 = eidx == i2
    onehot = (sel1 | sel2).astype(BF16)
    cnt = jnp.dot(onehot, upper_ref[...], preferred_element_type=F32) + carry_ref[:, 0:1]
    r1 = jnp.sum(jnp.where(sel1, cnt, 0.0), axis=0, keepdims=True) - 1.0
    r2 = jnp.sum(jnp.where(sel2, cnt, 0.0), axis=0, keepdims=True) - 1.0
    info_ref[...] = jnp.zeros_like(info_ref)
    info_ref[0:1, :] = i1.astype(I32)
    info_ref[1:2, :] = i2.astype(I32)
    info_ref[2:3, :] = r1.astype(I32)
    info_ref[3:4, :] = r2.astype(I32)
    p_ref[...] = jnp.zeros_like(p_ref)
    p_ref[0:1, :] = 1.0 / den
    p_ref[1:2, :] = e / den
    total = cnt[:, tm - 1:tm]
    carry_ref[...] = jnp.broadcast_to(total, carry_ref.shape)
    cnt_ref[...] = jnp.broadcast_to(total, cnt_ref.shape).astype(I32)


def _router(x, norm_g, mod, layer, j, rw_a, rw_b, rb):
    tm = TM_ROUTE
    nt = T // tm
    return pl.pallas_call(
        _router_kernel,
        out_shape=(jax.ShapeDtypeStruct((T, D_MODEL), F32),
                   jax.ShapeDtypeStruct((nt, 8, tm), I32),
                   jax.ShapeDtypeStruct((nt, 8, tm), F32),
                   jax.ShapeDtypeStruct((N_EXPERTS, 128), I32)),
        grid=(nt,),
        in_specs=[
            pl.BlockSpec((tm, D_MODEL), lambda i: (i, 0)),
            _layer_spec((1, D_MODEL), layer),
            _mod_spec(layer, 3, tm),
            _mod_spec(layer, 4, tm),
            _layer_spec((2 * N_EXPERTS, D_MODEL), j),
            _layer_spec((2 * N_EXPERTS, D_MODEL), j),
            _layer_spec((N_EXPERTS, 128), j),
        ],
        out_specs=(pl.BlockSpec((tm, D_MODEL), lambda i: (i, 0)),
                   pl.BlockSpec((None, 8, tm), lambda i: (i, 0, 0)),
                   pl.BlockSpec((None, 8, tm), lambda i: (i, 0, 0)),
                   pl.BlockSpec((N_EXPERTS, 128), lambda i: (0, 0))),
        scratch_shapes=[pltpu.VMEM((N_EXPERTS, 128), F32), pltpu.VMEM((tm, D_MODEL), BF16),
                        pltpu.VMEM((tm, D_MODEL), BF16), pltpu.VMEM((tm, tm), BF16)],
        compiler_params=_cparams(("arbitrary",)),
        name="router",
    )(x, norm_g, mod, mod, rw_a, rw_b, rb)


def _dispatch_kernel(pos1_ref, pos2_ref, zstart_ref, zcount_ref, h_ref, o_hbm, zrow_ref, sem):
    i = pl.program_id(0)
    td = TD_DISP
    base = i * td

    def row_copy(src_row, dst_row):
        return pltpu.make_async_copy(h_ref.at[pl.ds(src_row, 1)], o_hbm.at[pl.ds(dst_row, 1)], sem)

    def zero_copy(dst_row):
        return pltpu.make_async_copy(zrow_ref, o_hbm.at[pl.ds(dst_row, 1)], sem)

    @pl.when(i == 0)
    def _():
        zrow_ref[...] = jnp.zeros_like(zrow_ref)
        for e in range(N_EXPERTS + 1):
            def zbody(r, carry):
                zero_copy(zstart_ref[e] + r).start()
                return carry
            lax.fori_loop(0, zcount_ref[e], zbody, 0)

            def zwait(r, carry):
                zero_copy(0).wait()
                return carry
            lax.fori_loop(0, zcount_ref[e], zwait, 0)

    for r in range(td):
        row_copy(r, pos1_ref[base + r]).start(priority=0)
        row_copy(r, pos2_ref[base + r]).start(priority=1)

    tile_copy = pltpu.make_async_copy(h_ref, o_hbm.at[pl.ds(0, td)], sem)
    tile_copy.wait()
    tile_copy.wait()


def _dispatch(pos1, pos2, zstart, zcount, h):
    td = TD_DISP
    return pl.pallas_call(
        _dispatch_kernel,
        out_shape=jax.ShapeDtypeStruct((ROWS_CAP, D_MODEL), F32),
        grid_spec=pltpu.PrefetchScalarGridSpec(
            num_scalar_prefetch=4,
            grid=(T // td,),
            in_specs=[pl.BlockSpec((td, D_MODEL), lambda i, *_: (i, 0))],
            out_specs=pl.BlockSpec(memory_space=pl.ANY),
            scratch_shapes=[pltpu.VMEM((1, D_MODEL), F32), pltpu.SemaphoreType.DMA],
        ),
        compiler_params=_cparams(("arbitrary",)),
        name="dispatch",
    )(pos1, pos2, zstart, zcount, h)


def _expert_kernel(te_ref, nact_ref, hs_ref, w1_ref, w3_ref, w2_ref, y_ref, hb_ref):
    i = pl.program_id(0)
    f = pl.program_id(1)
    active = i < nact_ref[0]

    @pl.when(f == 0)
    def _():
        y_ref[...] = jnp.zeros_like(y_ref)

    @pl.when(active & (f == 0))
    def _():
        def body(s, carry):
            r = pl.multiple_of(s * SLAB, SLAB)
            hb_ref[pl.ds(r, SLAB), :] = hs_ref[pl.ds(r, SLAB), :].astype(BF16)
            return carry
        lax.fori_loop(0, TR_EXP // SLAB, body, 0, unroll=SLAB_UNROLL)

    @pl.when(active)
    def _():
        _swiglu_accumulate(hb_ref, w1_ref, w3_ref, w2_ref, y_ref, TR_EXP)


def _experts(tile_expert, n_active, hs, j, w1, w3, w2):
    tr, tf = TR_EXP, TF_EXP
    nf = D_FF // tf

    def row_map(i, f, te, na):
        return (jnp.minimum(i, na[0] - 1), 0)

    def out_map(i, f, te, na):
        return (i, 0)

    def w13_map(i, f, te, na):
        return (j, te[i], 0, jnp.where(i < na[0], f, nf - 1))

    def w2_map(i, f, te, na):
        return (j, te[i], jnp.where(i < na[0], f, nf - 1), 0)

    return pl.pallas_call(
        _expert_kernel,
        out_shape=jax.ShapeDtypeStruct((ROWS_CAP, D_MODEL), F32),
        grid_spec=pltpu.PrefetchScalarGridSpec(
            num_scalar_prefetch=2,
            grid=(NT_EXP, nf),
            in_specs=[
                pl.BlockSpec((tr, D_MODEL), row_map),
                pl.BlockSpec((None, None, D_MODEL, tf), w13_map),
                pl.BlockSpec((None, None, D_MODEL, tf), w13_map),
                pl.BlockSpec((None, None, tf, D_MODEL), w2_map),
            ],
            out_specs=pl.BlockSpec((tr, D_MODEL), out_map),
            scratch_shapes=[pltpu.VMEM((tr, D_MODEL), BF16)],
        ),
        compiler_params=_cparams(("arbitrary", "arbitrary")),
        name="experts",
    )(tile_expert, n_active, hs, w1, w3, w2)


def _gather_residual(pos1_ref, pos2_ref, x_ref, p1_ref, p2_ref, g2_ref, y_hbm, buf_ref, sem, emit):
    i = pl.program_id(0)
    n = pl.num_programs(0)
    tc = TC_COMB
    slot = i % 2

    def start_rows(tile, s, r0):
        base = tile * tc
        for d in range(COMB_SLAB):
            r = r0 + d
            pltpu.make_async_copy(y_hbm.at[pl.ds(pos1_ref[base + r], 1)],
                                  buf_ref.at[s, 0, pl.ds(r, 1)], sem.at[s]).start(priority=0)
            pltpu.make_async_copy(y_hbm.at[pl.ds(pos2_ref[base + r], 1)],
                                  buf_ref.at[s, 1, pl.ds(r, 1)], sem.at[s]).start(priority=1)

    @pl.when(i == 0)
    def _():
        def body(k, carry):
            start_rows(0, 0, k * COMB_SLAB)
            return carry
        lax.fori_loop(0, tc // COMB_SLAB, body, 0)

    pltpu.make_async_copy(y_hbm.at[pl.ds(0, tc)], buf_ref.at[slot, 0], sem.at[slot]).wait()
    pltpu.make_async_copy(y_hbm.at[pl.ds(0, tc)], buf_ref.at[slot, 1], sem.at[slot]).wait()
    g2 = g2_ref[...]

    def tile_pass(prefetch):
        for k in range(tc // COMB_SLAB):
            r0 = k * COMB_SLAB
            if prefetch:
                start_rows(i + 1, 1 - slot, r0)
            rows = slice(r0, r0 + COMB_SLAB)
            mix = (p1_ref[rows, :] * buf_ref[slot, 0, rows, :]
                   + p2_ref[rows, :] * buf_ref[slot, 1, rows, :])
            emit(rows, x_ref[rows, :] + g2 * mix)

    @pl.when(i + 1 < n)
    def _():
        tile_pass(True)

    @pl.when(i + 1 >= n)
    def _():
        tile_pass(False)


def _combine_kernel(pos1_ref, pos2_ref, x_ref, p1_ref, p2_ref, g2_ref, gn_ref, shn_ref, scn_ref, y_hbm,
                    o_ref, hn_ref, buf_ref, sem):
    gain_n, shn = gn_ref[...] * (1.0 + scn_ref[...]), shn_ref[...]

    def emit(rows, xn):
        o_ref[rows, :] = xn
        hn_ref[rows, :] = _modnorm(xn, gain_n, shn).astype(BF16)

    _gather_residual(pos1_ref, pos2_ref, x_ref, p1_ref, p2_ref, g2_ref, y_hbm, buf_ref, sem, emit)


def _combine_final_kernel(pos1_ref, pos2_ref, x_ref, p1_ref, p2_ref, g2_ref, gf_ref, y_hbm,
                          yp_ref, ys_ref, buf_ref, sem, yt_ref, *, n_ctx_tiles):
    gf = gf_ref[...]

    def emit(rows, xn):
        yt_ref[rows, :] = xn * lax.rsqrt(jnp.mean(xn * xn, axis=-1, keepdims=True) + EPS) * gf

    _gather_residual(pos1_ref, pos2_ref, x_ref, p1_ref, p2_ref, g2_ref, y_hbm, buf_ref, sem, emit)
    i = pl.program_id(0)

    @pl.when(i < n_ctx_tiles)
    def _():
        yp_ref[...] = yt_ref[...]

    @pl.when(i >= n_ctx_tiles)
    def _():
        ys_ref[...] = yt_ref[...]


def _combine_call(body, extra_specs, out_shape, out_specs, name, args, extra_scratch=()):
    tc = TC_COMB
    row_tile = pl.BlockSpec((tc, D_MODEL), lambda i, *_: (i, 0))
    col = pl.BlockSpec((tc, 1), lambda i, *_: (i, 0))
    return pl.pallas_call(
        body,
        out_shape=out_shape,
        grid_spec=pltpu.PrefetchScalarGridSpec(
            num_scalar_prefetch=2,
            grid=(T // tc,),
            in_specs=[row_tile, col, col] + extra_specs + [pl.BlockSpec(memory_space=pl.ANY)],
            out_specs=out_specs,
            scratch_shapes=[pltpu.VMEM((2, 2, tc, D_MODEL), F32), pltpu.SemaphoreType.DMA((2,))]
            + list(extra_scratch),
        ),
        compiler_params=_cparams(("arbitrary",)),
        name=name,
    )(*args)


def _combine(pos1, pos2, x, p1, p2, mod, layer, norm1_g, y):
    tc = TC_COMB
    row_tile = pl.BlockSpec((tc, D_MODEL), lambda i, *_: (i, 0))
    return _combine_call(
        _combine_kernel,
        [_mod_spec(layer, 5, tc), _layer_spec((1, D_MODEL), layer + 1),
         _mod_spec(layer + 1, 0, tc), _mod_spec(layer + 1, 1, tc)],
        (jax.ShapeDtypeStruct((T, D_MODEL), F32), jax.ShapeDtypeStruct((T, D_MODEL), BF16)),
        (row_tile, row_tile), "combine",
        (pos1, pos2, x, p1, p2, mod, norm1_g, mod, mod, y))


def _combine_final(pos1, pos2, x, p1, p2, mod, layer, final_g, y):
    tc = TC_COMB
    nct = TP // tc
    return _combine_call(
        functools.partial(_combine_final_kernel, n_ctx_tiles=nct),
        [_mod_spec(layer, 5, tc), pl.BlockSpec((1, D_MODEL), lambda i, *_: (0, 0))],
        (jax.ShapeDtypeStruct((TP, D_MODEL), F32), jax.ShapeDtypeStruct((TS, D_MODEL), F32)),
        (pl.BlockSpec((tc, D_MODEL), lambda i, *_: (jnp.minimum(i, nct - 1), 0)),
         pl.BlockSpec((tc, D_MODEL), lambda i, *_: (jnp.maximum(i - nct, 0), 0))),
        "combine_final",
        (pos1, pos2, x, p1, p2, mod, final_g, y),
        extra_scratch=[pltpu.VMEM((tc, D_MODEL), F32)])


def _moe(x, norm_g, mod, layer, j, rw, rb, w1, w3, w2, norm1_g, final_g):
    h, info, probs, counts = _router(x, norm_g, mod, layer, j, rw[0], rw[1], rb)
    counts = counts[:, 0]
    cap = ((counts + TR_EXP - 1) // TR_EXP) * TR_EXP
    ends = jnp.cumsum(cap)
    starts = ends - cap
    e1 = info[:, 0, :].reshape(T)
    e2 = info[:, 1, :].reshape(T)
    pos1 = starts[e1] + info[:, 2, :].reshape(T)
    pos2 = starts[e2] + info[:, 3, :].reshape(T)
    tile_end = ends // TR_EXP
    n_active = tile_end[-1:]
    tiles = jnp.minimum(jnp.arange(NT_EXP, dtype=I32), n_active[0] - 1)
    tile_expert = jnp.sum((tiles[:, None] >= tile_end[None, :]).astype(I32), axis=1)
    zstart = jnp.concatenate([starts + counts, ends[-1:]])
    zcount = jnp.concatenate([cap - counts, ROWS_CAP - ends[-1:]])
    hs = _dispatch(pos1, pos2, zstart, zcount, h)
    y = _experts(tile_expert, n_active, hs, 0, w1[None], w3[None], w2[None])
    p1 = probs[:, 0, :].reshape(T, 1)
    p2 = probs[:, 1, :].reshape(T, 1)
    if layer == DEPTH - 1:
        return _combine_final(pos1, pos2, x, p1, p2, mod, layer, final_g, y)
    return _combine(pos1, pos2, x, p1, p2, mod, layer, norm1_g, y)


def _grid_pos_embed():
    rows = DEC_SEQ // GRID_W
    r, col = jnp.meshgrid(jnp.arange(rows, dtype=F32), jnp.arange(GRID_W, dtype=F32), indexing='ij')
    r = r.reshape(-1)
    col = col.reshape(-1)
    quarter = D_MODEL // 4
    freq = jnp.exp(-math.log(10000.0) * jnp.arange(quarter, dtype=F32) / quarter)
    ar = r[:, None] * freq
    ac = col[:, None] * freq
    return jnp.concatenate([jnp.sin(ar), jnp.cos(ar), jnp.sin(ac), jnp.cos(ac)], axis=-1)


def kernel(x_prompt, x_sample, state_gla, c, c_ctx, norm1_g, norm2_g, w_mod, b_mod, w_in,
           sgu_ln_g, sgu_ln_b, w_spatial, b_spatial, gla_a2, gla_ab, gla_norm_g,
           w_branch_a, w_branch_b, w_out, ffn_w1, ffn_w3, ffn_w2,
           moe_router, moe_router_b, moe_w1, moe_w3, moe_w2, final_g):
    cond = jnp.zeros((COND_PAD, D_MODEL), F32).at[0].set(c_ctx).at[1:N_COND].set(c)
    mod = _mod_table(cond, w_mod, b_mod).reshape(DEPTH, COND_PAD, 6, 1, D_MODEL)

    lr_end = W_IN_LR + 2 * GLA_RANK
    w_main = jnp.concatenate([w_in[:, :, lr_end:], w_in[:, :, :W_IN_LR]], axis=2).astype(BF16)
    w_lr = jnp.pad(w_in[:, :, W_IN_LR:lr_end], ((0, 0), (0, 0), (0, LR_PAD - 2 * GLA_RANK))).astype(BF16)
    norm1 = norm1_g.reshape(DEPTH, 1, D_MODEL)
    norm2 = norm2_g.reshape(DEPTH, 1, D_MODEL)
    ln_g = sgu_ln_g.reshape(DEPTH, 1, D_A)
    ln_b = sgu_ln_b.reshape(DEPTH, 1, D_A)
    w_s = w_spatial.astype(BF16)
    bs_full = jnp.repeat(jnp.swapaxes(b_spatial, 1, 2), A_GROUP, axis=2)
    a2_pad = jnp.zeros((DEPTH, LR_PAD, GLA_DK), F32)
    a2f = a2_pad.at[:, :GLA_RANK].set(gla_a2[:, 0]).astype(BF16)
    a2b = a2_pad.at[:, GLA_RANK:2 * GLA_RANK].set(gla_a2[:, 1]).astype(BF16)
    ab = gla_ab.reshape(DEPTH, 2, 1, GLA_DK)
    gla_g = gla_norm_g.reshape(DEPTH, 1, GLA_DV)
    w_a = w_branch_a.astype(BF16)
    w_b = w_branch_b.astype(BF16)
    w_o = w_out.astype(BF16)
    f_w1, f_w3, f_w2 = ffn_w1.astype(BF16), ffn_w3.astype(BF16), ffn_w2.astype(BF16)
    rw_t = jnp.swapaxes(moe_router, 1, 2)
    rw_hi = rw_t.astype(BF16)
    rw_lo = (rw_t - rw_hi.astype(F32)).astype(BF16)
    rw = (jnp.concatenate([rw_hi, rw_lo], axis=1),
          jnp.concatenate([rw_hi, jnp.zeros_like(rw_hi)], axis=1))
    rb = jnp.broadcast_to(moe_router_b[:, :, None], moe_router_b.shape + (128,))

    x, h = _embed(x_prompt.reshape(TP, D_MODEL), x_sample.reshape(TS, D_MODEL), _grid_pos_embed(),
                  norm1, mod)
    new_state = jnp.zeros((BATCH, DEPTH, 2, N_GLA_HEADS, GLA_DK_HEAD, GLA_DV_HEAD), F32)
    for l in range(DEPTH):
        proj, lr = _proj(h, l, w_main, w_lr)
        a = _sgu(proj, l, ln_g, ln_b, w_s, bs_full)
        o_f, o_b, new_state = _gla(proj, lr, a2f, a2b, ab, state_gla, new_state, l)
        x = _mixout(a, o_f, o_b, proj, gla_g, x, mod, l, w_a, w_b, w_o)
        j = l // 2
        if l % 2 == 0:
            x, h, m_w1, m_w3, m_w2 = _ffn(x, norm2, norm1, mod, l, j, f_w1, f_w3, f_w2,
                                          moe_w1, moe_w3, moe_w2)
        elif l < DEPTH - 1:
            x, h = _moe(x, norm2, mod, l, j, rw, rb, m_w1, m_w3, m_w2, norm1, final_g[None])
        else:
            y_prompt, y_sample = _moe(x, norm2, mod, l, j, rw, rb, m_w1, m_w3, m_w2, norm1,
                                      final_g[None])

    return (y_prompt.reshape(BATCH, SEQ, D_MODEL), y_sample.reshape(DEC_BATCH, DEC_SEQ, D_MODEL),
            new_state)
```
